```python
import math
import jax, jax.numpy as jnp
from jax import lax
import numpy as np

D_MODEL = 1024
BATCH = 8
SEQ = 4096
DEPTH = 4

GRID_W = 64
CTX_LEN = 256
NORM_EPS = 1e-6
ROPE_THETA = 10000.0
Q_BLOCK = 128

HEAD_DIM = 64
GQA_HEADS = 4
GQA_KV_HEADS = 2
GMLP_GROUPS = 4
GMLP_CH = 64
GMLP_CHUNK = 128
DIFF_HEADS = 4
DIFF_DIM = 32
DN_HEADS = 4
DN_DK = 64
DN_DV = 64
DN_CONV = 3
DN_CHUNK = 64
N_EXPERTS = 32
TOP_K = 4
D_EXPERT = D_MODEL
SWIGLU_LIMIT = 7.0
SWIGLU_ALPHA = 1.702
MOE_BLOCK = 128

GQA_W = GQA_HEADS * HEAD_DIM
GMLP_W = GMLP_GROUPS * GMLP_CH
DIFF_W = DIFF_HEADS * 2 * DIFF_DIM
DN_W = DN_HEADS * DN_DV
DN_QKV = DN_HEADS * (2 * DN_DK + DN_DV)
MIX_WIDTH = GQA_W + GMLP_W + DIFF_W + DN_W
IN_SIZES = (GQA_W, GQA_KV_HEADS * HEAD_DIM, GQA_KV_HEADS * HEAD_DIM,
            GMLP_W, GMLP_W,
            DIFF_W, DIFF_W, DIFF_W,
            DN_QKV, DN_W, 2 * DN_HEADS, 2 * DN_HEADS)
D_IN = sum(IN_SIZES)
IN_OFFSETS = tuple(int(o) for o in np.cumsum(IN_SIZES)[:-1])

kernel_name = 'hybrid_head_group_diffusion_trunk'


def rms_norm(x, g):
    xf = x.astype(jnp.float32)
    y = xf * lax.rsqrt(jnp.mean(xf * xf, axis=-1, keepdims=True) + NORM_EPS)
    return (y * g.astype(jnp.float32)).astype(x.dtype)


def l2_norm(x):
    xf = x.astype(jnp.float32)
    return (xf * lax.rsqrt(jnp.sum(xf * xf, axis=-1, keepdims=True) + NORM_EPS)).astype(x.dtype)


def modulate(h, shift, scale):
    return h * (1.0 + scale) + shift


def heads(t, n):
    return t.reshape(*t.shape[:-1], n, t.shape[-1] // n)


def flip_t(t):
    return jnp.flip(t, axis=1)


def axial_rope(row_pos, col_pos, dim):
    n = dim // 4
    inv = jnp.power(ROPE_THETA, -jnp.arange(n, dtype=jnp.float32) / n)
    ang = jnp.concatenate([row_pos[:, None] * inv, col_pos[:, None] * inv], axis=-1)
    return jnp.cos(ang), jnp.sin(ang)


def apply_rope(x, rope):
    cos, sin = rope
    half = x.shape[-1] // 2
    x1, x2 = x[..., :half], x[..., half:]
    cs, sn = cos[:, None, :], sin[:, None, :]
    return jnp.concatenate([x1 * cs - x2 * sn, x1 * sn + x2 * cs], axis=-1).astype(x.dtype)


def sweep_query_blocks(fn, *qs):
    b, l = qs[0].shape[:2]
    nb = l // Q_BLOCK
    blocks = tuple(q.reshape(b, nb, Q_BLOCK, *q.shape[2:]).swapaxes(0, 1) for q in qs)
    out = lax.map(lambda args: fn(*args), blocks)
    return out.swapaxes(0, 1).reshape(b, l, *out.shape[3:])


def gqa_attend(q, k, v):
    scale = HEAD_DIM ** -0.5
    def one_block(qb):
        s = jnp.einsum('bqhgd,bkhd->bhgqk', qb, k).astype(jnp.float32) * scale
        p = jax.nn.softmax(s, axis=-1).astype(v.dtype)
        return jnp.einsum('bhgqk,bkhd->bqhgd', p, v)
    return sweep_query_blocks(one_block, q)


def diff_attend(q1, q2, k1, k2, v, lam):
    scale = DIFF_DIM ** -0.5
    def one_block(q1b, q2b):
        p1 = jax.nn.softmax(jnp.einsum('bqhd,bkhd->bhqk', q1b, k1).astype(jnp.float32) * scale, axis=-1)
        p2 = jax.nn.softmax(jnp.einsum('bqhd,bkhd->bhqk', q2b, k2).astype(jnp.float32) * scale, axis=-1)
        return jnp.einsum('bhqk,bkhd->bqhd', (p1 - lam * p2).astype(v.dtype), v)
    return sweep_query_blocks(one_block, q1, q2)


def gmlp_spatial(u, v, w_s, b_s, g_v):
    b, l, _ = v.shape
    vn = rms_norm(v, g_v).reshape(b, l // GMLP_CHUNK, GMLP_CHUNK, GMLP_GROUPS, GMLP_CH)
    sp = jnp.einsum('gij,bnjgc->bnigc', w_s, vn) + b_s.T[None, None, :, :, None]
    return u * sp.reshape(b, l, GMLP_W)


def short_conv(x, w):
    ch = x.shape[-1]
    y = lax.conv_general_dilated(x, w[:, None, :].astype(x.dtype), window_strides=(1,),
                                 padding=[(DN_CONV // 2, DN_CONV // 2)],
                                 dimension_numbers=('NWC', 'WIO', 'NWC'), feature_group_count=ch)
    return jax.nn.silu(y)


def gated_delta_chunked(q, k, v, beta, g, s0):
    b, l, h, dk = q.shape
    dv = v.shape[-1]
    n = l // DN_CHUNK

    def chunked(t):
        t = t.astype(jnp.float32).reshape(b, n, DN_CHUNK, h, *t.shape[3:])
        return jnp.moveaxis(t, 3, 1)

    qc, kc, vc, bc, gc = (chunked(t) for t in (q, k, v, beta, g))
    cum_g = jnp.cumsum(gc, axis=-1)
    idx = jnp.arange(DN_CHUNK)
    incl = idx[:, None] >= idx[None, :]
    strict = idx[:, None] > idx[None, :]
    decay = jnp.exp(jnp.where(incl, cum_g[..., :, None] - cum_g[..., None, :], -jnp.inf))
    a_mat = jnp.where(strict, jnp.einsum('bhnid,bhnjd->bhnij', kc, kc) * decay * bc[..., :, None], 0.0)
    t_mat = a_mat + jnp.eye(DN_CHUNK, dtype=jnp.float32)
    rhs = jnp.concatenate([bc[..., None] * vc, (bc * jnp.exp(cum_g))[..., None] * kc], axis=-1)
    sol = lax.linalg.triangular_solve(t_mat, rhs, left_side=True, lower=True, unit_diagonal=True)
    u_mat, w_mat = sol[..., :dv], sol[..., dv:]
    qk = jnp.einsum('bhnid,bhnjd->bhnij', qc, kc) * decay
    xs = tuple(jnp.moveaxis(t, 2, 0) for t in (qc, kc, u_mat, w_mat, cum_g, qk))

    def step(s, inp):
        q_, k_, u_, w_, g_, qk_ = inp
        new_v = u_ - jnp.einsum('bhck,bhvk->bhcv', w_, s)
        o = jnp.exp(g_)[..., None] * jnp.einsum('bhck,bhvk->bhcv', q_, s) + jnp.einsum('bhij,bhjv->bhiv', qk_, new_v)
        g_end = g_[..., -1:]
        s = jnp.exp(g_end)[..., None] * s + jnp.einsum('bhcv,bhck->bhvk', new_v * jnp.exp(g_end - g_)[..., None], k_)
        return s, o

    s_fin, o = lax.scan(step, s0.astype(jnp.float32), xs)
    o = o.transpose(1, 0, 3, 2, 4).reshape(b, l, h, dv)
    return o.astype(v.dtype), s_fin


def gdn_inputs(qkv_raw, b_raw, a_raw, conv_w, a_log, dt_bias):
    b, l, _ = qkv_raw.shape
    qkv = short_conv(qkv_raw, conv_w)
    q, k, v = jnp.split(qkv, [DN_HEADS * DN_DK, 2 * DN_HEADS * DN_DK], axis=-1)
    q = l2_norm(heads(q, DN_HEADS)) * (DN_DK ** -0.5)
    k = l2_norm(heads(k, DN_HEADS))
    v = heads(v, DN_HEADS)
    beta = jax.nn.sigmoid(b_raw.astype(jnp.float32)).reshape(b, l, 2, DN_HEADS)
    g = -jnp.exp(a_log) * jax.nn.softplus(a_raw.astype(jnp.float32).reshape(b, l, 2, DN_HEADS) + dt_bias)
    return q, k, v, beta, g


def gdn_bidirectional(ctx_in, lat_in, need_ctx):
    qc, kc, vc, bc, gc = ctx_in
    ql, kl, vl, bl, gl = lat_in
    s0 = jnp.zeros((qc.shape[0], DN_HEADS, DN_DV, DN_DK), jnp.float32)
    o_cf, s_f = gated_delta_chunked(qc, kc, vc, bc[:, :, 0], gc[:, :, 0], s0)
    o_cb, s_b = gated_delta_chunked(flip_t(qc), flip_t(kc), flip_t(vc), flip_t(bc[:, :, 1]), flip_t(gc[:, :, 1]), s0)
    o_lf, _ = gated_delta_chunked(ql, kl, vl, bl[:, :, 0], gl[:, :, 0], s_f)
    o_lb, _ = gated_delta_chunked(flip_t(ql), flip_t(kl), flip_t(vl), flip_t(bl[:, :, 1]), flip_t(gl[:, :, 1]), s_b)
    o_l = o_lf + flip_t(o_lb)
    o_c = (o_cf + flip_t(o_cb)) if need_ctx else None
    return o_c, o_l


def clamped_swiglu(hgu):
    gate, up = jnp.split(hgu, 2, axis=-1)
    gate = jnp.minimum(gate, SWIGLU_LIMIT)
    up = jnp.clip(up, -SWIGLU_LIMIT, SWIGLU_LIMIT)
    return gate * jax.nn.sigmoid(SWIGLU_ALPHA * gate) * (up + 1.0)


def moe_ffn(h, w_r, b_r, w_up, b_up, w_down, b_down):
    t, d = h.shape
    logits = (h @ w_r + b_r).astype(jnp.float32)
    top_val, top_idx = lax.top_k(logits, TOP_K)
    gates = jax.nn.softmax(top_val, axis=-1)
    m = t * TOP_K
    e_flat = top_idx.reshape(m)
    tok_flat = jnp.arange(m, dtype=jnp.int32) // TOP_K
    order = jnp.argsort(e_flat)
    e_sorted = e_flat[order]
    counts = jnp.bincount(e_flat, length=N_EXPERTS)
    starts = jnp.cumsum(counts) - counts
    padded = (counts + MOE_BLOCK - 1) // MOE_BLOCK * MOE_BLOCK
    ends_p = jnp.cumsum(padded)
    starts_p = ends_p - padded
    dest = starts_p[e_sorted] + jnp.arange(m, dtype=jnp.int32) - starts[e_sorted]
    n_blocks = (m + N_EXPERTS * (MOE_BLOCK - 1) + MOE_BLOCK - 1) // MOE_BLOCK
    p = n_blocks * MOE_BLOCK
    slot_tok = jnp.full((p,), t, jnp.int32).at[dest].set(tok_flat[order])
    slot_gate = jnp.zeros((p,), jnp.float32).at[dest].set(gates.reshape(m)[order])
    block_expert = jnp.minimum(jnp.searchsorted(ends_p, jnp.arange(n_blocks) * MOE_BLOCK, side='right'), N_EXPERTS - 1)
    h_pad = jnp.concatenate([h, jnp.zeros((1, d), h.dtype)], axis=0)

    def expert_block(args):
        tok, e = args
        xb = h_pad[tok]
        hid = clamped_swiglu(xb @ w_up[e] + b_up[e])
        return hid @ w_down[e] + b_down[e]

    ys = lax.map(expert_block, (slot_tok.reshape(n_blocks, MOE_BLOCK), block_expert))
    ys = ys.reshape(p, d) * slot_gate[:, None].astype(h.dtype)
    return jnp.zeros((t + 1, d), h.dtype).at[slot_tok].add(ys)[:t]


def hybrid_mixer(h_c, h_l, need_ctx, rope_a, rope_d, lam_init, w_in, w_out, q_norm, k_norm,
                 v_norm, w_s, b_s, lq1, lk1, lq2, lk2, subln, conv_w, a_log, dt_bias, out_norm):
    b, l, _ = h_l.shape
    lc = h_c.shape[1]
    grp = GQA_HEADS // GQA_KV_HEADS
    (aq_l, ak_l, av_l, bu_l, bv_l, cq_l, ck_l, cv_l, dqkv_l, dz_l, db_l, da_l) = jnp.split(h_l @ w_in, IN_OFFSETS, axis=-1)
    (aq_c, ak_c, av_c, bu_c, bv_c, cq_c, ck_c, cv_c, dqkv_c, dz_c, db_c, da_c) = jnp.split(h_c @ w_in, IN_OFFSETS, axis=-1)

    qa_l = apply_rope(rms_norm(heads(aq_l, GQA_HEADS), q_norm), rope_a)
    ka_l = apply_rope(rms_norm(heads(ak_l, GQA_KV_HEADS), k_norm), rope_a)
    ka_c = rms_norm(heads(ak_c, GQA_KV_HEADS), k_norm)
    va_l, va_c = heads(av_l, GQA_KV_HEADS), heads(av_c, GQA_KV_HEADS)
    out_a_l = gqa_attend(qa_l.reshape(b, l, GQA_KV_HEADS, grp, HEAD_DIM),
                         jnp.concatenate([ka_l, ka_c], axis=1), jnp.concatenate([va_l, va_c], axis=1)).reshape(b, l, GQA_W)

    out_b_l = gmlp_spatial(jax.nn.gelu(bu_l), jax.nn.gelu(bv_l), w_s, b_s, v_norm)

    lam = (jnp.exp(jnp.sum(lq1 * lk1).astype(jnp.float32)) - jnp.exp(jnp.sum(lq2 * lk2).astype(jnp.float32)) + lam_init)
    qd_l, kd_l, kd_c = heads(cq_l, DIFF_HEADS), heads(ck_l, DIFF_HEADS), heads(ck_c, DIFF_HEADS)
    q1_l, q2_l = apply_rope(qd_l[..., :DIFF_DIM], rope_d), apply_rope(qd_l[..., DIFF_DIM:], rope_d)
    k1_all = jnp.concatenate([apply_rope(kd_l[..., :DIFF_DIM], rope_d), kd_c[..., :DIFF_DIM]], axis=1)
    k2_all = jnp.concatenate([apply_rope(kd_l[..., DIFF_DIM:], rope_d), kd_c[..., DIFF_DIM:]], axis=1)
    vd_c = heads(cv_c, DIFF_HEADS)
    vd_all = jnp.concatenate([heads(cv_l, DIFF_HEADS), vd_c], axis=1)
    od_l = diff_attend(q1_l, q2_l, k1_all, k2_all, vd_all, lam)
    out_c_l = (rms_norm(od_l, subln) * (1.0 - lam_init)).reshape(b, l, DIFF_W)

    ctx_in = gdn_inputs(dqkv_c, db_c, da_c, conv_w, a_log, dt_bias)
    lat_in = gdn_inputs(dqkv_l, db_l, da_l, conv_w, a_log, dt_bias)
    o_c, o_l = gdn_bidirectional(ctx_in, lat_in, need_ctx)
    out_d_l = (rms_norm(o_l, out_norm) * jax.nn.silu(heads(dz_l, DN_HEADS))).reshape(b, l, DN_W)

    y_l = jnp.concatenate([out_a_l, out_b_l, out_c_l, out_d_l], axis=-1) @ w_out
    if not need_ctx:
        return None, y_l

    qa_c = rms_norm(heads(aq_c, GQA_HEADS), q_norm).reshape(b, lc, GQA_KV_HEADS, grp, HEAD_DIM)
    out_a_c = gqa_attend(qa_c, ka_c, va_c).reshape(b, lc, GQA_W)
    out_b_c = gmlp_spatial(jax.nn.gelu(bu_c), jax.nn.gelu(bv_c), w_s, b_s, v_norm)
    qd_c = heads(cq_c, DIFF_HEADS)
    od_c = diff_attend(qd_c[..., :DIFF_DIM], qd_c[..., DIFF_DIM:], kd_c[..., :DIFF_DIM], kd_c[..., DIFF_DIM:], vd_c, lam)
    out_c_c = (rms_norm(od_c, subln) * (1.0 - lam_init)).reshape(b, lc, DIFF_W)
    out_d_c = (rms_norm(o_c, out_norm) * jax.nn.silu(heads(dz_c, DN_HEADS))).reshape(b, lc, DN_W)
    y_c = jnp.concatenate([out_a_c, out_b_c, out_c_c, out_d_c], axis=-1) @ w_out
    return y_c, y_l


def setup_inputs(seed: int = 0) -> dict:
    key = jax.random.key(seed)
    ks = jax.random.split(key, 32)
    f32 = jnp.float32

    def nrm(k, shape, scale=1.0):
        return jax.random.normal(k, shape, f32) * scale

    def gain(k, shape):
        return 1.0 + 0.02 * jax.random.normal(k, shape, f32)

    dt = jnp.exp(jax.random.uniform(ks[23], (DEPTH, 2, DN_HEADS), f32, math.log(1e-3), math.log(1e-1)))
    return {
        'x': nrm(ks[0], (BATCH, SEQ, D_MODEL)),
        'c': nrm(ks[1], (BATCH, D_MODEL)),
        'ctx': nrm(ks[2], (BATCH, CTX_LEN, D_MODEL)),
        'c_ctx': nrm(ks[3], (D_MODEL,), 0.5),
        'w_ada': nrm(ks[4], (DEPTH, D_MODEL, 6 * D_MODEL), 0.5 * D_MODEL ** -0.5),
        'b_ada': nrm(ks[5], (DEPTH, 6 * D_MODEL), 0.01),
        'norm_mix': gain(ks[6], (DEPTH, D_MODEL)),
        'norm_ffn': gain(ks[7], (DEPTH, D_MODEL)),
        'w_in': nrm(ks[8], (DEPTH, D_MODEL, D_IN), D_MODEL ** -0.5),
        'w_out': nrm(ks[9], (DEPTH, MIX_WIDTH, D_MODEL), MIX_WIDTH ** -0.5),
        'gqa_q_norm': gain(ks[10], (DEPTH, HEAD_DIM)),
        'gqa_k_norm': gain(ks[11], (DEPTH, HEAD_DIM)),
        'gmlp_v_norm': gain(ks[12], (DEPTH, GMLP_W)),
        'gmlp_w_s': nrm(ks[13], (DEPTH, GMLP_GROUPS, GMLP_CHUNK, GMLP_CHUNK), GMLP_CHUNK ** -0.5),
        'gmlp_b_s': gain(ks[14], (DEPTH, GMLP_GROUPS, GMLP_CHUNK)),
        'diff_lambda_q1': nrm(ks[15], (DEPTH, DIFF_DIM), 0.1),
        'diff_lambda_k1': nrm(ks[16], (DEPTH, DIFF_DIM), 0.1),
        'diff_lambda_q2': nrm(ks[17], (DEPTH, DIFF_DIM), 0.1),
        'diff_lambda_k2': nrm(ks[18], (DEPTH, DIFF_DIM), 0.1),
        'diff_subln': gain(ks[19], (DEPTH, 2 * DIFF_DIM)),
        'dn_conv_w': nrm(ks[20], (DEPTH, DN_CONV, DN_QKV), DN_CONV ** -0.5),
        'dn_a_log': jnp.log(jax.random.uniform(ks[21], (DEPTH, 2, DN_HEADS), f32, 1.0, 16.0)),
        'dn_dt_bias': dt + jnp.log(-jnp.expm1(-dt)),
        'dn_out_norm': gain(ks[22], (DEPTH, DN_DV)),
        'router_w': nrm(ks[24], (DEPTH, D_MODEL, N_EXPERTS), D_MODEL ** -0.5),
        'router_b': nrm(ks[25], (DEPTH, N_EXPERTS), 0.01),
        'exp_w_up': nrm(ks[26], (DEPTH, N_EXPERTS, D_MODEL, 2 * D_EXPERT), D_MODEL ** -0.5),
        'exp_b_up': nrm(ks[27], (DEPTH, N_EXPERTS, 2 * D_EXPERT), 0.01),
        'exp_w_down': nrm(ks[28], (DEPTH, N_EXPERTS, D_EXPERT, D_MODEL), D_EXPERT ** -0.5),
        'exp_b_down': nrm(ks[29], (DEPTH, N_EXPERTS, D_MODEL), 0.01),
        'final_norm': gain(ks[30], (D_MODEL,)),
    }


def reference(x, c, ctx, c_ctx, w_ada, b_ada, norm_mix, norm_ffn, w_in, w_out, gqa_q_norm, gqa_k_norm,
              gmlp_v_norm, gmlp_w_s, gmlp_b_s, diff_lambda_q1, diff_lambda_k1, diff_lambda_q2, diff_lambda_k2,
              diff_subln, dn_conv_w, dn_a_log, dn_dt_bias, dn_out_norm, router_w, router_b,
              exp_w_up, exp_b_up, exp_w_down, exp_b_down, final_norm):
    b, l, d = x.shape
    lc = ctx.shape[1]
    rows = l // GRID_W
    r_idx, c_idx = jnp.meshgrid(jnp.arange(rows), jnp.arange(GRID_W), indexing='ij')
    row_pos = r_idx.reshape(-1).astype(jnp.float32)
    col_pos = c_idx.reshape(-1).astype(jnp.float32)
    rope_a = axial_rope(row_pos, col_pos, HEAD_DIM)
    rope_d = axial_rope(row_pos, col_pos, DIFF_DIM)
    silu_c = jax.nn.silu(c)
    silu_c_ctx = jax.nn.silu(c_ctx)

    for layer in range(DEPTH):
        need_ctx = layer < DEPTH - 1
        lam_init = 0.8 - 0.6 * math.exp(-0.3 * layer)
        mod = jnp.split(silu_c @ w_ada[layer] + b_ada[layer], 6, axis=-1)
        sh1, sc1, g1, sh2, sc2, g2 = (m[:, None, :] for m in mod)
        csh1, csc1, cg1, csh2, csc2, cg2 = jnp.split(silu_c_ctx @ w_ada[layer] + b_ada[layer], 6)

        h_l = modulate(rms_norm(x, norm_mix[layer]), sh1, sc1)
        h_c = modulate(rms_norm(ctx, norm_mix[layer]), csh1, csc1)
        y_c, y_l = hybrid_mixer(h_c, h_l, need_ctx, rope_a, rope_d, lam_init, w_in[layer], w_out[layer],
                                gqa_q_norm[layer], gqa_k_norm[layer], gmlp_v_norm[layer], gmlp_w_s[layer],
                                gmlp_b_s[layer], diff_lambda_q1[layer], diff_lambda_k1[layer],
                                diff_lambda_q2[layer], diff_lambda_k2[layer], diff_subln[layer],
                                dn_conv_w[layer], dn_a_log[layer], dn_dt_bias[layer], dn_out_norm[layer])
        x = x + g1 * y_l
        f_l = modulate(rms_norm(x, norm_ffn[layer]), sh2, sc2)
        if need_ctx:
            ctx = ctx + cg1 * y_c
            f_c = modulate(rms_norm(ctx, norm_ffn[layer]), csh2, csc2)
            tokens = jnp.concatenate([f_l.reshape(b * l, d), f_c.reshape(b * lc, d)], axis=0)
        else:
            tokens = f_l.reshape(b * l, d)
        ffn = moe_ffn(tokens, router_w[layer], router_b[layer], exp_w_up[layer], exp_b_up[layer],
                      exp_w_down[layer], exp_b_down[layer])
        x = x + g2 * ffn[:b * l].reshape(b, l, d)
        if need_ctx:
            ctx = ctx + cg2 * ffn[b * l:].reshape(b, lc, d)

    return rms_norm(x, final_norm)
```

```python
import functools
import math

import numpy as np
import jax
import jax.numpy as jnp
from jax import lax
from jax.experimental import pallas as pl
from jax.experimental.pallas import tpu as pltpu

F32 = jnp.float32
BF16 = jnp.bfloat16
HIGHEST = lax.Precision.HIGHEST

GRID_W = 64
NORM_EPS = 1e-6
ROPE_THETA = 10000.0
HEAD_DIM = 64
GQA_HEADS = 4
GQA_KV_HEADS = 2
GMLP_GROUPS = 4
GMLP_CH = 64
GMLP_CHUNK = 128
DIFF_HEADS = 4
DIFF_DIM = 32
DN_HEADS = 4
DN_DK = 64
DN_DV = 64
DN_CHUNK = 64
N_EXPERTS = 32
TOP_K = 4
SWIGLU_LIMIT = 7.0
SWIGLU_ALPHA = 1.702

MIXER_W = 256
TQ = 256
GDN_PAIR = 2 * DN_CHUNK
MOE_TM = 1024
MOE_RB = 160
VMEM_LIMIT = 56 * 1024 * 1024

_SEGS = ("aq", "aqs", "ak", "aks", "av", "bu", "bv", "cq", "cqs", "ck", "cks", "cv")
OFF = {name: i * MIXER_W for i, name in enumerate(_SEGS)}
OFF["dqkv"] = len(_SEGS) * MIXER_W
OFF["dz"] = OFF["dqkv"] + 3 * MIXER_W
OFF["dba"] = OFF["dz"] + MIXER_W
W_EXT = OFF["dba"] + 128


def _dot(a, b):
    return jnp.dot(a, b, preferred_element_type=F32)


def _dot_nt(a, b):
    return lax.dot_general(a, b, (((1,), (1,)), ((), ())), preferred_element_type=F32)


def _dot_tn(a, b):
    return lax.dot_general(a, b, (((0,), (0,)), ((), ())), preferred_element_type=F32)


def _split3(x):
    hi = x.astype(BF16)
    r1 = x - hi.astype(F32)
    mid = r1.astype(BF16)
    lo = (r1 - mid.astype(F32)).astype(BF16)
    return hi, mid, lo


def _dot_sel_r(x, sel):
    hi, mid, lo = _split3(x)
    return _dot(hi, sel) + _dot(mid, sel) + _dot(lo, sel)


def _dot_sel_l(sel, x):
    hi, mid, lo = _split3(x)
    return _dot(sel, hi) + _dot(sel, mid) + _dot(sel, lo)


def _group_sumsq(x, g_same):
    x2 = x * x
    hi = x2.astype(BF16)
    lo = (x2 - hi.astype(F32)).astype(BF16)
    return _dot(hi, g_same) + _dot(lo, g_same)


def _onehot(cond):
    return jnp.where(cond, 1.0, 0.0).astype(BF16)


def _keep(cond, x):
    return jnp.where(cond, x.astype(F32), 0.0).astype(BF16)


def _silu(x):
    return x * jax.nn.sigmoid(x)


def _softplus(x):
    return jnp.maximum(x, 0.0) + jnp.log1p(jnp.exp(-jnp.abs(x)))


def _params(sem):
    return pltpu.CompilerParams(dimension_semantics=sem, vmem_limit_bytes=VMEM_LIMIT)


def _mod_kernel(c_ref, w_ref, b_ref, o_ref):
    s = _silu(c_ref[...])
    o_ref[0, 0] = jnp.dot(s, w_ref[0], precision=HIGHEST, preferred_element_type=F32) + b_ref[0]


def _modulation(c_all, w_ada, b_ada):
    depth, d, _ = w_ada.shape
    r = c_all.shape[0]
    return pl.pallas_call(
        _mod_kernel,
        grid=(depth, 6),
        in_specs=[
            pl.BlockSpec((r, d), lambda l, j: (0, 0)),
            pl.BlockSpec((1, d, d), lambda l, j: (l, 0, j)),
            pl.BlockSpec((1, 1, d), lambda l, j: (l, 0, j)),
        ],
        out_specs=pl.BlockSpec((1, 1, r, d), lambda l, j: (l, j, 0, 0)),
        out_shape=jax.ShapeDtypeStruct((depth, 6, r, d), F32),
        compiler_params=_params(("arbitrary", "arbitrary")),
        name="adaln_mod",
    )(c_all, w_ada, b_ada.reshape(depth, 1, 6 * d))


def _inproj_kernel(*refs, has_prev, scale_a, scale_d):
    if has_prev:
        x_ref, ffn_ref, modp_ref, refs = refs[0], refs[1], refs[2], refs[3:]
    else:
        x_ref, refs = refs[0], refs[1:]
    (mod_ref, gmix_ref, w_ref, vec_ref, g64_ref, rope_ref, ws_ref, bst_ref) = refs[:8]
    outs = refs[8:]
    if has_prev:
        xo_ref, outs = outs[0], outs[1:]
    (qa_ref, ka_ref, va_ref, ob_ref, qd_ref, kd_ref, vd_ref, dqkv_ref, dz_ref, dba_ref) = outs

    x = x_ref[0]
    if has_prev:
        x = x + modp_ref[0, 0, 5:6, :] * ffn_ref[0]
        xo_ref[0] = x
    ms = jnp.mean(x * x, axis=-1, keepdims=True)
    xn = x * lax.rsqrt(ms + NORM_EPS) * gmix_ref[...]
    h = xn * (1.0 + mod_ref[0, 0, 1:2, :]) + mod_ref[0, 0, 0:1, :]
    p = _dot(h.astype(BF16), w_ref[...])

    def seg(name, width=MIXER_W):
        return p[:, OFF[name]:OFF[name] + width]

    g64 = g64_ref[...]
    cos_a, sin_a, cos_d, sin_d = rope_ref[0], rope_ref[1], rope_ref[2], rope_ref[3]

    def norm_rope(x0, xs, gain, gain_s, scale):
        r = lax.rsqrt(_group_sumsq(x0, g64) * (1.0 / HEAD_DIM) + NORM_EPS)
        return ((x0 * r * gain) * cos_a + (xs * r * gain_s) * sin_a) * scale

    qa_ref[0] = norm_rope(seg("aq"), seg("aqs"), vec_ref[0:1, :], vec_ref[1:2, :], scale_a).astype(BF16)
    ka_ref[0] = norm_rope(seg("ak"), seg("aks"), vec_ref[2:3, :], vec_ref[3:4, :], 1.0).astype(BF16)
    va_ref[0] = seg("av").astype(BF16)

    u = jax.nn.gelu(seg("bu"))
    v = jax.nn.gelu(seg("bv"))
    vn = v * lax.rsqrt(jnp.mean(v * v, axis=-1, keepdims=True) + NORM_EPS) * vec_ref[4:5, :]
    lane_grp = lax.broadcasted_iota(jnp.int32, (GMLP_CHUNK, MIXER_W), 1) // GMLP_CH
    for ci in range(TQ // GMLP_CHUNK):
        rows = slice(ci * GMLP_CHUNK, (ci + 1) * GMLP_CHUNK)
        vc = vn[rows]
        sp = bst_ref[...]
        for g in range(GMLP_GROUPS):
            sp = sp + _dot(ws_ref[g], jnp.where(lane_grp == g, vc, 0.0).astype(BF16))
        ob_ref[0, rows, :] = (u[rows] * sp).astype(BF16)

    qd_ref[0] = ((seg("cq") * cos_d + seg("cqs") * sin_d) * scale_d).astype(BF16)
    kd_ref[0] = (seg("ck") * cos_d + seg("cks") * sin_d).astype(BF16)
    vd_ref[0] = seg("cv").astype(BF16)

    dqkv_ref[0] = seg("dqkv", 3 * MIXER_W)
    dz_ref[0] = seg("dz")
    dba_ref[0] = seg("dba", 128)


def _inproj(layer, x, prev, mod_t, gmix, w_ext, vecs, g64, rope, ws, bst, n_ctx_tiles):
    b, s, d = x.shape
    nt = s // TQ
    ctx_row = b

    def mod_map(l):
        return lambda i, bb: (l, jnp.where(i < n_ctx_tiles, ctx_row, bb), 0, 0)

    tok = lambda i, bb: (bb, i, 0)
    const2 = lambda i, bb: (0, 0)
    const3 = lambda i, bb: (0, 0, 0)
    in_specs = [pl.BlockSpec((1, TQ, d), tok)]
    args = [x]
    if prev is not None:
        in_specs += [pl.BlockSpec((1, TQ, d), tok), pl.BlockSpec((1, 1, 6, d), mod_map(layer - 1))]
        args += [prev, mod_t]
    in_specs += [
        pl.BlockSpec((1, 1, 6, d), mod_map(layer)),
        pl.BlockSpec((1, d), const2),
        pl.BlockSpec((d, W_EXT), const2),
        pl.BlockSpec((8, MIXER_W), const2),
        pl.BlockSpec((MIXER_W, MIXER_W), const2),
        pl.BlockSpec((4, TQ, MIXER_W), lambda i, bb: (0, i, 0)),
        pl.BlockSpec((GMLP_GROUPS, GMLP_CHUNK, GMLP_CHUNK), const3),
        pl.BlockSpec((GMLP_CHUNK, MIXER_W), const2),
    ]
    args += [mod_t, gmix, w_ext, vecs, g64, rope, ws, bst]
    bf = lambda w: jax.ShapeDtypeStruct((b, s, w), BF16)
    ff = lambda w: jax.ShapeDtypeStruct((b, s, w), F32)
    out_shape = [bf(MIXER_W)] * 7 + [ff(3 * MIXER_W), ff(MIXER_W), ff(128)]
    out_specs = [pl.BlockSpec((1, TQ, MIXER_W), tok)] * 7 + [
        pl.BlockSpec((1, TQ, 3 * MIXER_W), tok), pl.BlockSpec((1, TQ, MIXER_W), tok), pl.BlockSpec((1, TQ, 128), tok)]
    if prev is not None:
        out_shape = [ff(d)] + out_shape
        out_specs = [pl.BlockSpec((1, TQ, d), tok)] + out_specs
    outs = pl.pallas_call(
        functools.partial(_inproj_kernel, has_prev=prev is not None,
                          scale_a=HEAD_DIM ** -0.5, scale_d=DIFF_DIM ** -0.5),
        grid=(nt, b), in_specs=in_specs, out_specs=out_specs, out_shape=out_shape,
        compiler_params=_params(("arbitrary", "arbitrary")), name="in_proj",
    )(*args)
    if prev is not None:
        return outs[0], outs[1:]
    return x, outs


def _softmax_pv(qm, k, v):
    s = _dot_nt(qm, k)
    m = jnp.max(s, axis=-1, keepdims=True)
    e = jnp.exp(s - m)
    den = jnp.sum(e, axis=-1, keepdims=True)
    return _dot(e.astype(BF16), v) / den


def _gqa_kernel(q_ref, k_ref, v_ref, o_ref, *, n_ctx_tiles, lc):
    i = pl.program_id(1)
    lane_head = lax.broadcasted_iota(jnp.int32, (TQ, MIXER_W), 1) // HEAD_DIM

    def attend(k, v):
        q = q_ref[0]
        acc = jnp.zeros((TQ, MIXER_W), F32)
        for hd in range(GQA_HEADS):
            mine = lane_head == hd
            o = _softmax_pv(_keep(mine, q), k, v)
            acc = jnp.where(mine, o, acc)
        o_ref[0] = acc.astype(BF16)

    @pl.when(i < n_ctx_tiles)
    def _():
        attend(k_ref[0, :lc, :], v_ref[0, :lc, :])

    @pl.when(i >= n_ctx_tiles)
    def _():
        attend(k_ref[0], v_ref[0])


def _diff_kernel(q_ref, k_ref, v_ref, lam_ref, sub_ref, g64_ref, o_ref, *, n_ctx_tiles, lc, lam_init):
    i = pl.program_id(1)
    lane = lax.broadcasted_iota(jnp.int32, (TQ, MIXER_W), 1)
    lane_head = lane // (2 * DIFF_DIM)
    lane_map = lane // DIFF_DIM
    lp = lam_ref[...]
    lam = (jnp.exp(jnp.sum(lp[0:1] * lp[1:2], axis=-1, keepdims=True))
           - jnp.exp(jnp.sum(lp[2:3] * lp[3:4], axis=-1, keepdims=True)) + lam_init)

    def attend(k, v):
        q = q_ref[0]
        acc = jnp.zeros((TQ, MIXER_W), F32)
        for hd in range(DIFF_HEADS):
            o1 = _softmax_pv(_keep(lane_map == 2 * hd, q), k, v)
            o2 = _softmax_pv(_keep(lane_map == 2 * hd + 1, q), k, v)
            acc = jnp.where(lane_head == hd, o1 - lam * o2, acc)
        r = lax.rsqrt(_group_sumsq(acc, g64_ref[...]) * (1.0 / (2 * DIFF_DIM)) + NORM_EPS)
        o_ref[0] = (acc * r * sub_ref[...] * (1.0 - lam_init)).astype(BF16)

    @pl.when(i < n_ctx_tiles)
    def _():
        attend(k_ref[0, :lc, :], v_ref[0, :lc, :])

    @pl.when(i >= n_ctx_tiles)
    def _():
        attend(k_ref[0], v_ref[0])


def _attention(kind, q, k, v, n_ctx_tiles, lc, extra=(), lam_init=0.0):
    b, s, w = q.shape
    nt = s // TQ
    tok = lambda bb, i: (bb, i, 0)
    row = lambda bb, i: (bb, 0, 0)
    in_specs = [pl.BlockSpec((1, TQ, w), tok), pl.BlockSpec((1, s, w), row), pl.BlockSpec((1, s, w), row)]
    if kind == "gqa":
        body = functools.partial(_gqa_kernel, n_ctx_tiles=n_ctx_tiles, lc=lc)
    else:
        body = functools.partial(_diff_kernel, n_ctx_tiles=n_ctx_tiles, lc=lc, lam_init=lam_init)
        in_specs += [pl.BlockSpec(e.shape, lambda bb, i: (0, 0)) for e in extra]
    return pl.pallas_call(
        body, grid=(b, nt), in_specs=in_specs, out_specs=pl.BlockSpec((1, TQ, w), tok),
        out_shape=jax.ShapeDtypeStruct((b, s, w), BF16),
        compiler_params=_params(("arbitrary", "arbitrary")), name=kind + "_attention",
    )(q, k, v, *extra)


def _gdn_prep_kernel(x_ref, xp_ref, xn_ref, ba_ref, cw_ref, av_ref, g64_ref,
                     q_ref, k_ref, v_ref, kt_ref, bg_ref, bgt_ref, *, n_ctx_tiles, n_tiles):
    i = pl.program_id(0)
    x = x_ref[0]
    has_prev = jnp.where((i != 0) & (i != n_ctx_tiles), 1.0, 0.0)
    has_next = jnp.where((i != n_ctx_tiles - 1) & (i != n_tiles - 1), 1.0, 0.0)
    row = lax.broadcasted_iota(jnp.int32, x.shape, 0)
    x_m1 = jnp.where(row == 0, xp_ref[0, 7:8, :] * has_prev, pltpu.roll(x, 1, 0))
    x_p1 = jnp.where(row == TQ - 1, xn_ref[0, 0:1, :] * has_next, pltpu.roll(x, TQ - 1, 0))
    y = _silu(cw_ref[0:1, :] * x_m1 + cw_ref[1:2, :] * x + cw_ref[2:3, :] * x_p1)
    q, k, v = y[:, :MIXER_W], y[:, MIXER_W:2 * MIXER_W], y[:, 2 * MIXER_W:]
    g64 = g64_ref[...]
    qn = q * lax.rsqrt(_group_sumsq(q, g64) + NORM_EPS) * (DN_DK ** -0.5)
    kn = k * lax.rsqrt(_group_sumsq(k, g64) + NORM_EPS)
    q_ref[0] = qn.astype(BF16)
    k_ref[0] = kn.astype(BF16)
    v_ref[0] = v.astype(BF16)
    kt_ref[0] = kn.T.astype(BF16)
    ba = ba_ref[0]
    lane = lax.broadcasted_iota(jnp.int32, ba.shape, 1)
    beta = jax.nn.sigmoid(ba)
    g = -jnp.exp(av_ref[0:1, :]) * _softplus(ba + av_ref[1:2, :])
    bg = jnp.where(lane < 2 * DN_HEADS, beta, jnp.where(lane < 4 * DN_HEADS, g, 0.0))
    bg_ref[0] = bg
    bgt_ref[0] = bg.T


def _gdn_prep(dqkv, dba, conv_w, avec, g64, n_ctx_tiles):
    b, s, w3 = dqkv.shape
    nt = s // TQ
    nb8 = s // 8
    tok = lambda i, bb: (bb, i, 0)
    tokt = lambda i, bb: (bb, 0, i)
    const2 = lambda i, bb: (0, 0)
    outs = pl.pallas_call(
        functools.partial(_gdn_prep_kernel, n_ctx_tiles=n_ctx_tiles, n_tiles=nt),
        grid=(nt, b),
        in_specs=[
            pl.BlockSpec((1, TQ, w3), tok),
            pl.BlockSpec((1, 8, w3), lambda i, bb: (bb, jnp.maximum(i * (TQ // 8) - 1, 0), 0)),
            pl.BlockSpec((1, 8, w3), lambda i, bb: (bb, jnp.minimum((i + 1) * (TQ // 8), nb8 - 1), 0)),
            pl.BlockSpec((1, TQ, 128), tok),
            pl.BlockSpec((3, w3), const2),
            pl.BlockSpec((2, 128), const2),
            pl.BlockSpec((MIXER_W, MIXER_W), const2),
        ],
        out_specs=[pl.BlockSpec((1, TQ, MIXER_W), tok)] * 3 + [
            pl.BlockSpec((1, MIXER_W, TQ), tokt), pl.BlockSpec((1, TQ, 128), tok), pl.BlockSpec((1, 128, TQ), tokt)],
        out_shape=[jax.ShapeDtypeStruct((b, s, MIXER_W), BF16)] * 3 + [
            jax.ShapeDtypeStruct((b, MIXER_W, s), BF16), jax.ShapeDtypeStruct((b, s, 128), F32),
            jax.ShapeDtypeStruct((b, 128, s), F32)],
        compiler_params=_params(("arbitrary", "arbitrary")), name="gdn_prep",
    )(dqkv, dqkv, dqkv, dba, conv_w, avec, g64)
    return outs


def _gdn_direction(d, q_ref, k_ref, v_ref, kt_ref, bg_ref, bgt_ref, o_ref, st_ref):
    pp, cc, w = GDN_PAIR, DN_CHUNK, MIXER_W
    ii = lax.broadcasted_iota(jnp.int32, (pp, pp), 0)
    jj = lax.broadcasted_iota(jnp.int32, (pp, pp), 1)
    same = (ii // cc) == (jj // cc)
    if d == 0:
        incl, strict, incl_t = same & (jj <= ii), same & (jj < ii), same & (ii <= jj)
    else:
        incl, strict, incl_t = same & (jj >= ii), same & (jj > ii), same & (ii >= jj)
    incl_b = _onehot(incl)
    incl_tb = _onehot(incl_t)
    eye = jnp.where(ii == jj, 1.0, 0.0)
    merge_masks = [((ii // (2 * sz)) == (jj // (2 * sz))) & ((ii // sz) != (jj // sz))
                   for sz in (2 ** e for e in range(int(math.log2(cc))))]

    bg = bg_ref[0]
    src = lax.broadcasted_iota(jnp.int32, (128, w), 0)
    lane_w = lax.broadcasted_iota(jnp.int32, (128, w), 1)
    exp_beta = _onehot(src == DN_HEADS * d + lane_w // DN_DV)
    exp_g = _onehot(src == 2 * DN_HEADS + DN_HEADS * d + lane_w // DN_DV)
    src2 = lax.broadcasted_iota(jnp.int32, (128, DN_HEADS * pp), 0)
    lane2 = lax.broadcasted_iota(jnp.int32, (128, DN_HEADS * pp), 1)
    exp_g2 = _onehot(src2 == 2 * DN_HEADS + DN_HEADS * d + lane2 // pp)
    beta_x = _dot_sel_r(bg, exp_beta)
    cg_x = _dot_sel_l(incl_b, _dot_sel_r(bg, exp_g))
    cg_x2 = _dot_sel_l(incl_b, _dot_sel_r(bg, exp_g2))
    cg_rows = _dot_sel_r(bgt_ref[0], incl_tb)

    q = q_ref[0]
    k = k_ref[0]
    kf = k.astype(F32)
    vf = v_ref[0].astype(F32)
    e_cg = jnp.exp(cg_x)
    lane_head = lax.broadcasted_iota(jnp.int32, (pp, w), 1) // DN_DV
    rhs_v = vf * beta_x
    rhs_k = kf * (beta_x * e_cg)
    kb = kf * beta_x

    u_all = jnp.zeros((pp, w), F32)
    w_all = jnp.zeros((pp, w), F32)
    qkd = []
    for hd in range(DN_HEADS):
        mine = lane_head == hd
        kk = _dot_nt(jnp.where(mine, kb, 0.0).astype(BF16), k)
        qk = _dot_nt(_keep(mine, q), k)
        gl = 2 * DN_HEADS + DN_HEADS * d + hd
        diff = jnp.where(incl, cg_x2[:, hd * pp:(hd + 1) * pp] - cg_rows[gl:gl + 1, :], 0.0)
        decay = jnp.where(incl, jnp.exp(diff), 0.0)
        a = jnp.where(strict, kk * decay, 0.0)
        qkd.append((qk * decay).astype(BF16))
        t_inv = eye - jnp.where(merge_masks[0], a, 0.0)
        for mask in merge_masks[1:]:
            tb = t_inv.astype(BF16)
            t_inv = t_inv - _dot(tb, _dot(jnp.where(mask, a, 0.0).astype(BF16), tb).astype(BF16))
        tb = t_inv.astype(BF16)
        u_all = u_all + _dot(tb, jnp.where(mine, rhs_v, 0.0).astype(BF16))
        w_all = w_all + _dot(tb, jnp.where(mine, rhs_k, 0.0).astype(BF16))

    st = st_ref[d]
    blk = (lax.broadcasted_iota(jnp.int32, (w, w), 0) // DN_DK) == (lax.broadcasted_iota(jnp.int32, (w, w), 1) // DN_DV)
    kt = kt_ref[0]
    order = ((0, cc), (cc, pp)) if d == 0 else ((cc, pp), (0, cc))
    zeros_c = jnp.zeros((cc, w), F32)
    lane_head_c = lax.broadcasted_iota(jnp.int32, (cc, w), 1) // DN_DV
    nv_acc = None
    for lo, hi in order:
        rows = slice(lo, hi)
        stb = st.astype(BF16)
        nv = u_all[rows] - _dot(w_all[rows].astype(BF16), stb)
        last = hi - 1 if d == 0 else lo
        g_end = cg_x[last:last + 1, :]
        place = (lambda t: jnp.concatenate([t, zeros_c], axis=0)) if lo == 0 else (
            lambda t: jnp.concatenate([zeros_c, t], axis=0))
        nv_full = place(nv)
        nv_acc = nv_full if nv_acc is None else nv_acc + nv_full
        nvb = nv_acc.astype(BF16)
        o = e_cg[rows] * _dot(q[rows], stb)
        for hd in range(DN_HEADS):
            o = o + jnp.where(lane_head_c == hd, _dot(qkd[hd][rows], nvb), 0.0)
        o_ref[0, rows, :] = o
        nvs = place(nv * jnp.exp(g_end - cg_x[rows])).astype(BF16)
        st = st * jnp.exp(g_end) + jnp.where(blk, _dot(kt, nvs), 0.0)
    st_ref[d] = st


def _gdn_scan_kernel(qf, kf, vf, ktf, bgf, bgtf, qb, kb, vb, ktb, bgb, bgtb, of_ref, ob_ref, st_ref):
    @pl.when(pl.program_id(1) == 0)
    def _():
        st_ref[...] = jnp.zeros_like(st_ref)

    _gdn_direction(0, qf, kf, vf, ktf, bgf, bgtf, of_ref, st_ref)
    _gdn_direction(1, qb, kb, vb, ktb, bgb, bgtb, ob_ref, st_ref)


def _gdn_scan(q, k, v, kt, bg, bgt, lc):
    b, s, w = q.shape
    n_pairs = s // GDN_PAIR
    ncp = lc // GDN_PAIR

    def fwd(bb, i):
        return i

    def bwd(bb, i):
        return jnp.where(i < ncp, ncp - 1 - i, n_pairs - 1 + ncp - i)

    def specs(pos):
        tok = lambda bb, i: (bb, pos(bb, i), 0)
        tokt = lambda bb, i: (bb, 0, pos(bb, i))
        return [pl.BlockSpec((1, GDN_PAIR, w), tok)] * 3 + [
            pl.BlockSpec((1, w, GDN_PAIR), tokt), pl.BlockSpec((1, GDN_PAIR, 128), tok),
            pl.BlockSpec((1, 128, GDN_PAIR), tokt)]

    return pl.pallas_call(
        _gdn_scan_kernel, grid=(b, n_pairs),
        in_specs=specs(fwd) + specs(bwd),
        out_specs=[pl.BlockSpec((1, GDN_PAIR, w), lambda bb, i: (bb, fwd(bb, i), 0)),
                   pl.BlockSpec((1, GDN_PAIR, w), lambda bb, i: (bb, bwd(bb, i), 0))],
        out_shape=[jax.ShapeDtypeStruct((b, s, w), F32)] * 2,
        scratch_shapes=[pltpu.VMEM((2, w, w), F32)],
        compiler_params=_params(("arbitrary", "arbitrary")), name="gdn_scan",
    )(q, k, v, kt, bg, bgt, q, k, v, kt, bg, bgt)


def _outproj_kernel(x_ref, oa_ref, ob_ref, oc_ref, of_ref, obw_ref, dz_ref, mod_ref, w_ref, gout_ref,
                    g64_ref, gffn_ref, wr_ref, br_ref, xo_ref, f_ref, lg_ref):
    o = of_ref[0] + obw_ref[0]
    r = lax.rsqrt(_group_sumsq(o, g64_ref[...]) * (1.0 / DN_DV) + NORM_EPS)
    od = (o * r * gout_ref[...] * _silu(dz_ref[0])).astype(BF16)
    y = (_dot(oa_ref[0], w_ref[0]) + _dot(ob_ref[0], w_ref[1]) + _dot(oc_ref[0], w_ref[2]) + _dot(od, w_ref[3]))
    x = x_ref[0] + mod_ref[0, 0, 2:3, :] * y
    xo_ref[0] = x
    ms = jnp.mean(x * x, axis=-1, keepdims=True)
    f = (x * lax.rsqrt(ms + NORM_EPS) * gffn_ref[...]) * (1.0 + mod_ref[0, 0, 4:5, :]) + mod_ref[0, 0, 3:4, :]
    f_ref[0] = f.astype(BF16)
    lg_ref[...] = lax.dot_general(wr_ref[...], f, (((1,), (1,)), ((), ())), precision=HIGHEST,
                                  preferred_element_type=F32) + br_ref[:, 0:1]


def _outproj(layer, x, oa, ob, oc, o_f, o_b, dz, mod_t, w_out4, gout, g64, gffn, wr_t, br, n_ctx_tiles):
    b, s, d = x.shape
    nt = s // TQ
    ctx_row = b
    tok = lambda i, bb: (bb, i, 0)
    const2 = lambda i, bb: (0, 0)
    slab = pl.BlockSpec((1, TQ, MIXER_W), tok)
    return pl.pallas_call(
        _outproj_kernel, grid=(nt, b),
        in_specs=[pl.BlockSpec((1, TQ, d), tok), slab, slab, slab, slab, slab, slab,
                  pl.BlockSpec((1, 1, 6, d), lambda i, bb: (layer, jnp.where(i < n_ctx_tiles, ctx_row, bb), 0, 0)),
                  pl.BlockSpec((4, MIXER_W, d), lambda i, bb: (0, 0, 0)),
                  pl.BlockSpec((1, MIXER_W), const2), pl.BlockSpec((MIXER_W, MIXER_W), const2),
                  pl.BlockSpec((1, d), const2), pl.BlockSpec((N_EXPERTS, d), const2),
                  pl.BlockSpec((N_EXPERTS, 128), const2)],
        out_specs=[pl.BlockSpec((1, TQ, d), tok), pl.BlockSpec((1, TQ, d), tok),
                   pl.BlockSpec((N_EXPERTS, TQ), lambda i, bb: (0, bb * nt + i))],
        out_shape=[jax.ShapeDtypeStruct((b, s, d), F32), jax.ShapeDtypeStruct((b, s, d), BF16),
                   jax.ShapeDtypeStruct((N_EXPERTS, b * s), F32)],
        compiler_params=_params(("arbitrary", "arbitrary")), name="out_proj",
    )(x, oa, ob, oc, o_f, o_b, dz, mod_t, w_out4, gout, g64, gffn, wr_t, br)


def _route_kernel(lg_ref, tri_ref, pos_ref, gate_ref, cnt_ref):
    x = lg_ref[...]
    e_iota = lax.broadcasted_iota(jnp.int32, x.shape, 0).astype(F32)
    work = x
    chosen = jnp.zeros(x.shape, F32)
    top = None
    den = None
    for kk in range(TOP_K):
        m = jnp.max(work, axis=0, keepdims=True)
        idx = jnp.min(jnp.where(work == m, e_iota, float(N_EXPERTS)), axis=0, keepdims=True)
        pick = e_iota == idx
        chosen = jnp.where(pick, 1.0, chosen)
        if kk == 0:
            top = m
            den = jnp.ones_like(m)
        else:
            den = den + jnp.exp(m - top)
        work = jnp.where(pick, -jnp.inf, work)
    sel = chosen > 0.5
    gate_ref[0] = jnp.where(sel, jnp.exp(x - top) / den, 0.0)
    rank = _dot(chosen.astype(BF16), tri_ref[...])
    pos_ref[0] = jnp.where(sel, rank.astype(jnp.int32), -1)
    cnt = jnp.sum(chosen, axis=1, keepdims=True).astype(jnp.int32)
    cnt_ref[0] = jnp.broadcast_to(cnt, cnt_ref.shape[1:])


def _route(logits_t, tri):
    n_exp, t = logits_t.shape
    n_tiles = t // MOE_TM
    return pl.pallas_call(
        _route_kernel, grid=(n_tiles,),
        in_specs=[pl.BlockSpec((n_exp, MOE_TM), lambda i: (0, i)), pl.BlockSpec((MOE_TM, MOE_TM), lambda i: (0, 0))],
        out_specs=[pl.BlockSpec((1, n_exp, MOE_TM), lambda i: (i, 0, 0)),
                   pl.BlockSpec((1, n_exp, MOE_TM), lambda i: (i, 0, 0)),
                   pl.BlockSpec((1, n_exp, 128), lambda i: (i, 0, 0))],
        out_shape=[jax.ShapeDtypeStruct((n_tiles, n_exp, MOE_TM), jnp.int32),
                   jax.ShapeDtypeStruct((n_tiles, n_exp, MOE_TM), F32),
                   jax.ShapeDtypeStruct((n_tiles, n_exp, 128), jnp.int32)],
        compiler_params=_params(("arbitrary",)), name="route",
    )(logits_t, tri)


def _moe_kernel(cnt_ref, x_ref, pos_ref, gate_ref, wu_ref, bu_ref, wd_ref, bd_ref, o_ref):
    t = pl.program_id(0)
    e = pl.program_id(1)
    tm = x_ref.shape[0]
    de = wd_ref.shape[1]

    @pl.when(e == 0)
    def _():
        o_ref[...] = jnp.zeros_like(o_ref)

    n_blocks = (cnt_ref[t * N_EXPERTS + e] + MOE_RB - 1) // MOE_RB
    pos_row = pos_ref[0, pl.ds(e, 1), :]
    gate_row = gate_ref[0, pl.ds(e, 1), :]

    def block(j, carry):
        slot = j * MOE_RB + lax.broadcasted_iota(jnp.int32, (MOE_RB, tm), 0)
        hit = pos_row == slot
        sel = _onehot(hit)
        xg = _dot(sel, x_ref[...]).astype(BF16)
        hgu = _dot(xg, wu_ref[0]) + bu_ref[0]
        gate = jnp.minimum(hgu[:, :de], SWIGLU_LIMIT)
        up = jnp.clip(hgu[:, de:], -SWIGLU_LIMIT, SWIGLU_LIMIT)
        hid = gate * jax.nn.sigmoid(SWIGLU_ALPHA * gate) * (up + 1.0)
        y = _dot(hid.astype(BF16), wd_ref[0]) + bd_ref[0]
        g_row = jnp.sum(jnp.where(hit, gate_row, 0.0), axis=1, keepdims=True)
        o_ref[...] += _dot_tn(sel, (y * g_row).astype(BF16))
        return carry

    lax.fori_loop(0, n_blocks, block, 0)


def _moe(counts, f_flat, pos_t, gate_t, w_up, b_up, w_down, b_down):
    t, d = f_flat.shape
    n_tiles = t // MOE_TM
    n_exp, _, de2 = w_up.shape
    de = de2 // 2
    grid_spec = pltpu.PrefetchScalarGridSpec(
        num_scalar_prefetch=1, grid=(n_tiles, n_exp),
        in_specs=[
            pl.BlockSpec((MOE_TM, d), lambda i, e, c: (i, 0)),
            pl.BlockSpec((1, n_exp, MOE_TM), lambda i, e, c: (i, 0, 0)),
            pl.BlockSpec((1, n_exp, MOE_TM), lambda i, e, c: (i, 0, 0)),
            pl.BlockSpec((1, d, de2), lambda i, e, c: (e, 0, 0)),
            pl.BlockSpec((1, 1, de2), lambda i, e, c: (e, 0, 0)),
            pl.BlockSpec((1, de, d), lambda i, e, c: (e, 0, 0)),
            pl.BlockSpec((1, 1, d), lambda i, e, c: (e, 0, 0)),
        ],
        out_specs=pl.BlockSpec((MOE_TM, d), lambda i, e, c: (i, 0)),
    )
    return pl.pallas_call(
        _moe_kernel, grid_spec=grid_spec, out_shape=jax.ShapeDtypeStruct((t, d), F32),
        compiler_params=_params(("arbitrary", "arbitrary")), name="moe_experts",
    )(counts, f_flat, pos_t, gate_t, w_up, b_up.reshape(n_exp, 1, de2), w_down, b_down.reshape(n_exp, 1, d))


def _final_kernel(x_ref, ffn_ref, mod_ref, g_ref, o_ref):
    x = x_ref[0] + mod_ref[0, 0, 5:6, :] * ffn_ref[0]
    ms = jnp.mean(x * x, axis=-1, keepdims=True)
    o_ref[0] = x * lax.rsqrt(ms + NORM_EPS) * g_ref[...]


def _final(layer, x, ffn, mod_t, gain, n_ctx_tiles, l):
    b, s, d = x.shape
    tok = lambda bb, i: (bb, i + n_ctx_tiles, 0)
    return pl.pallas_call(
        _final_kernel, grid=(b, l // TQ),
        in_specs=[pl.BlockSpec((1, TQ, d), tok), pl.BlockSpec((1, TQ, d), tok),
                  pl.BlockSpec((1, 1, 6, d), lambda bb, i: (layer, bb, 0, 0)),
                  pl.BlockSpec((1, d), lambda bb, i: (0, 0))],
        out_specs=pl.BlockSpec((1, TQ, d), lambda bb, i: (bb, i, 0)),
        out_shape=jax.ShapeDtypeStruct((b, l, d), F32),
        compiler_params=_params(("arbitrary", "arbitrary")), name="final_norm",
    )(x, ffn, mod_t, gain)


def _swap_halves(n_groups, width):
    base = np.arange(n_groups * width).reshape(n_groups, width)
    return np.concatenate([base[:, width // 2:], base[:, :width // 2]], axis=1).reshape(-1)


def _in_columns():
    sizes = (GQA_HEADS * HEAD_DIM, GQA_KV_HEADS * HEAD_DIM, GQA_KV_HEADS * HEAD_DIM, MIXER_W, MIXER_W,
             MIXER_W, MIXER_W, MIXER_W, 3 * MIXER_W, MIXER_W, 2 * DN_HEADS, 2 * DN_HEADS)
    starts = np.concatenate([[0], np.cumsum(sizes)[:-1]])
    aq, ak, av, bu, bv, cq, ck, cv, dqkv, dz, db, da = (np.arange(n) + o for n, o in zip(sizes, starts))
    grp = GQA_HEADS // GQA_KV_HEADS
    expand = np.concatenate([np.arange(HEAD_DIM) + (hd // grp) * HEAD_DIM for hd in range(GQA_HEADS)])
    sw64 = _swap_halves(GQA_HEADS, HEAD_DIM)
    sw32 = _swap_halves(2 * DIFF_HEADS, DIFF_DIM)
    ak_x, av_x = ak[expand], av[expand]
    cols = [aq, aq[sw64], ak_x, ak_x[sw64], av_x, bu, bv, cq, cq[sw32], ck, ck[sw32], cv, dqkv, dz,
            db, da, np.full(128 - 4 * DN_HEADS, -1)]
    return np.concatenate(cols), expand, sw64


def _take_cols(w, cols):
    safe = np.where(cols < 0, 0, cols)
    return jnp.where(jnp.asarray(cols >= 0)[None, :], w[:, safe], 0.0)


def _rope_tables(l, lc):
    rows = l // GRID_W
    r_idx, c_idx = np.meshgrid(np.arange(rows), np.arange(GRID_W), indexing="ij")
    row_pos = jnp.asarray(r_idx.reshape(-1), F32)
    col_pos = jnp.asarray(c_idx.reshape(-1), F32)

    def table(dim, reps):
        n = dim // 4
        inv = jnp.power(ROPE_THETA, -jnp.arange(n, dtype=F32) / n)
        ang = jnp.concatenate([row_pos[:, None] * inv, col_pos[:, None] * inv], axis=-1)
        cos, sin = jnp.cos(ang), jnp.sin(ang)
        cos_t = jnp.tile(jnp.concatenate([cos, cos], axis=-1), (1, reps))
        sin_t = jnp.tile(jnp.concatenate([-sin, sin], axis=-1), (1, reps))
        return (jnp.concatenate([jnp.ones((lc, cos_t.shape[1]), F32), cos_t], axis=0),
                jnp.concatenate([jnp.zeros((lc, sin_t.shape[1]), F32), sin_t], axis=0))

    cos_a, sin_a = table(HEAD_DIM, GQA_HEADS)
    cos_d, sin_d = table(DIFF_DIM, 2 * DIFF_HEADS)
    return jnp.stack([cos_a, sin_a, cos_d, sin_d])


def kernel(x, c, ctx, c_ctx, w_ada, b_ada, norm_mix, norm_ffn, w_in, w_out, gqa_q_norm, gqa_k_norm, gmlp_v_norm, gmlp_w_s, gmlp_b_s, diff_lambda_q1, diff_lambda_k1, diff_lambda_q2, diff_lambda_k2, diff_subln, dn_conv_w, dn_a_log, dn_dt_bias, dn_out_norm, router_w, router_b, exp_w_up, exp_b_up, exp_w_down, exp_b_down, final_norm):
    b, l, d = x.shape
    lc = ctx.shape[1]
    depth = w_ada.shape[0]
    s = lc + l
    assert lc % TQ == 0 and l % TQ == 0 and (b * s) % MOE_TM == 0 and l % GRID_W == 0
    n_ctx_tiles = lc // TQ

    rows = -(-(b + 1) // 8) * 8
    c_all = jnp.zeros((rows, d), F32).at[:b].set(c).at[b].set(c_ctx)
    mod_t = _modulation(c_all, w_ada, b_ada).transpose(0, 2, 1, 3)

    cols, expand, sw64 = _in_columns()
    rope = _rope_tables(l, lc)
    lane = np.arange(MIXER_W)
    g64 = jnp.asarray((lane[:, None] // 64) == (lane[None, :] // 64), BF16)
    tri = jnp.asarray(np.arange(MOE_TM)[:, None] < np.arange(MOE_TM)[None, :], BF16)

    xs = jnp.concatenate([ctx, x], axis=1)
    ffn = None
    for layer in range(depth):
        lam_init = 0.8 - 0.6 * math.exp(-0.3 * layer)
        w_ext = _take_cols(w_in[layer], cols).astype(BF16)
        gq = jnp.tile(gqa_q_norm[layer], GQA_HEADS)
        gk = jnp.tile(gqa_k_norm[layer], GQA_HEADS)
        vecs = jnp.zeros((8, MIXER_W), F32).at[0].set(gq).at[1].set(gq[sw64]).at[2].set(gk).at[3].set(gk[sw64])
        vecs = vecs.at[4].set(gmlp_v_norm[layer])
        bst = jnp.repeat(gmlp_b_s[layer].T, GMLP_CH, axis=1)
        xs, (qa, ka, va, out_b, qd, kd, vd, dqkv, dz, dba) = _inproj(
            layer, xs, ffn, mod_t, norm_mix[layer][None, :], w_ext, vecs, g64, rope,
            gmlp_w_s[layer].astype(BF16), bst, n_ctx_tiles)

        out_a = _attention("gqa", qa, ka, va, n_ctx_tiles, lc)
        lam_p = jnp.stack([diff_lambda_q1[layer], diff_lambda_k1[layer], diff_lambda_q2[layer], diff_lambda_k2[layer]])
        sub = jnp.tile(diff_subln[layer], DIFF_HEADS)[None, :]
        out_c = _attention("diff", qd, kd, vd, n_ctx_tiles, lc, extra=(lam_p, sub, g64), lam_init=lam_init)

        avec = jnp.zeros((2, 128), F32)
        avec = avec.at[0, 2 * DN_HEADS:4 * DN_HEADS].set(dn_a_log[layer].reshape(-1))
        avec = avec.at[1, 2 * DN_HEADS:4 * DN_HEADS].set(dn_dt_bias[layer].reshape(-1))
        gq_, gk_, gv_, gkt, gbg, gbgt = _gdn_prep(dqkv, dba, dn_conv_w[layer], avec, g64, n_ctx_tiles)
        o_f, o_b = _gdn_scan(gq_, gk_, gv_, gkt, gbg, gbgt, lc)

        xs, f, logits_t = _outproj(
            layer, xs, out_a, out_b, out_c, o_f, o_b, dz, mod_t,
            w_out[layer].reshape(4, MIXER_W, d).astype(BF16), jnp.tile(dn_out_norm[layer], DN_HEADS)[None, :], g64,
            norm_ffn[layer][None, :], router_w[layer].T, jnp.broadcast_to(router_b[layer][:, None], (N_EXPERTS, 128)),
            n_ctx_tiles)

        pos_t, gate_t, cnt = _route(logits_t, tri)
        ffn = _moe(cnt[:, :, 0].reshape(-1), f.reshape(b * s, d), pos_t, gate_t,
                   exp_w_up[layer].astype(BF16), exp_b_up[layer], exp_w_down[layer].astype(BF16), exp_b_down[layer])
        ffn = ffn.reshape(b, s, d)

    return _final(depth - 1, xs, ffn, mod_t, final_norm[None, :], n_ctx_tiles, l)
```

```python
import functools
import math

import numpy as np
import jax
import jax.numpy as jnp
from jax import lax
from jax.experimental import pallas as pl
from jax.experimental.pallas import tpu as pltpu

F32 = jnp.float32
BF16 = jnp.bfloat16
HIGHEST = lax.Precision.HIGHEST

GRID_W = 64
NORM_EPS = 1e-6
ROPE_THETA = 10000.0
HEAD_DIM = 64
GQA_HEADS = 4
GQA_KV_HEADS = 2
GMLP_GROUPS = 4
GMLP_CH = 64
GMLP_CHUNK = 128
DIFF_HEADS = 4
DIFF_DIM = 32
DN_HEADS = 4
DN_DK = 64
DN_DV = 64
DN_CHUNK = 64
N_EXPERTS = 32
TOP_K = 4
SWIGLU_LIMIT = 7.0
SWIGLU_ALPHA = 1.702

MIXER_W = 256
TQ = 256
GDN_PAIR = 2 * DN_CHUNK
MOE_TM = 1024
MOE_RB = 160
VMEM_LIMIT = 56 * 1024 * 1024

_SEGS = ("aq", "aqs", "ak", "aks", "av", "bu", "bv", "cq", "cqs", "ck", "cks", "cv")
OFF = {name: i * MIXER_W for i, name in enumerate(_SEGS)}
OFF["dqkv"] = len(_SEGS) * MIXER_W
OFF["dz"] = OFF["dqkv"] + 3 * MIXER_W
OFF["dba"] = OFF["dz"] + MIXER_W
W_EXT = OFF["dba"] + 128


def _dot(a, b):
    return jnp.dot(a, b, preferred_element_type=F32)


def _dot_nt(a, b):
    return lax.dot_general(a, b, (((1,), (1,)), ((), ())), preferred_element_type=F32)


def _dot_tn(a, b):
    return lax.dot_general(a, b, (((0,), (0,)), ((), ())), preferred_element_type=F32)


def _split3(x):
    hi = x.astype(BF16)
    r1 = x - hi.astype(F32)
    mid = r1.astype(BF16)
    lo = (r1 - mid.astype(F32)).astype(BF16)
    return hi, mid, lo


def _dot_sel_r(x, sel):
    hi, mid, lo = _split3(x)
    return _dot(hi, sel) + _dot(mid, sel) + _dot(lo, sel)


def _dot_sel_l(sel, x):
    hi, mid, lo = _split3(x)
    return _dot(sel, hi) + _dot(sel, mid) + _dot(sel, lo)


def _group_sumsq(x, g_same):
    x2 = x * x
    hi = x2.astype(BF16)
    lo = (x2 - hi.astype(F32)).astype(BF16)
    return _dot(hi, g_same) + _dot(lo, g_same)


def _onehot(cond):
    return jnp.where(cond, 1.0, 0.0).astype(BF16)


def _keep(cond, x):
    return jnp.where(cond, x.astype(F32), 0.0).astype(BF16)


def _silu(x):
    return x * jax.nn.sigmoid(x)


def _softplus(x):
    return jnp.maximum(x, 0.0) + jnp.log1p(jnp.exp(-jnp.abs(x)))


def _params(sem):
    return pltpu.CompilerParams(dimension_semantics=sem, vmem_limit_bytes=VMEM_LIMIT)


def _mod_kernel(c_ref, w_ref, b_ref, o_ref):
    s = _silu(c_ref[...])
    o_ref[0, 0] = jnp.dot(s, w_ref[0], precision=HIGHEST, preferred_element_type=F32) + b_ref[0]


def _modulation(c_all, w_ada, b_ada):
    depth, d, _ = w_ada.shape
    r = c_all.shape[0]
    return pl.pallas_call(
        _mod_kernel,
        grid=(depth, 6),
        in_specs=[
            pl.BlockSpec((r, d), lambda l, j: (0, 0)),
            pl.BlockSpec((1, d, d), lambda l, j: (l, 0, j)),
            pl.BlockSpec((1, 1, d), lambda l, j: (l, 0, j)),
        ],
        out_specs=pl.BlockSpec((1, 1, r, d), lambda l, j: (l, j, 0, 0)),
        out_shape=jax.ShapeDtypeStruct((depth, 6, r, d), F32),
        compiler_params=_params(("arbitrary", "arbitrary")),
        name="adaln_mod",
    )(c_all, w_ada, b_ada.reshape(depth, 1, 6 * d))


def _inproj_kernel(*refs, has_prev, scale_a, scale_d):
    if has_prev:
        x_ref, ffn_ref, modp_ref, refs = refs[0], refs[1], refs[2], refs[3:]
    else:
        x_ref, refs = refs[0], refs[1:]
    (mod_ref, gmix_ref, w_ref, vec_ref, g64_ref, rope_ref, ws_ref, bst_ref) = refs[:8]
    outs = refs[8:]
    if has_prev:
        xo_ref, outs = outs[0], outs[1:]
    (qa_ref, ka_ref, va_ref, ob_ref, qd_ref, kd_ref, vd_ref, dqkv_ref, dz_ref, dba_ref) = outs

    x = x_ref[0]
    if has_prev:
        x = x + modp_ref[0, 0, 5:6, :] * ffn_ref[0]
        xo_ref[0] = x
    ms = jnp.mean(x * x, axis=-1, keepdims=True)
    xn = x * lax.rsqrt(ms + NORM_EPS) * gmix_ref[...]
    h = xn * (1.0 + mod_ref[0, 0, 1:2, :]) + mod_ref[0, 0, 0:1, :]
    p = _dot(h.astype(BF16), w_ref[...])

    def seg(name, width=MIXER_W):
        return p[:, OFF[name]:OFF[name] + width]

    g64 = g64_ref[...]
    cos_a, sin_a, cos_d, sin_d = rope_ref[0], rope_ref[1], rope_ref[2], rope_ref[3]

    def norm_rope(x0, xs, gain, gain_s, scale):
        r = lax.rsqrt(_group_sumsq(x0, g64) * (1.0 / HEAD_DIM) + NORM_EPS)
        return ((x0 * r * gain) * cos_a + (xs * r * gain_s) * sin_a) * scale

    qa_ref[0] = norm_rope(seg("aq"), seg("aqs"), vec_ref[0:1, :], vec_ref[1:2, :], scale_a).astype(BF16)
    ka_ref[0] = norm_rope(seg("ak"), seg("aks"), vec_ref[2:3, :], vec_ref[3:4, :], 1.0).astype(BF16)
    va_ref[0] = seg("av").astype(BF16)

    u = jax.nn.gelu(seg("bu"))
    v = jax.nn.gelu(seg("bv"))
    vn = v * lax.rsqrt(jnp.mean(v * v, axis=-1, keepdims=True) + NORM_EPS) * vec_ref[4:5, :]
    lane_grp = lax.broadcasted_iota(jnp.int32, (GMLP_CHUNK, MIXER_W), 1) // GMLP_CH
    for ci in range(TQ // GMLP_CHUNK):
        rows = slice(ci * GMLP_CHUNK, (ci + 1) * GMLP_CHUNK)
        vc = vn[rows]
        sp = bst_ref[...]
        for g in range(GMLP_GROUPS):
            sp = sp + _dot(ws_ref[g], jnp.where(lane_grp == g, vc, 0.0).astype(BF16))
        ob_ref[0, rows, :] = (u[rows] * sp).astype(BF16)

    qd_ref[0] = ((seg("cq") * cos_d + seg("cqs") * sin_d) * scale_d).astype(BF16)
    kd_ref[0] = (seg("ck") * cos_d + seg("cks") * sin_d).astype(BF16)
    vd_ref[0] = seg("cv").astype(BF16)

    dqkv_ref[0] = seg("dqkv", 3 * MIXER_W)
    dz_ref[0] = seg("dz")
    dba_ref[0] = seg("dba", 128)


def _inproj(layer, x, prev, mod_t, gmix, w_ext, vecs, g64, rope, ws, bst, n_ctx_tiles):
    b, s, d = x.shape
    nt = s // TQ
    ctx_row = b

    def mod_map(l):
        return lambda i, bb: (l, jnp.where(i < n_ctx_tiles, ctx_row, bb), 0, 0)

    tok = lambda i, bb: (bb, i, 0)
    const2 = lambda i, bb: (0, 0)
    const3 = lambda i, bb: (0, 0, 0)
    in_specs = [pl.BlockSpec((1, TQ, d), tok)]
    args = [x]
    if prev is not None:
        in_specs += [pl.BlockSpec((1, TQ, d), tok), pl.BlockSpec((1, 1, 6, d), mod_map(layer - 1))]
        args += [prev, mod_t]
    in_specs += [
        pl.BlockSpec((1, 1, 6, d), mod_map(layer)),
        pl.BlockSpec((1, d), const2),
        pl.BlockSpec((d, W_EXT), const2),
        pl.BlockSpec((8, MIXER_W), const2),
        pl.BlockSpec((MIXER_W, MIXER_W), const2),
        pl.BlockSpec((4, TQ, MIXER_W), lambda i, bb: (0, i, 0)),
        pl.BlockSpec((GMLP_GROUPS, GMLP_CHUNK, GMLP_CHUNK), const3),
        pl.BlockSpec((GMLP_CHUNK, MIXER_W), const2),
    ]
    args += [mod_t, gmix, w_ext, vecs, g64, rope, ws, bst]
    bf = lambda w: jax.ShapeDtypeStruct((b, s, w), BF16)
    ff = lambda w: jax.ShapeDtypeStruct((b, s, w), F32)
    out_shape = [bf(MIXER_W)] * 7 + [ff(3 * MIXER_W), ff(MIXER_W), ff(128)]
    out_specs = [pl.BlockSpec((1, TQ, MIXER_W), tok)] * 7 + [
        pl.BlockSpec((1, TQ, 3 * MIXER_W), tok), pl.BlockSpec((1, TQ, MIXER_W), tok), pl.BlockSpec((1, TQ, 128), tok)]
    if prev is not None:
        out_shape = [ff(d)] + out_shape
        out_specs = [pl.BlockSpec((1, TQ, d), tok)] + out_specs
    outs = pl.pallas_call(
        functools.partial(_inproj_kernel, has_prev=prev is not None,
                          scale_a=HEAD_DIM ** -0.5, scale_d=DIFF_DIM ** -0.5),
        grid=(nt, b), in_specs=in_specs, out_specs=out_specs, out_shape=out_shape,
        compiler_params=_params(("arbitrary", "arbitrary")), name="in_proj",
    )(*args)
    if prev is not None:
        return outs[0], outs[1:]
    return x, outs


def _softmax_pv(qm, k, v):
    s = _dot_nt(qm, k)
    m = jnp.max(s, axis=-1, keepdims=True)
    e = jnp.exp(s - m)
    den = jnp.sum(e, axis=-1, keepdims=True)
    return _dot(e.astype(BF16), v) / den


def _gqa_kernel(q_ref, k_ref, v_ref, o_ref, *, n_ctx_tiles, lc):
    i = pl.program_id(1)
    lane_head = lax.broadcasted_iota(jnp.int32, (TQ, MIXER_W), 1) // HEAD_DIM

    def attend(k, v):
        q = q_ref[0]
        acc = jnp.zeros((TQ, MIXER_W), F32)
        for hd in range(GQA_HEADS):
            mine = lane_head == hd
            o = _softmax_pv(_keep(mine, q), k, v)
            acc = jnp.where(mine, o, acc)
        o_ref[0] = acc.astype(BF16)

    @pl.when(i < n_ctx_tiles)
    def _():
        attend(k_ref[0, :lc, :], v_ref[0, :lc, :])

    @pl.when(i >= n_ctx_tiles)
    def _():
        attend(k_ref[0], v_ref[0])


def _diff_kernel(q_ref, k_ref, v_ref, lam_ref, sub_ref, g64_ref, o_ref, *, n_ctx_tiles, lc, lam_init):
    i = pl.program_id(1)
    lane = lax.broadcasted_iota(jnp.int32, (TQ, MIXER_W), 1)
    lane_head = lane // (2 * DIFF_DIM)
    lane_map = lane // DIFF_DIM
    lp = lam_ref[...]
    lam = (jnp.exp(jnp.sum(lp[0:1] * lp[1:2], axis=-1, keepdims=True))
           - jnp.exp(jnp.sum(lp[2:3] * lp[3:4], axis=-1, keepdims=True)) + lam_init)

    def attend(k, v):
        q = q_ref[0]
        acc = jnp.zeros((TQ, MIXER_W), F32)
        for hd in range(DIFF_HEADS):
            o1 = _softmax_pv(_keep(lane_map == 2 * hd, q), k, v)
            o2 = _softmax_pv(_keep(lane_map == 2 * hd + 1, q), k, v)
            acc = jnp.where(lane_head == hd, o1 - lam * o2, acc)
        r = lax.rsqrt(_group_sumsq(acc, g64_ref[...]) * (1.0 / (2 * DIFF_DIM)) + NORM_EPS)
        o_ref[0] = (acc * r * sub_ref[...] * (1.0 - lam_init)).astype(BF16)

    @pl.when(i < n_ctx_tiles)
    def _():
        attend(k_ref[0, :lc, :], v_ref[0, :lc, :])

    @pl.when(i >= n_ctx_tiles)
    def _():
        attend(k_ref[0], v_ref[0])


def _attention(kind, q, k, v, n_ctx_tiles, lc, extra=(), lam_init=0.0):
    b, s, w = q.shape
    nt = s // TQ
    tok = lambda bb, i: (bb, i, 0)
    row = lambda bb, i: (bb, 0, 0)
    in_specs = [pl.BlockSpec((1, TQ, w), tok), pl.BlockSpec((1, s, w), row), pl.BlockSpec((1, s, w), row)]
    if kind == "gqa":
        body = functools.partial(_gqa_kernel, n_ctx_tiles=n_ctx_tiles, lc=lc)
    else:
        body = functools.partial(_diff_kernel, n_ctx_tiles=n_ctx_tiles, lc=lc, lam_init=lam_init)
        in_specs += [pl.BlockSpec(e.shape, lambda bb, i: (0, 0)) for e in extra]
    return pl.pallas_call(
        body, grid=(b, nt), in_specs=in_specs, out_specs=pl.BlockSpec((1, TQ, w), tok),
        out_shape=jax.ShapeDtypeStruct((b, s, w), BF16),
        compiler_params=_params(("arbitrary", "arbitrary")), name=kind + "_attention",
    )(q, k, v, *extra)


def _gdn_prep_kernel(x_ref, xp_ref, xn_ref, ba_ref, cw_ref, av_ref, g64_ref,
                     q_ref, k_ref, v_ref, kt_ref, bg_ref, bgt_ref, *, n_ctx_tiles, n_tiles):
    i = pl.program_id(0)
    x = x_ref[0]
    has_prev = jnp.where((i != 0) & (i != n_ctx_tiles), 1.0, 0.0)
    has_next = jnp.where((i != n_ctx_tiles - 1) & (i != n_tiles - 1), 1.0, 0.0)
    row = lax.broadcasted_iota(jnp.int32, x.shape, 0)
    x_m1 = jnp.where(row == 0, xp_ref[0, 7:8, :] * has_prev, pltpu.roll(x, 1, 0))
    x_p1 = jnp.where(row == TQ - 1, xn_ref[0, 0:1, :] * has_next, pltpu.roll(x, TQ - 1, 0))
    y = _silu(cw_ref[0:1, :] * x_m1 + cw_ref[1:2, :] * x + cw_ref[2:3, :] * x_p1)
    q, k, v = y[:, :MIXER_W], y[:, MIXER_W:2 * MIXER_W], y[:, 2 * MIXER_W:]
    g64 = g64_ref[...]
    qn = q * lax.rsqrt(_group_sumsq(q, g64) + NORM_EPS) * (DN_DK ** -0.5)
    kn = k * lax.rsqrt(_group_sumsq(k, g64) + NORM_EPS)
    q_ref[0] = qn.astype(BF16)
    k_ref[0] = kn.astype(BF16)
    v_ref[0] = v.astype(BF16)
    kt_ref[0] = kn.T.astype(BF16)
    ba = ba_ref[0]
    lane = lax.broadcasted_iota(jnp.int32, ba.shape, 1)
    beta = jax.nn.sigmoid(ba)
    g = -jnp.exp(av_ref[0:1, :]) * _softplus(ba + av_ref[1:2, :])
    bg = jnp.where(lane < 2 * DN_HEADS, beta, jnp.where(lane < 4 * DN_HEADS, g, 0.0))
    bg_ref[0] = bg
    bgt_ref[0] = bg.T


def _gdn_prep(dqkv, dba, conv_w, avec, g64, n_ctx_tiles):
    b, s, w3 = dqkv.shape
    nt = s // TQ
    nb8 = s // 8
    tok = lambda i, bb: (bb, i, 0)
    tokt = lambda i, bb: (bb, 0, i)
    const2 = lambda i, bb: (0, 0)
    outs = pl.pallas_call(
        functools.partial(_gdn_prep_kernel, n_ctx_tiles=n_ctx_tiles, n_tiles=nt),
        grid=(nt, b),
        in_specs=[
            pl.BlockSpec((1, TQ, w3), tok),
            pl.BlockSpec((1, 8, w3), lambda i, bb: (bb, jnp.maximum(i * (TQ // 8) - 1, 0), 0)),
            pl.BlockSpec((1, 8, w3), lambda i, bb: (bb, jnp.minimum((i + 1) * (TQ // 8), nb8 - 1), 0)),
            pl.BlockSpec((1, TQ, 128), tok),
            pl.BlockSpec((3, w3), const2),
            pl.BlockSpec((2, 128), const2),
            pl.BlockSpec((MIXER_W, MIXER_W), const2),
        ],
        out_specs=[pl.BlockSpec((1, TQ, MIXER_W), tok)] * 3 + [
            pl.BlockSpec((1, MIXER_W, TQ), tokt), pl.BlockSpec((1, TQ, 128), tok), pl.BlockSpec((1, 128, TQ), tokt)],
        out_shape=[jax.ShapeDtypeStruct((b, s, MIXER_W), BF16)] * 3 + [
            jax.ShapeDtypeStruct((b, MIXER_W, s), BF16), jax.ShapeDtypeStruct((b, s, 128), F32),
            jax.ShapeDtypeStruct((b, 128, s), F32)],
        compiler_params=_params(("arbitrary", "arbitrary")), name="gdn_prep",
    )(dqkv, dqkv, dqkv, dba, conv_w, avec, g64)
    return outs


def _gdn_scan_kernel(qf, kf, vf, ktf, bgf, bgtf, qb, kb, vb, ktb, bgb, bgtb, of_ref, ob_ref, st_ref):
    @pl.when(pl.program_id(1) == 0)
    def _():
        st_ref[...] = jnp.zeros_like(st_ref)

    pp, cc, w = GDN_PAIR, DN_CHUNK, MIXER_W
    dirs = (0, 1)
    heads = range(DN_HEADS)
    chains = [(d, hd) for d in dirs for hd in heads]
    q_refs, k_refs, v_refs, kt_refs = (qf, qb), (kf, kb), (vf, vb), (ktf, ktb)
    bg_refs, bgt_refs, o_refs = (bgf, bgb), (bgtf, bgtb), (of_ref, ob_ref)

    ii = lax.broadcasted_iota(jnp.int32, (pp, pp), 0)
    jj = lax.broadcasted_iota(jnp.int32, (pp, pp), 1)
    same = (ii // cc) == (jj // cc)
    incl = (same & (jj <= ii), same & (jj >= ii))
    strict = (same & (jj < ii), same & (jj > ii))
    incl_b = [_onehot(m) for m in incl]
    incl_tb = [_onehot(same & (ii <= jj)), _onehot(same & (ii >= jj))]
    eye = jnp.where(ii == jj, 1.0, 0.0)
    merge_masks = [((ii // (2 * sz)) == (jj // (2 * sz))) & ((ii // sz) != (jj // sz))
                   for sz in (2 ** e for e in range(int(math.log2(cc))))]

    src = lax.broadcasted_iota(jnp.int32, (128, w), 0)
    lane_w = lax.broadcasted_iota(jnp.int32, (128, w), 1)
    src2 = lax.broadcasted_iota(jnp.int32, (128, DN_HEADS * pp), 0)
    lane2 = lax.broadcasted_iota(jnp.int32, (128, DN_HEADS * pp), 1)
    bg = [r[0] for r in bg_refs]
    beta_x = [_dot_sel_r(bg[d], _onehot(src == DN_HEADS * d + lane_w // DN_DV)) for d in dirs]
    g_x = [_dot_sel_r(bg[d], _onehot(src == 2 * DN_HEADS + DN_HEADS * d + lane_w // DN_DV)) for d in dirs]
    g_x2 = [_dot_sel_r(bg[d], _onehot(src2 == 2 * DN_HEADS + DN_HEADS * d + lane2 // pp)) for d in dirs]
    cg_rows = [_dot_sel_r(bgt_refs[d][0], incl_tb[d]) for d in dirs]
    cg_x = [_dot_sel_l(incl_b[d], g_x[d]) for d in dirs]
    cg_x2 = [_dot_sel_l(incl_b[d], g_x2[d]) for d in dirs]

    q = [r[0] for r in q_refs]
    k = [r[0] for r in k_refs]
    kf32 = [t.astype(F32) for t in k]
    e_cg = [jnp.exp(t) for t in cg_x]
    rhs_v = [v_refs[d][0].astype(F32) * beta_x[d] for d in dirs]
    rhs_k = [kf32[d] * (beta_x[d] * e_cg[d]) for d in dirs]
    k_beta = [kf32[d] * beta_x[d] for d in dirs]
    lane_head = lax.broadcasted_iota(jnp.int32, (pp, w), 1) // DN_DV

    kk = [_dot_nt(jnp.where(lane_head == hd, k_beta[d], 0.0).astype(BF16), k[d]) for d, hd in chains]
    qk = [_dot_nt(_keep(lane_head == hd, q[d]), k[d]) for d, hd in chains]
    a, qkd = [], []
    for ci, (d, hd) in enumerate(chains):
        gl = 2 * DN_HEADS + DN_HEADS * d + hd
        diff = jnp.where(incl[d], cg_x2[d][:, hd * pp:(hd + 1) * pp] - cg_rows[d][gl:gl + 1, :], 0.0)
        decay = jnp.where(incl[d], jnp.exp(diff), 0.0)
        a.append(jnp.where(strict[d], kk[ci] * decay, 0.0))
        qkd.append((qk[ci] * decay).astype(BF16))

    t_inv = [eye - jnp.where(merge_masks[0], a_c, 0.0) for a_c in a]
    for mask in merge_masks[1:]:
        tb = [t.astype(BF16) for t in t_inv]
        lm = [_dot(jnp.where(mask, a_c, 0.0).astype(BF16), tb_c).astype(BF16) for a_c, tb_c in zip(a, tb)]
        t_inv = [t - _dot(tb_c, lm_c) for t, tb_c, lm_c in zip(t_inv, tb, lm)]
    tb = [t.astype(BF16) for t in t_inv]
    u_part = [_dot(tb[ci], jnp.where(lane_head == hd, rhs_v[d], 0.0).astype(BF16)) for ci, (d, hd) in enumerate(chains)]
    w_part = [_dot(tb[ci], jnp.where(lane_head == hd, rhs_k[d], 0.0).astype(BF16)) for ci, (d, hd) in enumerate(chains)]
    u_all = [sum(u_part[d * DN_HEADS + hd] for hd in heads) for d in dirs]
    w_all = [sum(w_part[d * DN_HEADS + hd] for hd in heads) for d in dirs]

    st = [st_ref[d] for d in dirs]
    blk = (lax.broadcasted_iota(jnp.int32, (w, w), 0) // DN_DK) == (lax.broadcasted_iota(jnp.int32, (w, w), 1) // DN_DV)
    kt = [r[0] for r in kt_refs]
    order = (((0, cc), (cc, pp)), ((cc, pp), (0, cc)))
    zeros_c = jnp.zeros((cc, w), F32)
    lane_head_c = lax.broadcasted_iota(jnp.int32, (cc, w), 1) // DN_DV

    def place(lo, t):
        return jnp.concatenate([t, zeros_c] if lo == 0 else [zeros_c, t], axis=0)

    nv_acc = [None, None]
    for step in range(2):
        lo = [order[d][step][0] for d in dirs]
        rows = [slice(*order[d][step]) for d in dirs]
        stb = [t.astype(BF16) for t in st]
        w_s = [_dot(w_all[d][rows[d]].astype(BF16), stb[d]) for d in dirs]
        q_s = [_dot(q[d][rows[d]], stb[d]) for d in dirs]
        nv = [u_all[d][rows[d]] - w_s[d] for d in dirs]
        for d in dirs:
            full = place(lo[d], nv[d])
            nv_acc[d] = full if nv_acc[d] is None else nv_acc[d] + full
        nvb = [t.astype(BF16) for t in nv_acc]
        intra = [_dot(qkd[ci][rows[d]], nvb[d]) for ci, (d, hd) in enumerate(chains)]
        last = [order[0][step][1] - 1, order[1][step][0]]
        g_end = [cg_x[d][last[d]:last[d] + 1, :] for d in dirs]
        nvs = [place(lo[d], nv[d] * jnp.exp(g_end[d] - cg_x[d][rows[d]])).astype(BF16) for d in dirs]
        upd = [_dot(kt[d], nvs[d]) for d in dirs]
        for d in dirs:
            o = e_cg[d][rows[d]] * q_s[d]
            for hd in heads:
                o = o + jnp.where(lane_head_c == hd, intra[d * DN_HEADS + hd], 0.0)
            o_refs[d][0, rows[d], :] = o
            st[d] = st[d] * jnp.exp(g_end[d]) + jnp.where(blk, upd[d], 0.0)
    for d in dirs:
        st_ref[d] = st[d]


def _gdn_scan(q, k, v, kt, bg, bgt, lc):
    b, s, w = q.shape
    n_pairs = s // GDN_PAIR
    ncp = lc // GDN_PAIR

    def fwd(bb, i):
        return i

    def bwd(bb, i):
        return jnp.where(i < ncp, ncp - 1 - i, n_pairs - 1 + ncp - i)

    def specs(pos):
        tok = lambda bb, i: (bb, pos(bb, i), 0)
        tokt = lambda bb, i: (bb, 0, pos(bb, i))
        return [pl.BlockSpec((1, GDN_PAIR, w), tok)] * 3 + [
            pl.BlockSpec((1, w, GDN_PAIR), tokt), pl.BlockSpec((1, GDN_PAIR, 128), tok),
            pl.BlockSpec((1, 128, GDN_PAIR), tokt)]

    return pl.pallas_call(
        _gdn_scan_kernel, grid=(b, n_pairs),
        in_specs=specs(fwd) + specs(bwd),
        out_specs=[pl.BlockSpec((1, GDN_PAIR, w), lambda bb, i: (bb, fwd(bb, i), 0)),
                   pl.BlockSpec((1, GDN_PAIR, w), lambda bb, i: (bb, bwd(bb, i), 0))],
        out_shape=[jax.ShapeDtypeStruct((b, s, w), F32)] * 2,
        scratch_shapes=[pltpu.VMEM((2, w, w), F32)],
        compiler_params=_params(("arbitrary", "arbitrary")), name="gdn_scan",
    )(q, k, v, kt, bg, bgt, q, k, v, kt, bg, bgt)


def _outproj_kernel(x_ref, oa_ref, ob_ref, oc_ref, of_ref, obw_ref, dz_ref, mod_ref, w_ref, gout_ref,
                    g64_ref, gffn_ref, wr_ref, br_ref, xo_ref, f_ref, lg_ref):
    o = of_ref[0] + obw_ref[0]
    r = lax.rsqrt(_group_sumsq(o, g64_ref[...]) * (1.0 / DN_DV) + NORM_EPS)
    od = (o * r * gout_ref[...] * _silu(dz_ref[0])).astype(BF16)
    y = (_dot(oa_ref[0], w_ref[0]) + _dot(ob_ref[0], w_ref[1]) + _dot(oc_ref[0], w_ref[2]) + _dot(od, w_ref[3]))
    x = x_ref[0] + mod_ref[0, 0, 2:3, :] * y
    xo_ref[0] = x
    ms = jnp.mean(x * x, axis=-1, keepdims=True)
    f = (x * lax.rsqrt(ms + NORM_EPS) * gffn_ref[...]) * (1.0 + mod_ref[0, 0, 4:5, :]) + mod_ref[0, 0, 3:4, :]
    f_ref[0] = f.astype(BF16)
    lg_ref[...] = lax.dot_general(wr_ref[...], f, (((1,), (1,)), ((), ())), precision=HIGHEST,
                                  preferred_element_type=F32) + br_ref[:, 0:1]


def _outproj(layer, x, oa, ob, oc, o_f, o_b, dz, mod_t, w_out4, gout, g64, gffn, wr_t, br, n_ctx_tiles):
    b, s, d = x.shape
    nt = s // TQ
    ctx_row = b
    tok = lambda i, bb: (bb, i, 0)
    const2 = lambda i, bb: (0, 0)
    slab = pl.BlockSpec((1, TQ, MIXER_W), tok)
    return pl.pallas_call(
        _outproj_kernel, grid=(nt, b),
        in_specs=[pl.BlockSpec((1, TQ, d), tok), slab, slab, slab, slab, slab, slab,
                  pl.BlockSpec((1, 1, 6, d), lambda i, bb: (layer, jnp.where(i < n_ctx_tiles, ctx_row, bb), 0, 0)),
                  pl.BlockSpec((4, MIXER_W, d), lambda i, bb: (0, 0, 0)),
                  pl.BlockSpec((1, MIXER_W), const2), pl.BlockSpec((MIXER_W, MIXER_W), const2),
                  pl.BlockSpec((1, d), const2), pl.BlockSpec((N_EXPERTS, d), const2),
                  pl.BlockSpec((N_EXPERTS, 128), const2)],
        out_specs=[pl.BlockSpec((1, TQ, d), tok), pl.BlockSpec((1, TQ, d), tok),
                   pl.BlockSpec((N_EXPERTS, TQ), lambda i, bb: (0, bb * nt + i))],
        out_shape=[jax.ShapeDtypeStruct((b, s, d), F32), jax.ShapeDtypeStruct((b, s, d), BF16),
                   jax.ShapeDtypeStruct((N_EXPERTS, b * s), F32)],
        compiler_params=_params(("arbitrary", "arbitrary")), name="out_proj",
    )(x, oa, ob, oc, o_f, o_b, dz, mod_t, w_out4, gout, g64, gffn, wr_t, br)


def _route_kernel(lg_ref, tri_ref, pos_ref, gate_ref, cnt_ref):
    x = lg_ref[...]
    e_iota = lax.broadcasted_iota(jnp.int32, x.shape, 0).astype(F32)
    work = x
    chosen = jnp.zeros(x.shape, F32)
    top = None
    den = None
    for kk in range(TOP_K):
        m = jnp.max(work, axis=0, keepdims=True)
        idx = jnp.min(jnp.where(work == m, e_iota, float(N_EXPERTS)), axis=0, keepdims=True)
        pick = e_iota == idx
        chosen = jnp.where(pick, 1.0, chosen)
        if kk == 0:
            top = m
            den = jnp.ones_like(m)
        else:
            den = den + jnp.exp(m - top)
        work = jnp.where(pick, -jnp.inf, work)
    sel = chosen > 0.5
    gate_ref[0] = jnp.where(sel, jnp.exp(x - top) / den, 0.0)
    rank = _dot(chosen.astype(BF16), tri_ref[...])
    pos_ref[0] = jnp.where(sel, rank.astype(jnp.int32), -1)
    cnt = jnp.sum(chosen, axis=1, keepdims=True).astype(jnp.int32)
    cnt_ref[0] = jnp.broadcast_to(cnt, cnt_ref.shape[1:])


def _route(logits_t, tri):
    n_exp, t = logits_t.shape
    n_tiles = t // MOE_TM
    return pl.pallas_call(
        _route_kernel, grid=(n_tiles,),
        in_specs=[pl.BlockSpec((n_exp, MOE_TM), lambda i: (0, i)), pl.BlockSpec((MOE_TM, MOE_TM), lambda i: (0, 0))],
        out_specs=[pl.BlockSpec((1, n_exp, MOE_TM), lambda i: (i, 0, 0)),
                   pl.BlockSpec((1, n_exp, MOE_TM), lambda i: (i, 0, 0)),
                   pl.BlockSpec((1, n_exp, 128), lambda i: (i, 0, 0))],
        out_shape=[jax.ShapeDtypeStruct((n_tiles, n_exp, MOE_TM), jnp.int32),
                   jax.ShapeDtypeStruct((n_tiles, n_exp, MOE_TM), F32),
                   jax.ShapeDtypeStruct((n_tiles, n_exp, 128), jnp.int32)],
        compiler_params=_params(("arbitrary",)), name="route",
    )(logits_t, tri)


def _moe_kernel(cnt_ref, x_ref, pos_ref, gate_ref, wu_ref, bu_ref, wd_ref, bd_ref, o_ref):
    t = pl.program_id(0)
    e = pl.program_id(1)
    tm = x_ref.shape[0]
    de = wd_ref.shape[1]

    @pl.when(e == 0)
    def _():
        o_ref[...] = jnp.zeros_like(o_ref)

    n_blocks = (cnt_ref[t * N_EXPERTS + e] + MOE_RB - 1) // MOE_RB
    pos_row = pos_ref[0, pl.ds(e, 1), :]
    gate_row = gate_ref[0, pl.ds(e, 1), :]

    def block(j, carry):
        slot = j * MOE_RB + lax.broadcasted_iota(jnp.int32, (MOE_RB, tm), 0)
        hit = pos_row == slot
        sel = _onehot(hit)
        xg = _dot(sel, x_ref[...]).astype(BF16)
        hgu = _dot(xg, wu_ref[0]) + bu_ref[0]
        gate = jnp.minimum(hgu[:, :de], SWIGLU_LIMIT)
        up = jnp.clip(hgu[:, de:], -SWIGLU_LIMIT, SWIGLU_LIMIT)
        hid = gate * jax.nn.sigmoid(SWIGLU_ALPHA * gate) * (up + 1.0)
        y = _dot(hid.astype(BF16), wd_ref[0]) + bd_ref[0]
        g_row = jnp.sum(jnp.where(hit, gate_row, 0.0), axis=1, keepdims=True)
        o_ref[...] += _dot_tn(sel, (y * g_row).astype(BF16))
        return carry

    lax.fori_loop(0, n_blocks, block, 0)


def _moe(counts, f_flat, pos_t, gate_t, w_up, b_up, w_down, b_down):
    t, d = f_flat.shape
    n_tiles = t // MOE_TM
    n_exp, _, de2 = w_up.shape
    de = de2 // 2
    grid_spec = pltpu.PrefetchScalarGridSpec(
        num_scalar_prefetch=1, grid=(n_tiles, n_exp),
        in_specs=[
            pl.BlockSpec((MOE_TM, d), lambda i, e, c: (i, 0)),
            pl.BlockSpec((1, n_exp, MOE_TM), lambda i, e, c: (i, 0, 0)),
            pl.BlockSpec((1, n_exp, MOE_TM), lambda i, e, c: (i, 0, 0)),
            pl.BlockSpec((1, d, de2), lambda i, e, c: (e, 0, 0)),
            pl.BlockSpec((1, 1, de2), lambda i, e, c: (e, 0, 0)),
            pl.BlockSpec((1, de, d), lambda i, e, c: (e, 0, 0)),
            pl.BlockSpec((1, 1, d), lambda i, e, c: (e, 0, 0)),
        ],
        out_specs=pl.BlockSpec((MOE_TM, d), lambda i, e, c: (i, 0)),
    )
    return pl.pallas_call(
        _moe_kernel, grid_spec=grid_spec, out_shape=jax.ShapeDtypeStruct((t, d), F32),
        compiler_params=_params(("arbitrary", "arbitrary")), name="moe_experts",
    )(counts, f_flat, pos_t, gate_t, w_up, b_up.reshape(n_exp, 1, de2), w_down, b_down.reshape(n_exp, 1, d))


def _final_kernel(x_ref, ffn_ref, mod_ref, g_ref, o_ref):
    x = x_ref[0] + mod_ref[0, 0, 5:6, :] * ffn_ref[0]
    ms = jnp.mean(x * x, axis=-1, keepdims=True)
    o_ref[0] = x * lax.rsqrt(ms + NORM_EPS) * g_ref[...]


def _final(layer, x, ffn, mod_t, gain, n_ctx_tiles, l):
    b, s, d = x.shape
    tok = lambda bb, i: (bb, i + n_ctx_tiles, 0)
    return pl.pallas_call(
        _final_kernel, grid=(b, l // TQ),
        in_specs=[pl.BlockSpec((1, TQ, d), tok), pl.BlockSpec((1, TQ, d), tok),
                  pl.BlockSpec((1, 1, 6, d), lambda bb, i: (layer, bb, 0, 0)),
                  pl.BlockSpec((1, d), lambda bb, i: (0, 0))],
        out_specs=pl.BlockSpec((1, TQ, d), lambda bb, i: (bb, i, 0)),
        out_shape=jax.ShapeDtypeStruct((b, l, d), F32),
        compiler_params=_params(("arbitrary", "arbitrary")), name="final_norm",
    )(x, ffn, mod_t, gain)


def _swap_halves(n_groups, width):
    base = np.arange(n_groups * width).reshape(n_groups, width)
    return np.concatenate([base[:, width // 2:], base[:, :width // 2]], axis=1).reshape(-1)


def _in_columns():
    sizes = (GQA_HEADS * HEAD_DIM, GQA_KV_HEADS * HEAD_DIM, GQA_KV_HEADS * HEAD_DIM, MIXER_W, MIXER_W,
             MIXER_W, MIXER_W, MIXER_W, 3 * MIXER_W, MIXER_W, 2 * DN_HEADS, 2 * DN_HEADS)
    starts = np.concatenate([[0], np.cumsum(sizes)[:-1]])
    aq, ak, av, bu, bv, cq, ck, cv, dqkv, dz, db, da = (np.arange(n) + o for n, o in zip(sizes, starts))
    grp = GQA_HEADS // GQA_KV_HEADS
    expand = np.concatenate([np.arange(HEAD_DIM) + (hd // grp) * HEAD_DIM for hd in range(GQA_HEADS)])
    sw64 = _swap_halves(GQA_HEADS, HEAD_DIM)
    sw32 = _swap_halves(2 * DIFF_HEADS, DIFF_DIM)
    ak_x, av_x = ak[expand], av[expand]
    cols = [aq, aq[sw64], ak_x, ak_x[sw64], av_x, bu, bv, cq, cq[sw32], ck, ck[sw32], cv, dqkv, dz,
            db, da, np.full(128 - 4 * DN_HEADS, -1)]
    return np.concatenate(cols), expand, sw64


def _take_cols(w, cols):
    safe = np.where(cols < 0, 0, cols)
    return jnp.where(jnp.asarray(cols >= 0)[None, :], w[:, safe], 0.0)


def _rope_tables(l, lc):
    rows = l // GRID_W
    r_idx, c_idx = np.meshgrid(np.arange(rows), np.arange(GRID_W), indexing="ij")
    row_pos = jnp.asarray(r_idx.reshape(-1), F32)
    col_pos = jnp.asarray(c_idx.reshape(-1), F32)

    def table(dim, reps):
        n = dim // 4
        inv = jnp.power(ROPE_THETA, -jnp.arange(n, dtype=F32) / n)
        ang = jnp.concatenate([row_pos[:, None] * inv, col_pos[:, None] * inv], axis=-1)
        cos, sin = jnp.cos(ang), jnp.sin(ang)
        cos_t = jnp.tile(jnp.concatenate([cos, cos], axis=-1), (1, reps))
        sin_t = jnp.tile(jnp.concatenate([-sin, sin], axis=-1), (1, reps))
        return (jnp.concatenate([jnp.ones((lc, cos_t.shape[1]), F32), cos_t], axis=0),
                jnp.concatenate([jnp.zeros((lc, sin_t.shape[1]), F32), sin_t], axis=0))

    cos_a, sin_a = table(HEAD_DIM, GQA_HEADS)
    cos_d, sin_d = table(DIFF_DIM, 2 * DIFF_HEADS)
    return jnp.stack([cos_a, sin_a, cos_d, sin_d])


def kernel(x, c, ctx, c_ctx, w_ada, b_ada, norm_mix, norm_ffn, w_in, w_out, gqa_q_norm, gqa_k_norm, gmlp_v_norm, gmlp_w_s, gmlp_b_s, diff_lambda_q1, diff_lambda_k1, diff_lambda_q2, diff_lambda_k2, diff_subln, dn_conv_w, dn_a_log, dn_dt_bias, dn_out_norm, router_w, router_b, exp_w_up, exp_b_up, exp_w_down, exp_b_down, final_norm):
    b, l, d = x.shape
    lc = ctx.shape[1]
    depth = w_ada.shape[0]
    s = lc + l
    assert lc % TQ == 0 and l % TQ == 0 and (b * s) % MOE_TM == 0 and l % GRID_W == 0
    n_ctx_tiles = lc // TQ

    rows = -(-(b + 1) // 8) * 8
    c_all = jnp.zeros((rows, d), F32).at[:b].set(c).at[b].set(c_ctx)
    mod_t = _modulation(c_all, w_ada, b_ada).transpose(0, 2, 1, 3)

    cols, expand, sw64 = _in_columns()
    rope = _rope_tables(l, lc)
    lane = np.arange(MIXER_W)
    g64 = jnp.asarray((lane[:, None] // 64) == (lane[None, :] // 64), BF16)
    tri = jnp.asarray(np.arange(MOE_TM)[:, None] < np.arange(MOE_TM)[None, :], BF16)

    xs = jnp.concatenate([ctx, x], axis=1)
    ffn = None
    for layer in range(depth):
        lam_init = 0.8 - 0.6 * math.exp(-0.3 * layer)
        w_ext = _take_cols(w_in[layer], cols).astype(BF16)
        gq = jnp.tile(gqa_q_norm[layer], GQA_HEADS)
        gk = jnp.tile(gqa_k_norm[layer], GQA_HEADS)
        vecs = jnp.zeros((8, MIXER_W), F32).at[0].set(gq).at[1].set(gq[sw64]).at[2].set(gk).at[3].set(gk[sw64])
        vecs = vecs.at[4].set(gmlp_v_norm[layer])
        bst = jnp.repeat(gmlp_b_s[layer].T, GMLP_CH, axis=1)
        xs, (qa, ka, va, out_b, qd, kd, vd, dqkv, dz, dba) = _inproj(
            layer, xs, ffn, mod_t, norm_mix[layer][None, :], w_ext, vecs, g64, rope,
            gmlp_w_s[layer].astype(BF16), bst, n_ctx_tiles)

        out_a = _attention("gqa", qa, ka, va, n_ctx_tiles, lc)
        lam_p = jnp.stack([diff_lambda_q1[layer], diff_lambda_k1[layer], diff_lambda_q2[layer], diff_lambda_k2[layer]])
        sub = jnp.tile(diff_subln[layer], DIFF_HEADS)[None, :]
        out_c = _attention("diff", qd, kd, vd, n_ctx_tiles, lc, extra=(lam_p, sub, g64), lam_init=lam_init)

        avec = jnp.zeros((2, 128), F32)
        avec = avec.at[0, 2 * DN_HEADS:4 * DN_HEADS].set(dn_a_log[layer].reshape(-1))
        avec = avec.at[1, 2 * DN_HEADS:4 * DN_HEADS].set(dn_dt_bias[layer].reshape(-1))
        gq_, gk_, gv_, gkt, gbg, gbgt = _gdn_prep(dqkv, dba, dn_conv_w[layer], avec, g64, n_ctx_tiles)
        o_f, o_b = _gdn_scan(gq_, gk_, gv_, gkt, gbg, gbgt, lc)

        xs, f, logits_t = _outproj(
            layer, xs, out_a, out_b, out_c, o_f, o_b, dz, mod_t,
            w_out[layer].reshape(4, MIXER_W, d).astype(BF16), jnp.tile(dn_out_norm[layer], DN_HEADS)[None, :], g64,
            norm_ffn[layer][None, :], router_w[layer].T, jnp.broadcast_to(router_b[layer][:, None], (N_EXPERTS, 128)),
            n_ctx_tiles)

        pos_t, gate_t, cnt = _route(logits_t, tri)
        ffn = _moe(cnt[:, :, 0].reshape(-1), f.reshape(b * s, d), pos_t, gate_t,
                   exp_w_up[layer].astype(BF16), exp_b_up[layer], exp_w_down[layer].astype(BF16), exp_b_down[layer])
        ffn = ffn.reshape(b, s, d)

    return _final(depth - 1, xs, ffn, mod_t, final_norm[None, :], n_ctx_tiles, l)
```

```python
import functools
import math

import numpy as np
import jax
import jax.numpy as jnp
from jax import lax
from jax.experimental import pallas as pl
from jax.experimental.pallas import tpu as pltpu

F32 = jnp.float32
BF16 = jnp.bfloat16
HIGHEST = lax.Precision.HIGHEST

GRID_W = 64
NORM_EPS = 1e-6
ROPE_THETA = 10000.0
HEAD_DIM = 64
GQA_HEADS = 4
GQA_KV_HEADS = 2
GMLP_GROUPS = 4
GMLP_CH = 64
GMLP_CHUNK = 128
DIFF_HEADS = 4
DIFF_DIM = 32
DN_HEADS = 4
DN_DK = 64
DN_DV = 64
DN_CHUNK = 64
N_EXPERTS = 32
TOP_K = 4
SWIGLU_LIMIT = 7.0
SWIGLU_ALPHA = 1.702

MIXER_W = 256
TQ = 256
GDN_PAIR = 2 * DN_CHUNK
MOE_TM = 1024
MOE_BM = 256
MOE_TC = 512
VMEM_LIMIT = 56 * 1024 * 1024

_SEGS = ("aq", "aqs", "ak", "aks", "av", "bu", "bv", "cq", "cqs", "ck", "cks", "cv")
OFF = {name: i * MIXER_W for i, name in enumerate(_SEGS)}
OFF["dqkv"] = len(_SEGS) * MIXER_W
OFF["dz"] = OFF["dqkv"] + 3 * MIXER_W
OFF["dba"] = OFF["dz"] + MIXER_W
W_EXT = OFF["dba"] + 128


def _dot(a, b):
    return jnp.dot(a, b, preferred_element_type=F32)


def _dot_nt(a, b):
    return lax.dot_general(a, b, (((1,), (1,)), ((), ())), preferred_element_type=F32)


def _dot_tn(a, b):
    return lax.dot_general(a, b, (((0,), (0,)), ((), ())), preferred_element_type=F32)


def _split3(x):
    hi = x.astype(BF16)
    r1 = x - hi.astype(F32)
    mid = r1.astype(BF16)
    lo = (r1 - mid.astype(F32)).astype(BF16)
    return hi, mid, lo


def _dot_sel_r(x, sel):
    hi, mid, lo = _split3(x)
    return _dot(hi, sel) + _dot(mid, sel) + _dot(lo, sel)


def _dot_sel_l(sel, x):
    hi, mid, lo = _split3(x)
    return _dot(sel, hi) + _dot(sel, mid) + _dot(sel, lo)


def _group_sumsq(x, g_same):
    x2 = x * x
    hi = x2.astype(BF16)
    lo = (x2 - hi.astype(F32)).astype(BF16)
    return _dot(hi, g_same) + _dot(lo, g_same)


def _onehot(cond):
    return jnp.where(cond, 1.0, 0.0).astype(BF16)


def _keep(cond, x):
    return jnp.where(cond, x.astype(F32), 0.0).astype(BF16)


def _silu(x):
    return x * jax.nn.sigmoid(x)


def _softplus(x):
    return jnp.maximum(x, 0.0) + jnp.log1p(jnp.exp(-jnp.abs(x)))


def _params(sem):
    return pltpu.CompilerParams(dimension_semantics=sem, vmem_limit_bytes=VMEM_LIMIT)


def _mod_kernel(c_ref, w_ref, b_ref, o_ref):
    s = _silu(c_ref[...])
    o_ref[0, 0] = jnp.dot(s, w_ref[0], precision=HIGHEST, preferred_element_type=F32) + b_ref[0]


def _modulation(c_all, w_ada, b_ada):
    depth, d, _ = w_ada.shape
    r = c_all.shape[0]
    return pl.pallas_call(
        _mod_kernel,
        grid=(depth, 6),
        in_specs=[
            pl.BlockSpec((r, d), lambda l, j: (0, 0)),
            pl.BlockSpec((1, d, d), lambda l, j: (l, 0, j)),
            pl.BlockSpec((1, 1, d), lambda l, j: (l, 0, j)),
        ],
        out_specs=pl.BlockSpec((1, 1, r, d), lambda l, j: (l, j, 0, 0)),
        out_shape=jax.ShapeDtypeStruct((depth, 6, r, d), F32),
        compiler_params=_params(("arbitrary", "arbitrary")),
        name="adaln_mod",
    )(c_all, w_ada, b_ada.reshape(depth, 1, 6 * d))


def _inproj_kernel(*refs, has_prev, scale_a, scale_d):
    if has_prev:
        x_ref, ffn_ref, modp_ref, refs = refs[0], refs[1], refs[2], refs[3:]
    else:
        x_ref, refs = refs[0], refs[1:]
    (mod_ref, gmix_ref, w_ref, vec_ref, g64_ref, rope_ref, ws_ref, bst_ref) = refs[:8]
    outs = refs[8:]
    if has_prev:
        xo_ref, outs = outs[0], outs[1:]
    (qa_ref, ka_ref, va_ref, ob_ref, qd_ref, kd_ref, vd_ref, dqkv_ref, dz_ref, dba_ref) = outs

    x = x_ref[0]
    if has_prev:
        x = x + modp_ref[0, 0, 5:6, :] * ffn_ref[0]
        xo_ref[0] = x
    ms = jnp.mean(x * x, axis=-1, keepdims=True)
    xn = x * lax.rsqrt(ms + NORM_EPS) * gmix_ref[...]
    h = xn * (1.0 + mod_ref[0, 0, 1:2, :]) + mod_ref[0, 0, 0:1, :]
    p = _dot(h.astype(BF16), w_ref[...])

    def seg(name, width=MIXER_W):
        return p[:, OFF[name]:OFF[name] + width]

    g64 = g64_ref[...]
    cos_a, sin_a, cos_d, sin_d = rope_ref[0], rope_ref[1], rope_ref[2], rope_ref[3]

    def norm_rope(x0, xs, gain, gain_s, scale):
        r = lax.rsqrt(_group_sumsq(x0, g64) * (1.0 / HEAD_DIM) + NORM_EPS)
        return ((x0 * r * gain) * cos_a + (xs * r * gain_s) * sin_a) * scale

    qa_ref[0] = norm_rope(seg("aq"), seg("aqs"), vec_ref[0:1, :], vec_ref[1:2, :], scale_a).astype(BF16)
    ka_ref[0] = norm_rope(seg("ak"), seg("aks"), vec_ref[2:3, :], vec_ref[3:4, :], 1.0).astype(BF16)
    va_ref[0] = seg("av").astype(BF16)

    u = jax.nn.gelu(seg("bu"))
    v = jax.nn.gelu(seg("bv"))
    vn = v * lax.rsqrt(jnp.mean(v * v, axis=-1, keepdims=True) + NORM_EPS) * vec_ref[4:5, :]
    lane_grp = lax.broadcasted_iota(jnp.int32, (GMLP_CHUNK, MIXER_W), 1) // GMLP_CH
    for ci in range(TQ // GMLP_CHUNK):
        rows = slice(ci * GMLP_CHUNK, (ci + 1) * GMLP_CHUNK)
        vc = vn[rows]
        sp = bst_ref[...]
        for g in range(GMLP_GROUPS):
            sp = sp + _dot(ws_ref[g], jnp.where(lane_grp == g, vc, 0.0).astype(BF16))
        ob_ref[0, rows, :] = (u[rows] * sp).astype(BF16)

    qd_ref[0] = ((seg("cq") * cos_d + seg("cqs") * sin_d) * scale_d).astype(BF16)
    kd_ref[0] = (seg("ck") * cos_d + seg("cks") * sin_d).astype(BF16)
    vd_ref[0] = seg("cv").astype(BF16)

    dqkv_ref[0] = seg("dqkv", 3 * MIXER_W)
    dz_ref[0] = seg("dz")
    dba_ref[0] = seg("dba", 128)


def _inproj(layer, x, prev, mod_t, gmix, w_ext, vecs, g64, rope, ws, bst, n_ctx_tiles):
    b, s, d = x.shape
    nt = s // TQ
    ctx_row = b

    def mod_map(l):
        return lambda i, bb: (l, jnp.where(i < n_ctx_tiles, ctx_row, bb), 0, 0)

    tok = lambda i, bb: (bb, i, 0)
    const2 = lambda i, bb: (0, 0)
    const3 = lambda i, bb: (0, 0, 0)
    in_specs = [pl.BlockSpec((1, TQ, d), tok)]
    args = [x]
    if prev is not None:
        in_specs += [pl.BlockSpec((1, TQ, d), tok), pl.BlockSpec((1, 1, 6, d), mod_map(layer - 1))]
        args += [prev, mod_t]
    in_specs += [
        pl.BlockSpec((1, 1, 6, d), mod_map(layer)),
        pl.BlockSpec((1, d), const2),
        pl.BlockSpec((d, W_EXT), const2),
        pl.BlockSpec((8, MIXER_W), const2),
        pl.BlockSpec((MIXER_W, MIXER_W), const2),
        pl.BlockSpec((4, TQ, MIXER_W), lambda i, bb: (0, i, 0)),
        pl.BlockSpec((GMLP_GROUPS, GMLP_CHUNK, GMLP_CHUNK), const3),
        pl.BlockSpec((GMLP_CHUNK, MIXER_W), const2),
    ]
    args += [mod_t, gmix, w_ext, vecs, g64, rope, ws, bst]
    bf = lambda w: jax.ShapeDtypeStruct((b, s, w), BF16)
    ff = lambda w: jax.ShapeDtypeStruct((b, s, w), F32)
    out_shape = [bf(MIXER_W)] * 7 + [ff(3 * MIXER_W), ff(MIXER_W), ff(128)]
    out_specs = [pl.BlockSpec((1, TQ, MIXER_W), tok)] * 7 + [
        pl.BlockSpec((1, TQ, 3 * MIXER_W), tok), pl.BlockSpec((1, TQ, MIXER_W), tok), pl.BlockSpec((1, TQ, 128), tok)]
    if prev is not None:
        out_shape = [ff(d)] + out_shape
        out_specs = [pl.BlockSpec((1, TQ, d), tok)] + out_specs
    outs = pl.pallas_call(
        functools.partial(_inproj_kernel, has_prev=prev is not None,
                          scale_a=HEAD_DIM ** -0.5, scale_d=DIFF_DIM ** -0.5),
        grid=(nt, b), in_specs=in_specs, out_specs=out_specs, out_shape=out_shape,
        compiler_params=_params(("arbitrary", "arbitrary")), name="in_proj",
    )(*args)
    if prev is not None:
        return outs[0], outs[1:]
    return x, outs


def _softmax_pv(qm, k, v):
    s = _dot_nt(qm, k)
    m = jnp.max(s, axis=-1, keepdims=True)
    e = jnp.exp(s - m)
    den = jnp.sum(e, axis=-1, keepdims=True)
    return _dot(e.astype(BF16), v) / den


def _gqa_kernel(q_ref, k_ref, v_ref, o_ref, *, n_ctx_tiles, lc):
    i = pl.program_id(1)
    lane_head = lax.broadcasted_iota(jnp.int32, (TQ, MIXER_W), 1) // HEAD_DIM

    def attend(k, v):
        q = q_ref[0]
        acc = jnp.zeros((TQ, MIXER_W), F32)
        for hd in range(GQA_HEADS):
            mine = lane_head == hd
            o = _softmax_pv(_keep(mine, q), k, v)
            acc = jnp.where(mine, o, acc)
        o_ref[0] = acc.astype(BF16)

    @pl.when(i < n_ctx_tiles)
    def _():
        attend(k_ref[0, :lc, :], v_ref[0, :lc, :])

    @pl.when(i >= n_ctx_tiles)
    def _():
        attend(k_ref[0], v_ref[0])


def _diff_kernel(q_ref, k_ref, v_ref, lam_ref, sub_ref, g64_ref, o_ref, *, n_ctx_tiles, lc, lam_init):
    i = pl.program_id(1)
    lane = lax.broadcasted_iota(jnp.int32, (TQ, MIXER_W), 1)
    lane_head = lane // (2 * DIFF_DIM)
    lane_map = lane // DIFF_DIM
    lp = lam_ref[...]
    lam = (jnp.exp(jnp.sum(lp[0:1] * lp[1:2], axis=-1, keepdims=True))
           - jnp.exp(jnp.sum(lp[2:3] * lp[3:4], axis=-1, keepdims=True)) + lam_init)

    def attend(k, v):
        q = q_ref[0]
        acc = jnp.zeros((TQ, MIXER_W), F32)
        for hd in range(DIFF_HEADS):
            o1 = _softmax_pv(_keep(lane_map == 2 * hd, q), k, v)
            o2 = _softmax_pv(_keep(lane_map == 2 * hd + 1, q), k, v)
            acc = jnp.where(lane_head == hd, o1 - lam * o2, acc)
        r = lax.rsqrt(_group_sumsq(acc, g64_ref[...]) * (1.0 / (2 * DIFF_DIM)) + NORM_EPS)
        o_ref[0] = (acc * r * sub_ref[...] * (1.0 - lam_init)).astype(BF16)

    @pl.when(i < n_ctx_tiles)
    def _():
        attend(k_ref[0, :lc, :], v_ref[0, :lc, :])

    @pl.when(i >= n_ctx_tiles)
    def _():
        attend(k_ref[0], v_ref[0])


def _attention(kind, q, k, v, n_ctx_tiles, lc, extra=(), lam_init=0.0):
    b, s, w = q.shape
    nt = s // TQ
    tok = lambda bb, i: (bb, i, 0)
    row = lambda bb, i: (bb, 0, 0)
    in_specs = [pl.BlockSpec((1, TQ, w), tok), pl.BlockSpec((1, s, w), row), pl.BlockSpec((1, s, w), row)]
    if kind == "gqa":
        body = functools.partial(_gqa_kernel, n_ctx_tiles=n_ctx_tiles, lc=lc)
    else:
        body = functools.partial(_diff_kernel, n_ctx_tiles=n_ctx_tiles, lc=lc, lam_init=lam_init)
        in_specs += [pl.BlockSpec(e.shape, lambda bb, i: (0, 0)) for e in extra]
    return pl.pallas_call(
        body, grid=(b, nt), in_specs=in_specs, out_specs=pl.BlockSpec((1, TQ, w), tok),
        out_shape=jax.ShapeDtypeStruct((b, s, w), BF16),
        compiler_params=_params(("arbitrary", "arbitrary")), name=kind + "_attention",
    )(q, k, v, *extra)


def _gdn_prep_kernel(x_ref, xp_ref, xn_ref, ba_ref, cw_ref, av_ref, g64_ref,
                     q_ref, k_ref, v_ref, kt_ref, bg_ref, bgt_ref, *, n_ctx_tiles, n_tiles):
    i = pl.program_id(0)
    x = x_ref[0]
    has_prev = jnp.where((i != 0) & (i != n_ctx_tiles), 1.0, 0.0)
    has_next = jnp.where((i != n_ctx_tiles - 1) & (i != n_tiles - 1), 1.0, 0.0)
    row = lax.broadcasted_iota(jnp.int32, x.shape, 0)
    x_m1 = jnp.where(row == 0, xp_ref[0, 7:8, :] * has_prev, pltpu.roll(x, 1, 0))
    x_p1 = jnp.where(row == TQ - 1, xn_ref[0, 0:1, :] * has_next, pltpu.roll(x, TQ - 1, 0))
    y = _silu(cw_ref[0:1, :] * x_m1 + cw_ref[1:2, :] * x + cw_ref[2:3, :] * x_p1)
    q, k, v = y[:, :MIXER_W], y[:, MIXER_W:2 * MIXER_W], y[:, 2 * MIXER_W:]
    g64 = g64_ref[...]
    qn = q * lax.rsqrt(_group_sumsq(q, g64) + NORM_EPS) * (DN_DK ** -0.5)
    kn = k * lax.rsqrt(_group_sumsq(k, g64) + NORM_EPS)
    q_ref[0] = qn.astype(BF16)
    k_ref[0] = kn.astype(BF16)
    v_ref[0] = v.astype(BF16)
    kt_ref[0] = kn.T.astype(BF16)
    ba = ba_ref[0]
    lane = lax.broadcasted_iota(jnp.int32, ba.shape, 1)
    beta = jax.nn.sigmoid(ba)
    g = -jnp.exp(av_ref[0:1, :]) * _softplus(ba + av_ref[1:2, :])
    bg = jnp.where(lane < 2 * DN_HEADS, beta, jnp.where(lane < 4 * DN_HEADS, g, 0.0))
    bg_ref[0] = bg
    bgt_ref[0] = bg.T


def _gdn_prep(dqkv, dba, conv_w, avec, g64, n_ctx_tiles):
    b, s, w3 = dqkv.shape
    nt = s // TQ
    nb8 = s // 8
    tok = lambda i, bb: (bb, i, 0)
    tokt = lambda i, bb: (bb, 0, i)
    const2 = lambda i, bb: (0, 0)
    outs = pl.pallas_call(
        functools.partial(_gdn_prep_kernel, n_ctx_tiles=n_ctx_tiles, n_tiles=nt),
        grid=(nt, b),
        in_specs=[
            pl.BlockSpec((1, TQ, w3), tok),
            pl.BlockSpec((1, 8, w3), lambda i, bb: (bb, jnp.maximum(i * (TQ // 8) - 1, 0), 0)),
            pl.BlockSpec((1, 8, w3), lambda i, bb: (bb, jnp.minimum((i + 1) * (TQ // 8), nb8 - 1), 0)),
            pl.BlockSpec((1, TQ, 128), tok),
            pl.BlockSpec((3, w3), const2),
            pl.BlockSpec((2, 128), const2),
            pl.BlockSpec((MIXER_W, MIXER_W), const2),
        ],
        out_specs=[pl.BlockSpec((1, TQ, MIXER_W), tok)] * 3 + [
            pl.BlockSpec((1, MIXER_W, TQ), tokt), pl.BlockSpec((1, TQ, 128), tok), pl.BlockSpec((1, 128, TQ), tokt)],
        out_shape=[jax.ShapeDtypeStruct((b, s, MIXER_W), BF16)] * 3 + [
            jax.ShapeDtypeStruct((b, MIXER_W, s), BF16), jax.ShapeDtypeStruct((b, s, 128), F32),
            jax.ShapeDtypeStruct((b, 128, s), F32)],
        compiler_params=_params(("arbitrary", "arbitrary")), name="gdn_prep",
    )(dqkv, dqkv, dqkv, dba, conv_w, avec, g64)
    return outs


def _gdn_scan_kernel(qf, kf, vf, ktf, bgf, bgtf, qb, kb, vb, ktb, bgb, bgtb, of_ref, ob_ref, st_ref):
    @pl.when(pl.program_id(1) == 0)
    def _():
        st_ref[...] = jnp.zeros_like(st_ref)

    pp, cc, w = GDN_PAIR, DN_CHUNK, MIXER_W
    dirs = (0, 1)
    heads = range(DN_HEADS)
    chains = [(d, hd) for d in dirs for hd in heads]
    q_refs, k_refs, v_refs, kt_refs = (qf, qb), (kf, kb), (vf, vb), (ktf, ktb)
    bg_refs, bgt_refs, o_refs = (bgf, bgb), (bgtf, bgtb), (of_ref, ob_ref)

    ii = lax.broadcasted_iota(jnp.int32, (pp, pp), 0)
    jj = lax.broadcasted_iota(jnp.int32, (pp, pp), 1)
    same = (ii // cc) == (jj // cc)
    incl = (same & (jj <= ii), same & (jj >= ii))
    strict = (same & (jj < ii), same & (jj > ii))
    incl_b = [_onehot(m) for m in incl]
    incl_tb = [_onehot(same & (ii <= jj)), _onehot(same & (ii >= jj))]
    eye = jnp.where(ii == jj, 1.0, 0.0)
    merge_masks = [((ii // (2 * sz)) == (jj // (2 * sz))) & ((ii // sz) != (jj // sz))
                   for sz in (2 ** e for e in range(int(math.log2(cc))))]

    src = lax.broadcasted_iota(jnp.int32, (128, w), 0)
    lane_w = lax.broadcasted_iota(jnp.int32, (128, w), 1)
    src2 = lax.broadcasted_iota(jnp.int32, (128, DN_HEADS * pp), 0)
    lane2 = lax.broadcasted_iota(jnp.int32, (128, DN_HEADS * pp), 1)
    bg = [r[0] for r in bg_refs]
    beta_x = [_dot_sel_r(bg[d], _onehot(src == DN_HEADS * d + lane_w // DN_DV)) for d in dirs]
    g_x = [_dot_sel_r(bg[d], _onehot(src == 2 * DN_HEADS + DN_HEADS * d + lane_w // DN_DV)) for d in dirs]
    g_x2 = [_dot_sel_r(bg[d], _onehot(src2 == 2 * DN_HEADS + DN_HEADS * d + lane2 // pp)) for d in dirs]
    cg_rows = [_dot_sel_r(bgt_refs[d][0], incl_tb[d]) for d in dirs]
    cg_x = [_dot_sel_l(incl_b[d], g_x[d]) for d in dirs]
    cg_x2 = [_dot_sel_l(incl_b[d], g_x2[d]) for d in dirs]

    q = [r[0] for r in q_refs]
    k = [r[0] for r in k_refs]
    kf32 = [t.astype(F32) for t in k]
    e_cg = [jnp.exp(t) for t in cg_x]
    rhs_v = [v_refs[d][0].astype(F32) * beta_x[d] for d in dirs]
    rhs_k = [kf32[d] * (beta_x[d] * e_cg[d]) for d in dirs]
    k_beta = [kf32[d] * beta_x[d] for d in dirs]
    lane_head = lax.broadcasted_iota(jnp.int32, (pp, w), 1) // DN_DV

    kk = [_dot_nt(jnp.where(lane_head == hd, k_beta[d], 0.0).astype(BF16), k[d]) for d, hd in chains]
    qk = [_dot_nt(_keep(lane_head == hd, q[d]), k[d]) for d, hd in chains]
    a, qkd = [], []
    for ci, (d, hd) in enumerate(chains):
        gl = 2 * DN_HEADS + DN_HEADS * d + hd
        diff = jnp.where(incl[d], cg_x2[d][:, hd * pp:(hd + 1) * pp] - cg_rows[d][gl:gl + 1, :], 0.0)
        decay = jnp.where(incl[d], jnp.exp(diff), 0.0)
        a.append(jnp.where(strict[d], kk[ci] * decay, 0.0))
        qkd.append((qk[ci] * decay).astype(BF16))

    t_inv = [eye - jnp.where(merge_masks[0], a_c, 0.0) for a_c in a]
    for mask in merge_masks[1:]:
        tb = [t.astype(BF16) for t in t_inv]
        lm = [_dot(jnp.where(mask, a_c, 0.0).astype(BF16), tb_c).astype(BF16) for a_c, tb_c in zip(a, tb)]
        t_inv = [t - _dot(tb_c, lm_c) for t, tb_c, lm_c in zip(t_inv, tb, lm)]
    tb = [t.astype(BF16) for t in t_inv]
    u_part = [_dot(tb[ci], jnp.where(lane_head == hd, rhs_v[d], 0.0).astype(BF16)) for ci, (d, hd) in enumerate(chains)]
    w_part = [_dot(tb[ci], jnp.where(lane_head == hd, rhs_k[d], 0.0).astype(BF16)) for ci, (d, hd) in enumerate(chains)]
    u_all = [sum(u_part[d * DN_HEADS + hd] for hd in heads) for d in dirs]
    w_all = [sum(w_part[d * DN_HEADS + hd] for hd in heads) for d in dirs]

    st = [st_ref[d] for d in dirs]
    blk = (lax.broadcasted_iota(jnp.int32, (w, w), 0) // DN_DK) == (lax.broadcasted_iota(jnp.int32, (w, w), 1) // DN_DV)
    kt = [r[0] for r in kt_refs]
    order = (((0, cc), (cc, pp)), ((cc, pp), (0, cc)))
    zeros_c = jnp.zeros((cc, w), F32)
    lane_head_c = lax.broadcasted_iota(jnp.int32, (cc, w), 1) // DN_DV

    def place(lo, t):
        return jnp.concatenate([t, zeros_c] if lo == 0 else [zeros_c, t], axis=0)

    nv_acc = [None, None]
    for step in range(2):
        lo = [order[d][step][0] for d in dirs]
        rows = [slice(*order[d][step]) for d in dirs]
        stb = [t.astype(BF16) for t in st]
        w_s = [_dot(w_all[d][rows[d]].astype(BF16), stb[d]) for d in dirs]
        q_s = [_dot(q[d][rows[d]], stb[d]) for d in dirs]
        nv = [u_all[d][rows[d]] - w_s[d] for d in dirs]
        for d in dirs:
            full = place(lo[d], nv[d])
            nv_acc[d] = full if nv_acc[d] is None else nv_acc[d] + full
        nvb = [t.astype(BF16) for t in nv_acc]
        intra = [_dot(qkd[ci][rows[d]], nvb[d]) for ci, (d, hd) in enumerate(chains)]
        last = [order[0][step][1] - 1, order[1][step][0]]
        g_end = [cg_x[d][last[d]:last[d] + 1, :] for d in dirs]
        nvs = [place(lo[d], nv[d] * jnp.exp(g_end[d] - cg_x[d][rows[d]])).astype(BF16) for d in dirs]
        upd = [_dot(kt[d], nvs[d]) for d in dirs]
        for d in dirs:
            o = e_cg[d][rows[d]] * q_s[d]
            for hd in heads:
                o = o + jnp.where(lane_head_c == hd, intra[d * DN_HEADS + hd], 0.0)
            o_refs[d][0, rows[d], :] = o
            st[d] = st[d] * jnp.exp(g_end[d]) + jnp.where(blk, upd[d], 0.0)
    for d in dirs:
        st_ref[d] = st[d]


def _gdn_scan(q, k, v, kt, bg, bgt, lc):
    b, s, w = q.shape
    n_pairs = s // GDN_PAIR
    ncp = lc // GDN_PAIR

    def fwd(bb, i):
        return i

    def bwd(bb, i):
        return jnp.where(i < ncp, ncp - 1 - i, n_pairs - 1 + ncp - i)

    def specs(pos):
        tok = lambda bb, i: (bb, pos(bb, i), 0)
        tokt = lambda bb, i: (bb, 0, pos(bb, i))
        return [pl.BlockSpec((1, GDN_PAIR, w), tok)] * 3 + [
            pl.BlockSpec((1, w, GDN_PAIR), tokt), pl.BlockSpec((1, GDN_PAIR, 128), tok),
            pl.BlockSpec((1, 128, GDN_PAIR), tokt)]

    return pl.pallas_call(
        _gdn_scan_kernel, grid=(b, n_pairs),
        in_specs=specs(fwd) + specs(bwd),
        out_specs=[pl.BlockSpec((1, GDN_PAIR, w), lambda bb, i: (bb, fwd(bb, i), 0)),
                   pl.BlockSpec((1, GDN_PAIR, w), lambda bb, i: (bb, bwd(bb, i), 0))],
        out_shape=[jax.ShapeDtypeStruct((b, s, w), F32)] * 2,
        scratch_shapes=[pltpu.VMEM((2, w, w), F32)],
        compiler_params=_params(("arbitrary", "arbitrary")), name="gdn_scan",
    )(q, k, v, kt, bg, bgt, q, k, v, kt, bg, bgt)


def _outproj_kernel(x_ref, oa_ref, ob_ref, oc_ref, of_ref, obw_ref, dz_ref, mod_ref, w_ref, gout_ref,
                    g64_ref, gffn_ref, wr_ref, br_ref, xo_ref, f_ref, lg_ref):
    o = of_ref[0] + obw_ref[0]
    r = lax.rsqrt(_group_sumsq(o, g64_ref[...]) * (1.0 / DN_DV) + NORM_EPS)
    od = (o * r * gout_ref[...] * _silu(dz_ref[0])).astype(BF16)
    y = (_dot(oa_ref[0], w_ref[0]) + _dot(ob_ref[0], w_ref[1]) + _dot(oc_ref[0], w_ref[2]) + _dot(od, w_ref[3]))
    x = x_ref[0] + mod_ref[0, 0, 2:3, :] * y
    xo_ref[0] = x
    ms = jnp.mean(x * x, axis=-1, keepdims=True)
    f = (x * lax.rsqrt(ms + NORM_EPS) * gffn_ref[...]) * (1.0 + mod_ref[0, 0, 4:5, :]) + mod_ref[0, 0, 3:4, :]
    for j in range(f_ref.shape[1]):
        f_ref[:, j, :] = f[:, j * 128:(j + 1) * 128]
    lg_ref[...] = lax.dot_general(wr_ref[...], f, (((1,), (1,)), ((), ())), precision=HIGHEST,
                                  preferred_element_type=F32) + br_ref[:, 0:1]


def _outproj(layer, x, oa, ob, oc, o_f, o_b, dz, mod_t, w_out4, gout, g64, gffn, wr_t, br, n_ctx_tiles):
    b, s, d = x.shape
    nt = s // TQ
    ctx_row = b
    tok = lambda i, bb: (bb, i, 0)
    const2 = lambda i, bb: (0, 0)
    slab = pl.BlockSpec((1, TQ, MIXER_W), tok)
    return pl.pallas_call(
        _outproj_kernel, grid=(nt, b),
        in_specs=[pl.BlockSpec((1, TQ, d), tok), slab, slab, slab, slab, slab, slab,
                  pl.BlockSpec((1, 1, 6, d), lambda i, bb: (layer, jnp.where(i < n_ctx_tiles, ctx_row, bb), 0, 0)),
                  pl.BlockSpec((4, MIXER_W, d), lambda i, bb: (0, 0, 0)),
                  pl.BlockSpec((1, MIXER_W), const2), pl.BlockSpec((MIXER_W, MIXER_W), const2),
                  pl.BlockSpec((1, d), const2), pl.BlockSpec((N_EXPERTS, d), const2),
                  pl.BlockSpec((N_EXPERTS, 128), const2)],
        out_specs=[pl.BlockSpec((1, TQ, d), tok), pl.BlockSpec((TQ, d // 128, 128), lambda i, bb: (bb * nt + i, 0, 0)),
                   pl.BlockSpec((N_EXPERTS, TQ), lambda i, bb: (0, bb * nt + i))],
        out_shape=[jax.ShapeDtypeStruct((b, s, d), F32), jax.ShapeDtypeStruct((b * s, d // 128, 128), F32),
                   jax.ShapeDtypeStruct((N_EXPERTS, b * s), F32)],
        compiler_params=_params(("arbitrary", "arbitrary")), name="out_proj",
    )(x, oa, ob, oc, o_f, o_b, dz, mod_t, w_out4, gout, g64, gffn, wr_t, br)


def _route_kernel(lg_ref, tri_ref, pos_ref, gate_ref, cnt_ref):
    x = lg_ref[...]
    e_iota = lax.broadcasted_iota(jnp.int32, x.shape, 0).astype(F32)
    work = x
    chosen = jnp.zeros(x.shape, F32)
    top = None
    den = None
    for kk in range(TOP_K):
        m = jnp.max(work, axis=0, keepdims=True)
        idx = jnp.min(jnp.where(work == m, e_iota, float(N_EXPERTS)), axis=0, keepdims=True)
        pick = e_iota == idx
        chosen = jnp.where(pick, 1.0, chosen)
        if kk == 0:
            top = m
            den = jnp.ones_like(m)
        else:
            den = den + jnp.exp(m - top)
        work = jnp.where(pick, -jnp.inf, work)
    sel = chosen > 0.5
    gate_ref[0] = jnp.where(sel, jnp.exp(x - top) / den, 0.0)
    rank = _dot(chosen.astype(BF16), tri_ref[...])
    pos_ref[0] = jnp.where(sel, rank.astype(jnp.int32), -1)
    cnt = jnp.sum(chosen, axis=1, keepdims=True).astype(jnp.int32)
    cnt_ref[0] = jnp.broadcast_to(cnt, cnt_ref.shape[1:])


def _route(logits_t, tri):
    n_exp, t = logits_t.shape
    n_tiles = t // MOE_TM
    return pl.pallas_call(
        _route_kernel, grid=(n_tiles,),
        in_specs=[pl.BlockSpec((n_exp, MOE_TM), lambda i: (0, i)), pl.BlockSpec((MOE_TM, MOE_TM), lambda i: (0, 0))],
        out_specs=[pl.BlockSpec((1, n_exp, MOE_TM), lambda i: (i, 0, 0)),
                   pl.BlockSpec((1, n_exp, MOE_TM), lambda i: (i, 0, 0)),
                   pl.BlockSpec((1, n_exp, 128), lambda i: (i, 0, 0))],
        out_shape=[jax.ShapeDtypeStruct((n_tiles, n_exp, MOE_TM), jnp.int32),
                   jax.ShapeDtypeStruct((n_tiles, n_exp, MOE_TM), F32),
                   jax.ShapeDtypeStruct((n_tiles, n_exp, 128), jnp.int32)],
        compiler_params=_params(("arbitrary",)), name="route",
    )(logits_t, tri)


def _slots_kernel(pos_ref, gate_ref, base_ref, tril_ref, dest_ref, gk_ref):
    pos = pos_ref[0]
    chosen = pos >= 0
    slot = (base_ref[0][:, 0:1] + pos).astype(F32)
    choice = _dot(tril_ref[...], _onehot(chosen))
    gate = gate_ref[0]
    pad = jnp.zeros((8 - TOP_K, pos.shape[1]), F32)
    d_rows, g_rows = [], []
    for kk in range(TOP_K):
        mine = chosen & (choice == float(kk))
        d_rows.append(jnp.sum(jnp.where(mine, slot, 0.0), axis=0, keepdims=True))
        g_rows.append(jnp.sum(jnp.where(mine, gate, 0.0), axis=0, keepdims=True))
    dest_ref[0] = jnp.concatenate(d_rows + [pad], axis=0).astype(jnp.int32)
    gk_ref[0] = jnp.concatenate(g_rows + [pad], axis=0)


def _slots(pos_t, gate_t, base_b, tril):
    n_tiles, n_exp, tm = pos_t.shape
    tile = lambda i: (i, 0, 0)
    return pl.pallas_call(
        _slots_kernel, grid=(n_tiles,),
        in_specs=[pl.BlockSpec((1, n_exp, tm), tile), pl.BlockSpec((1, n_exp, tm), tile),
                  pl.BlockSpec((1, n_exp, 128), tile), pl.BlockSpec((n_exp, n_exp), lambda i: (0, 0))],
        out_specs=[pl.BlockSpec((1, 8, tm), tile), pl.BlockSpec((1, 8, tm), tile)],
        out_shape=[jax.ShapeDtypeStruct((n_tiles, 8, tm), jnp.int32), jax.ShapeDtypeStruct((n_tiles, 8, tm), F32)],
        compiler_params=_params(("arbitrary",)), name="moe_slots",
    )(pos_t, gate_t, base_b, tril)


def _dispatch_kernel(pad_ref, f_ref, dest_ref, xs_ref, zbuf, zsem, sem):
    tm = f_ref.shape[0]
    n_blocks = xs_ref.shape[0] // MOE_BM

    @pl.when(pl.program_id(0) == 0)
    def _():
        zbuf[...] = jnp.zeros_like(zbuf)
        live = pad_ref[2 * N_EXPERTS]

        def zero_block(row):
            return pltpu.make_async_copy(zbuf, xs_ref.at[pl.ds(pl.multiple_of(row, MOE_BM), MOE_BM)], zsem)

        for wait in (False, True):
            for e in range(N_EXPERTS):
                for cond, row in ((pad_ref[N_EXPERTS + e] > 0, pad_ref[e]), (live + e < n_blocks, (live + e) * MOE_BM)):
                    @pl.when(cond)
                    def _():
                        zero_block(row).wait() if wait else zero_block(row).start()

    def rows(tok, carry):
        for kk in range(TOP_K):
            pltpu.make_async_copy(f_ref.at[tok], xs_ref.at[dest_ref[0, kk, tok]], sem).start()
        return carry

    lax.fori_loop(0, tm, rows, 0, unroll=8)
    for kk in range(TOP_K):
        pltpu.make_async_copy(f_ref, xs_ref.at[pl.ds(0, tm)], sem).wait()


def _dispatch(pad_info, f3, dest, n_slots):
    t, nj, _ = f3.shape
    n_tiles, _, tm = dest.shape
    grid_spec = pltpu.PrefetchScalarGridSpec(
        num_scalar_prefetch=1, grid=(n_tiles,),
        in_specs=[pl.BlockSpec((tm, nj, 128), lambda i, p: (i, 0, 0)),
                  pl.BlockSpec((1, 8, tm), lambda i, p: (i, 0, 0), memory_space=pltpu.SMEM)],
        out_specs=pl.BlockSpec(memory_space=pl.ANY),
        scratch_shapes=[pltpu.VMEM((MOE_BM, nj, 128), F32), pltpu.SemaphoreType.DMA(()), pltpu.SemaphoreType.DMA(())],
    )
    return pl.pallas_call(
        _dispatch_kernel, grid_spec=grid_spec, out_shape=jax.ShapeDtypeStruct((n_slots, nj, 128), F32),
        compiler_params=_params(("arbitrary",)), name="moe_dispatch",
    )(pad_info, f3, dest)


def _experts_kernel(meta_ref, xs_ref, wu_ref, bu_ref, wd_ref, bd_ref, ys_ref, wu_bf, wd_bf):
    i = pl.program_id(0)
    nb = pl.num_programs(0)
    live = i < meta_ref[2 * nb]
    nj = xs_ref.shape[1]
    de = wd_bf.shape[0]

    @pl.when(live & (meta_ref[nb + i] > 0))
    def _():
        wu_bf[...] = wu_ref[0].astype(BF16)
        wd_bf[...] = wd_ref[0].astype(BF16)

    @pl.when(live)
    def _():
        x = jnp.concatenate([xs_ref[:, j, :] for j in range(nj)], axis=1).astype(BF16)
        hgu = _dot(x, wu_bf[...]) + bu_ref[0]
        gate = jnp.minimum(hgu[:, :de], SWIGLU_LIMIT)
        up = jnp.clip(hgu[:, de:], -SWIGLU_LIMIT, SWIGLU_LIMIT)
        hid = gate * jax.nn.sigmoid(SWIGLU_ALPHA * gate) * (up + 1.0)
        y = _dot(hid.astype(BF16), wd_bf[...]) + bd_ref[0]
        for j in range(nj):
            ys_ref[:, j, :] = y[:, j * 128:(j + 1) * 128]

    @pl.when(jnp.logical_not(live))
    def _():
        ys_ref[...] = jnp.zeros_like(ys_ref)


def _experts(meta, xs, w_up, b_up, w_down, b_down):
    n_slots, nj, _ = xs.shape
    nb = n_slots // MOE_BM
    n_exp, d, de2 = w_up.shape
    de = de2 // 2
    blk = lambda i, m: (jnp.minimum(i, m[2 * nb] - 1), 0, 0)
    exp = lambda i, m: (m[i], 0, 0)
    grid_spec = pltpu.PrefetchScalarGridSpec(
        num_scalar_prefetch=1, grid=(nb,),
        in_specs=[pl.BlockSpec((MOE_BM, nj, 128), blk),
                  pl.BlockSpec((1, d, de2), exp), pl.BlockSpec((1, 1, de2), exp),
                  pl.BlockSpec((1, de, d), exp), pl.BlockSpec((1, 1, d), exp)],
        out_specs=pl.BlockSpec((MOE_BM, nj, 128), lambda i, m: (i, 0, 0)),
        scratch_shapes=[pltpu.VMEM((d, de2), BF16), pltpu.VMEM((de, d), BF16)],
    )
    return pl.pallas_call(
        _experts_kernel, grid_spec=grid_spec, out_shape=jax.ShapeDtypeStruct((n_slots, nj, 128), F32),
        compiler_params=_params(("arbitrary",)), name="moe_experts",
    )(meta, xs, w_up, b_up.reshape(n_exp, 1, de2), w_down, b_down.reshape(n_exp, 1, d))


def _combine_kernel(dest_ref, gk_ref, ys_ref, o_ref, ybuf, sem):
    tc = o_ref.shape[0]
    nj = ybuf.shape[2]

    def rows(tok, carry):
        for kk in range(TOP_K):
            pltpu.make_async_copy(ys_ref.at[dest_ref[0, kk, tok]], ybuf.at[kk, tok], sem).start()
        return carry

    lax.fori_loop(0, tc, rows, 0, unroll=8)
    for kk in range(TOP_K):
        pltpu.make_async_copy(ys_ref.at[pl.ds(0, tc)], ybuf.at[kk], sem).wait()

    hi, mid, lo = _split3(gk_ref[0])
    pick = lax.broadcasted_iota(jnp.int32, (8, 128), 0)
    cols = []
    for kk in range(TOP_K):
        sel = _onehot(pick == kk)
        cols.append(_dot_tn(hi, sel) + _dot_tn(mid, sel) + _dot_tn(lo, sel))
    for j in range(nj):
        acc = cols[0] * ybuf[0, :, j, :]
        for kk in range(1, TOP_K):
            acc = acc + cols[kk] * ybuf[kk, :, j, :]
        o_ref[:, j * 128:(j + 1) * 128] = acc


def _combine(dest, gk, ys):
    n_tiles, _, tm = dest.shape
    _, nj, _ = ys.shape
    per = tm // MOE_TC
    blk = lambda i: (i // per, 0, i % per)
    return pl.pallas_call(
        _combine_kernel, grid=(n_tiles * per,),
        in_specs=[pl.BlockSpec((1, 8, MOE_TC), blk, memory_space=pltpu.SMEM), pl.BlockSpec((1, 8, MOE_TC), blk),
                  pl.BlockSpec(memory_space=pl.ANY)],
        out_specs=pl.BlockSpec((MOE_TC, nj * 128), lambda i: (i, 0)),
        out_shape=jax.ShapeDtypeStruct((n_tiles * tm, nj * 128), F32),
        scratch_shapes=[pltpu.VMEM((TOP_K, MOE_TC, nj, 128), F32), pltpu.SemaphoreType.DMA(())],
        compiler_params=_params(("arbitrary",)), name="moe_combine",
    )(dest, gk, ys)


def _moe(f3, pos_t, gate_t, cnt, tril, w_up, b_up, w_down, b_down):
    t = f3.shape[0]
    nb = -(-(t * TOP_K + N_EXPERTS * (MOE_BM - 1)) // MOE_BM)
    per_expert = jnp.sum(cnt, axis=0)
    blocks_e = (per_expert + MOE_BM - 1) // MOE_BM
    blk_end = jnp.cumsum(blocks_e)
    blk_start = blk_end - blocks_e
    base = (blk_start * MOE_BM)[None, :] + jnp.cumsum(cnt, axis=0) - cnt
    live = blk_end[-1]
    blk_id = jnp.minimum(jnp.arange(nb, dtype=jnp.int32), live - 1)
    blk_expert = jnp.minimum(jnp.searchsorted(blk_end, blk_id, side="right"), N_EXPERTS - 1).astype(jnp.int32)
    first = (jnp.arange(nb, dtype=jnp.int32) == blk_start[blk_expert]).astype(jnp.int32)
    meta = jnp.concatenate([blk_expert, first, live[None]]).astype(jnp.int32)
    pad_info = jnp.concatenate([jnp.maximum(blk_end - 1, 0) * MOE_BM, (blocks_e > 0).astype(jnp.int32),
                                live[None]]).astype(jnp.int32)
    base_b = jnp.broadcast_to(base[:, :, None], base.shape + (128,)).astype(jnp.int32)

    dest, gk = _slots(pos_t, gate_t, base_b, tril)
    xs = _dispatch(pad_info, f3, dest, nb * MOE_BM)
    ys = _experts(meta, xs, w_up, b_up, w_down, b_down)
    return _combine(dest, gk, ys)


def _final_kernel(x_ref, ffn_ref, mod_ref, g_ref, o_ref):
    x = x_ref[0] + mod_ref[0, 0, 5:6, :] * ffn_ref[0]
    ms = jnp.mean(x * x, axis=-1, keepdims=True)
    o_ref[0] = x * lax.rsqrt(ms + NORM_EPS) * g_ref[...]


def _final(layer, x, ffn, mod_t, gain, n_ctx_tiles, l):
    b, s, d = x.shape
    tok = lambda bb, i: (bb, i + n_ctx_tiles, 0)
    return pl.pallas_call(
        _final_kernel, grid=(b, l // TQ),
        in_specs=[pl.BlockSpec((1, TQ, d), tok), pl.BlockSpec((1, TQ, d), tok),
                  pl.BlockSpec((1, 1, 6, d), lambda bb, i: (layer, bb, 0, 0)),
                  pl.BlockSpec((1, d), lambda bb, i: (0, 0))],
        out_specs=pl.BlockSpec((1, TQ, d), lambda bb, i: (bb, i, 0)),
        out_shape=jax.ShapeDtypeStruct((b, l, d), F32),
        compiler_params=_params(("arbitrary", "arbitrary")), name="final_norm",
    )(x, ffn, mod_t, gain)


def _swap_halves(n_groups, width):
    base = np.arange(n_groups * width).reshape(n_groups, width)
    return np.concatenate([base[:, width // 2:], base[:, :width // 2]], axis=1).reshape(-1)


def _in_columns():
    sizes = (GQA_HEADS * HEAD_DIM, GQA_KV_HEADS * HEAD_DIM, GQA_KV_HEADS * HEAD_DIM, MIXER_W, MIXER_W,
             MIXER_W, MIXER_W, MIXER_W, 3 * MIXER_W, MIXER_W, 2 * DN_HEADS, 2 * DN_HEADS)
    starts = np.concatenate([[0], np.cumsum(sizes)[:-1]])
    aq, ak, av, bu, bv, cq, ck, cv, dqkv, dz, db, da = (np.arange(n) + o for n, o in zip(sizes, starts))
    grp = GQA_HEADS // GQA_KV_HEADS
    expand = np.concatenate([np.arange(HEAD_DIM) + (hd // grp) * HEAD_DIM for hd in range(GQA_HEADS)])
    sw64 = _swap_halves(GQA_HEADS, HEAD_DIM)
    sw32 = _swap_halves(2 * DIFF_HEADS, DIFF_DIM)
    ak_x, av_x = ak[expand], av[expand]
    cols = [aq, aq[sw64], ak_x, ak_x[sw64], av_x, bu, bv, cq, cq[sw32], ck, ck[sw32], cv, dqkv, dz,
            db, da, np.full(128 - 4 * DN_HEADS, -1)]
    return np.concatenate(cols), expand, sw64


def _take_cols(w, cols):
    safe = np.where(cols < 0, 0, cols)
    return jnp.where(jnp.asarray(cols >= 0)[None, :], w[:, safe], 0.0)


def _rope_tables(l, lc):
    rows = l // GRID_W
    r_idx, c_idx = np.meshgrid(np.arange(rows), np.arange(GRID_W), indexing="ij")
    row_pos = jnp.asarray(r_idx.reshape(-1), F32)
    col_pos = jnp.asarray(c_idx.reshape(-1), F32)

    def table(dim, reps):
        n = dim // 4
        inv = jnp.power(ROPE_THETA, -jnp.arange(n, dtype=F32) / n)
        ang = jnp.concatenate([row_pos[:, None] * inv, col_pos[:, None] * inv], axis=-1)
        cos, sin = jnp.cos(ang), jnp.sin(ang)
        cos_t = jnp.tile(jnp.concatenate([cos, cos], axis=-1), (1, reps))
        sin_t = jnp.tile(jnp.concatenate([-sin, sin], axis=-1), (1, reps))
        return (jnp.concatenate([jnp.ones((lc, cos_t.shape[1]), F32), cos_t], axis=0),
                jnp.concatenate([jnp.zeros((lc, sin_t.shape[1]), F32), sin_t], axis=0))

    cos_a, sin_a = table(HEAD_DIM, GQA_HEADS)
    cos_d, sin_d = table(DIFF_DIM, 2 * DIFF_HEADS)
    return jnp.stack([cos_a, sin_a, cos_d, sin_d])


def kernel(x, c, ctx, c_ctx, w_ada, b_ada, norm_mix, norm_ffn, w_in, w_out, gqa_q_norm, gqa_k_norm, gmlp_v_norm, gmlp_w_s, gmlp_b_s, diff_lambda_q1, diff_lambda_k1, diff_lambda_q2, diff_lambda_k2, diff_subln, dn_conv_w, dn_a_log, dn_dt_bias, dn_out_norm, router_w, router_b, exp_w_up, exp_b_up, exp_w_down, exp_b_down, final_norm):
    b, l, d = x.shape
    lc = ctx.shape[1]
    depth = w_ada.shape[0]
    s = lc + l
    assert lc % TQ == 0 and l % TQ == 0 and (b * s) % MOE_TM == 0 and l % GRID_W == 0
    n_ctx_tiles = lc // TQ

    rows = -(-(b + 1) // 8) * 8
    c_all = jnp.zeros((rows, d), F32).at[:b].set(c).at[b].set(c_ctx)
    mod_t = _modulation(c_all, w_ada, b_ada).transpose(0, 2, 1, 3)

    cols, expand, sw64 = _in_columns()
    rope = _rope_tables(l, lc)
    lane = np.arange(MIXER_W)
    g64 = jnp.asarray((lane[:, None] // 64) == (lane[None, :] // 64), BF16)
    tri = jnp.asarray(np.arange(MOE_TM)[:, None] < np.arange(MOE_TM)[None, :], BF16)
    tril = jnp.asarray(np.arange(N_EXPERTS)[:, None] > np.arange(N_EXPERTS)[None, :], BF16)

    xs = jnp.concatenate([ctx, x], axis=1)
    ffn = None
    for layer in range(depth):
        lam_init = 0.8 - 0.6 * math.exp(-0.3 * layer)
        w_ext = _take_cols(w_in[layer], cols).astype(BF16)
        gq = jnp.tile(gqa_q_norm[layer], GQA_HEADS)
        gk = jnp.tile(gqa_k_norm[layer], GQA_HEADS)
        vecs = jnp.zeros((8, MIXER_W), F32).at[0].set(gq).at[1].set(gq[sw64]).at[2].set(gk).at[3].set(gk[sw64])
        vecs = vecs.at[4].set(gmlp_v_norm[layer])
        bst = jnp.repeat(gmlp_b_s[layer].T, GMLP_CH, axis=1)
        xs, (qa, ka, va, out_b, qd, kd, vd, dqkv, dz, dba) = _inproj(
            layer, xs, ffn, mod_t, norm_mix[layer][None, :], w_ext, vecs, g64, rope,
            gmlp_w_s[layer].astype(BF16), bst, n_ctx_tiles)

        out_a = _attention("gqa", qa, ka, va, n_ctx_tiles, lc)
        lam_p = jnp.stack([diff_lambda_q1[layer], diff_lambda_k1[layer], diff_lambda_q2[layer], diff_lambda_k2[layer]])
        sub = jnp.tile(diff_subln[layer], DIFF_HEADS)[None, :]
        out_c = _attention("diff", qd, kd, vd, n_ctx_tiles, lc, extra=(lam_p, sub, g64), lam_init=lam_init)

        avec = jnp.zeros((2, 128), F32)
        avec = avec.at[0, 2 * DN_HEADS:4 * DN_HEADS].set(dn_a_log[layer].reshape(-1))
        avec = avec.at[1, 2 * DN_HEADS:4 * DN_HEADS].set(dn_dt_bias[layer].reshape(-1))
        gq_, gk_, gv_, gkt, gbg, gbgt = _gdn_prep(dqkv, dba, dn_conv_w[layer], avec, g64, n_ctx_tiles)
        o_f, o_b = _gdn_scan(gq_, gk_, gv_, gkt, gbg, gbgt, lc)

        xs, f, logits_t = _outproj(
            layer, xs, out_a, out_b, out_c, o_f, o_b, dz, mod_t,
            w_out[layer].reshape(4, MIXER_W, d).astype(BF16), jnp.tile(dn_out_norm[layer], DN_HEADS)[None, :], g64,
            norm_ffn[layer][None, :], router_w[layer].T, jnp.broadcast_to(router_b[layer][:, None], (N_EXPERTS, 128)),
            n_ctx_tiles)

        pos_t, gate_t, cnt = _route(logits_t, tri)
        ffn = _moe(f, pos_t, gate_t, cnt[:, :, 0], tril, exp_w_up[layer], exp_b_up[layer], exp_w_down[layer],
                   exp_b_down[layer])
        ffn = ffn.reshape(b, s, d)

    return _final(depth - 1, xs, ffn, mod_t, final_norm[None, :], n_ctx_tiles, l)
```

```python
import functools
import math

import numpy as np
import jax
import jax.numpy as jnp
from jax import lax
from jax.experimental import pallas as pl
from jax.experimental.pallas import tpu as pltpu

F32 = jnp.float32
BF16 = jnp.bfloat16
HIGHEST = lax.Precision.HIGHEST

GRID_W = 64
NORM_EPS = 1e-6
ROPE_THETA = 10000.0
HEAD_DIM = 64
GQA_HEADS = 4
GQA_KV_HEADS = 2
GMLP_GROUPS = 4
GMLP_CH = 64
GMLP_CHUNK = 128
DIFF_HEADS = 4
DIFF_DIM = 32
DN_HEADS = 4
DN_DK = 64
DN_DV = 64
DN_CHUNK = 64
N_EXPERTS = 32
TOP_K = 4
SWIGLU_LIMIT = 7.0
SWIGLU_ALPHA = 1.702

MIXER_W = 256
TQ = 256
GDN_PAIR = 2 * DN_CHUNK
MOE_TM = 1024
MOE_BM = 256
MOE_TC = 512
VMEM_LIMIT = 56 * 1024 * 1024

_SEGS = ("aq", "aqs", "ak", "aks", "av", "bu", "bv", "cq", "cqs", "ck", "cks", "cv")
OFF = {name: i * MIXER_W for i, name in enumerate(_SEGS)}
OFF["dqkv"] = len(_SEGS) * MIXER_W
OFF["dz"] = OFF["dqkv"] + 3 * MIXER_W
OFF["dba"] = OFF["dz"] + MIXER_W
W_EXT = OFF["dba"] + 128


def _dot(a, b):
    return jnp.dot(a, b, preferred_element_type=F32)


def _dot_nt(a, b):
    return lax.dot_general(a, b, (((1,), (1,)), ((), ())), preferred_element_type=F32)


def _dot_tn(a, b):
    return lax.dot_general(a, b, (((0,), (0,)), ((), ())), preferred_element_type=F32)


def _split3(x):
    hi = x.astype(BF16)
    r1 = x - hi.astype(F32)
    mid = r1.astype(BF16)
    lo = (r1 - mid.astype(F32)).astype(BF16)
    return hi, mid, lo


def _dot_sel_r(x, sel):
    hi, mid, lo = _split3(x)
    return _dot(hi, sel) + _dot(mid, sel) + _dot(lo, sel)


def _dot_sel_l(sel, x):
    hi, mid, lo = _split3(x)
    return _dot(sel, hi) + _dot(sel, mid) + _dot(sel, lo)


def _group_sumsq(x, g_same):
    x2 = x * x
    hi = x2.astype(BF16)
    lo = (x2 - hi.astype(F32)).astype(BF16)
    return _dot(hi, g_same) + _dot(lo, g_same)


def _onehot(cond):
    return jnp.where(cond, 1.0, 0.0).astype(BF16)


def _keep(cond, x):
    return jnp.where(cond, x.astype(F32), 0.0).astype(BF16)


def _silu(x):
    return x * jax.nn.sigmoid(x)


def _softplus(x):
    return jnp.maximum(x, 0.0) + jnp.log1p(jnp.exp(-jnp.abs(x)))


def _params(sem):
    return pltpu.CompilerParams(dimension_semantics=sem, vmem_limit_bytes=VMEM_LIMIT)


def _mod_kernel(c_ref, w_ref, b_ref, o_ref):
    s = _silu(c_ref[...])
    o_ref[0, 0] = jnp.dot(s, w_ref[0], precision=HIGHEST, preferred_element_type=F32) + b_ref[0]


def _modulation(c_all, w_ada, b_ada):
    depth, d, _ = w_ada.shape
    r = c_all.shape[0]
    return pl.pallas_call(
        _mod_kernel,
        grid=(depth, 6),
        in_specs=[
            pl.BlockSpec((r, d), lambda l, j: (0, 0)),
            pl.BlockSpec((1, d, d), lambda l, j: (l, 0, j)),
            pl.BlockSpec((1, 1, d), lambda l, j: (l, 0, j)),
        ],
        out_specs=pl.BlockSpec((1, 1, r, d), lambda l, j: (l, j, 0, 0)),
        out_shape=jax.ShapeDtypeStruct((depth, 6, r, d), F32),
        compiler_params=_params(("arbitrary", "arbitrary")),
        name="adaln_mod",
    )(c_all, w_ada, b_ada.reshape(depth, 1, 6 * d))


def _inproj_kernel(*refs, has_prev, scale_a, scale_d):
    if has_prev:
        x_ref, ffn_ref, modp_ref, refs = refs[0], refs[1], refs[2], refs[3:]
    else:
        x_ref, refs = refs[0], refs[1:]
    (mod_ref, gmix_ref, w_ref, vec_ref, g64_ref, rope_ref, ws_ref, bst_ref) = refs[:8]
    outs = refs[8:]
    if has_prev:
        xo_ref, outs = outs[0], outs[1:]
    (qa_ref, ka_ref, va_ref, ob_ref, qd_ref, kd_ref, vd_ref, dqkv_ref, dz_ref, dba_ref) = outs

    x = x_ref[0]
    if has_prev:
        x = x + modp_ref[0, 0, 5:6, :] * ffn_ref[0]
        xo_ref[0] = x
    ms = jnp.mean(x * x, axis=-1, keepdims=True)
    xn = x * lax.rsqrt(ms + NORM_EPS) * gmix_ref[...]
    h = xn * (1.0 + mod_ref[0, 0, 1:2, :]) + mod_ref[0, 0, 0:1, :]
    p = _dot(h.astype(BF16), w_ref[...])

    def seg(name, width=MIXER_W):
        return p[:, OFF[name]:OFF[name] + width]

    g64 = g64_ref[...]
    cos_a, sin_a, cos_d, sin_d = rope_ref[0], rope_ref[1], rope_ref[2], rope_ref[3]

    def norm_rope(x0, xs, gain, gain_s, scale):
        r = lax.rsqrt(_group_sumsq(x0, g64) * (1.0 / HEAD_DIM) + NORM_EPS)
        return ((x0 * r * gain) * cos_a + (xs * r * gain_s) * sin_a) * scale

    qa_ref[0] = norm_rope(seg("aq"), seg("aqs"), vec_ref[0:1, :], vec_ref[1:2, :], scale_a).astype(BF16)
    ka_ref[0] = norm_rope(seg("ak"), seg("aks"), vec_ref[2:3, :], vec_ref[3:4, :], 1.0).astype(BF16)
    va_ref[0] = seg("av").astype(BF16)

    u = jax.nn.gelu(seg("bu"))
    v = jax.nn.gelu(seg("bv"))
    vn = v * lax.rsqrt(jnp.mean(v * v, axis=-1, keepdims=True) + NORM_EPS) * vec_ref[4:5, :]
    lane_grp = lax.broadcasted_iota(jnp.int32, (GMLP_CHUNK, MIXER_W), 1) // GMLP_CH
    for ci in range(TQ // GMLP_CHUNK):
        rows = slice(ci * GMLP_CHUNK, (ci + 1) * GMLP_CHUNK)
        vc = vn[rows]
        sp = bst_ref[...]
        for g in range(GMLP_GROUPS):
            sp = sp + _dot(ws_ref[g], jnp.where(lane_grp == g, vc, 0.0).astype(BF16))
        ob_ref[0, rows, :] = (u[rows] * sp).astype(BF16)

    qd_ref[0] = ((seg("cq") * cos_d + seg("cqs") * sin_d) * scale_d).astype(BF16)
    kd_ref[0] = (seg("ck") * cos_d + seg("cks") * sin_d).astype(BF16)
    vd_ref[0] = seg("cv").astype(BF16)

    dqkv_ref[0] = seg("dqkv", 3 * MIXER_W)
    dz_ref[0] = seg("dz")
    dba_ref[0] = seg("dba", 128)


def _inproj(layer, x, prev, mod_t, gmix, w_ext, vecs, g64, rope, ws, bst, n_ctx_tiles):
    b, s, d = x.shape
    nt = s // TQ
    ctx_row = b

    def mod_map(l):
        return lambda i, bb: (l, jnp.where(i < n_ctx_tiles, ctx_row, bb), 0, 0)

    tok = lambda i, bb: (bb, i, 0)
    const2 = lambda i, bb: (0, 0)
    const3 = lambda i, bb: (0, 0, 0)
    in_specs = [pl.BlockSpec((1, TQ, d), tok)]
    args = [x]
    if prev is not None:
        in_specs += [pl.BlockSpec((1, TQ, d), tok), pl.BlockSpec((1, 1, 6, d), mod_map(layer - 1))]
        args += [prev, mod_t]
    in_specs += [
        pl.BlockSpec((1, 1, 6, d), mod_map(layer)),
        pl.BlockSpec((1, d), const2),
        pl.BlockSpec((d, W_EXT), const2),
        pl.BlockSpec((8, MIXER_W), const2),
        pl.BlockSpec((MIXER_W, MIXER_W), const2),
        pl.BlockSpec((4, TQ, MIXER_W), lambda i, bb: (0, i, 0)),
        pl.BlockSpec((GMLP_GROUPS, GMLP_CHUNK, GMLP_CHUNK), const3),
        pl.BlockSpec((GMLP_CHUNK, MIXER_W), const2),
    ]
    args += [mod_t, gmix, w_ext, vecs, g64, rope, ws, bst]
    bf = lambda w: jax.ShapeDtypeStruct((b, s, w), BF16)
    ff = lambda w: jax.ShapeDtypeStruct((b, s, w), F32)
    out_shape = [bf(MIXER_W)] * 7 + [ff(3 * MIXER_W), ff(MIXER_W), ff(128)]
    out_specs = [pl.BlockSpec((1, TQ, MIXER_W), tok)] * 7 + [
        pl.BlockSpec((1, TQ, 3 * MIXER_W), tok), pl.BlockSpec((1, TQ, MIXER_W), tok), pl.BlockSpec((1, TQ, 128), tok)]
    if prev is not None:
        out_shape = [ff(d)] + out_shape
        out_specs = [pl.BlockSpec((1, TQ, d), tok)] + out_specs
    outs = pl.pallas_call(
        functools.partial(_inproj_kernel, has_prev=prev is not None,
                          scale_a=HEAD_DIM ** -0.5, scale_d=DIFF_DIM ** -0.5),
        grid=(nt, b), in_specs=in_specs, out_specs=out_specs, out_shape=out_shape,
        compiler_params=_params(("arbitrary", "arbitrary")), name="in_proj",
    )(*args)
    if prev is not None:
        return outs[0], outs[1:]
    return x, outs


def _softmax_pv(qm, k, v):
    s = _dot_nt(qm, k)
    m = jnp.max(s, axis=-1, keepdims=True)
    e = jnp.exp(s - m)
    den = jnp.sum(e, axis=-1, keepdims=True)
    return _dot(e.astype(BF16), v) / den


def _gqa_kernel(q_ref, k_ref, v_ref, o_ref, *, n_ctx_tiles, lc):
    i = pl.program_id(1)
    lane_head = lax.broadcasted_iota(jnp.int32, (TQ, MIXER_W), 1) // HEAD_DIM

    def attend(k, v):
        q = q_ref[0]
        acc = jnp.zeros((TQ, MIXER_W), F32)
        for hd in range(GQA_HEADS):
            mine = lane_head == hd
            o = _softmax_pv(_keep(mine, q), k, v)
            acc = jnp.where(mine, o, acc)
        o_ref[0] = acc.astype(BF16)

    @pl.when(i < n_ctx_tiles)
    def _():
        attend(k_ref[0, :lc, :], v_ref[0, :lc, :])

    @pl.when(i >= n_ctx_tiles)
    def _():
        attend(k_ref[0], v_ref[0])


def _diff_kernel(q_ref, k_ref, v_ref, lam_ref, sub_ref, g64_ref, o_ref, *, n_ctx_tiles, lc, lam_init):
    i = pl.program_id(1)
    lane = lax.broadcasted_iota(jnp.int32, (TQ, MIXER_W), 1)
    lane_head = lane // (2 * DIFF_DIM)
    lane_map = lane // DIFF_DIM
    lp = lam_ref[...]
    lam = (jnp.exp(jnp.sum(lp[0:1] * lp[1:2], axis=-1, keepdims=True))
           - jnp.exp(jnp.sum(lp[2:3] * lp[3:4], axis=-1, keepdims=True)) + lam_init)

    def attend(k, v):
        q = q_ref[0]
        acc = jnp.zeros((TQ, MIXER_W), F32)
        for hd in range(DIFF_HEADS):
            o1 = _softmax_pv(_keep(lane_map == 2 * hd, q), k, v)
            o2 = _softmax_pv(_keep(lane_map == 2 * hd + 1, q), k, v)
            acc = jnp.where(lane_head == hd, o1 - lam * o2, acc)
        r = lax.rsqrt(_group_sumsq(acc, g64_ref[...]) * (1.0 / (2 * DIFF_DIM)) + NORM_EPS)
        o_ref[0] = (acc * r * sub_ref[...] * (1.0 - lam_init)).astype(BF16)

    @pl.when(i < n_ctx_tiles)
    def _():
        attend(k_ref[0, :lc, :], v_ref[0, :lc, :])

    @pl.when(i >= n_ctx_tiles)
    def _():
        attend(k_ref[0], v_ref[0])


def _attention(kind, q, k, v, n_ctx_tiles, lc, extra=(), lam_init=0.0):
    b, s, w = q.shape
    nt = s // TQ
    tok = lambda bb, i: (bb, i, 0)
    row = lambda bb, i: (bb, 0, 0)
    in_specs = [pl.BlockSpec((1, TQ, w), tok), pl.BlockSpec((1, s, w), row), pl.BlockSpec((1, s, w), row)]
    if kind == "gqa":
        body = functools.partial(_gqa_kernel, n_ctx_tiles=n_ctx_tiles, lc=lc)
    else:
        body = functools.partial(_diff_kernel, n_ctx_tiles=n_ctx_tiles, lc=lc, lam_init=lam_init)
        in_specs += [pl.BlockSpec(e.shape, lambda bb, i: (0, 0)) for e in extra]
    return pl.pallas_call(
        body, grid=(b, nt), in_specs=in_specs, out_specs=pl.BlockSpec((1, TQ, w), tok),
        out_shape=jax.ShapeDtypeStruct((b, s, w), BF16),
        compiler_params=_params(("arbitrary", "arbitrary")), name=kind + "_attention",
    )(q, k, v, *extra)


def _gdn_prep_kernel(x_ref, xp_ref, xn_ref, ba_ref, cw_ref, av_ref, g64_ref,
                     q_ref, k_ref, v_ref, kt_ref, bg_ref, bgt_ref, *, n_ctx_tiles, n_tiles):
    i = pl.program_id(0)
    x = x_ref[0]
    has_prev = jnp.where((i != 0) & (i != n_ctx_tiles), 1.0, 0.0)
    has_next = jnp.where((i != n_ctx_tiles - 1) & (i != n_tiles - 1), 1.0, 0.0)
    row = lax.broadcasted_iota(jnp.int32, x.shape, 0)
    x_m1 = jnp.where(row == 0, xp_ref[0, 7:8, :] * has_prev, pltpu.roll(x, 1, 0))
    x_p1 = jnp.where(row == TQ - 1, xn_ref[0, 0:1, :] * has_next, pltpu.roll(x, TQ - 1, 0))
    y = _silu(cw_ref[0:1, :] * x_m1 + cw_ref[1:2, :] * x + cw_ref[2:3, :] * x_p1)
    q, k, v = y[:, :MIXER_W], y[:, MIXER_W:2 * MIXER_W], y[:, 2 * MIXER_W:]
    g64 = g64_ref[...]
    qn = q * lax.rsqrt(_group_sumsq(q, g64) + NORM_EPS) * (DN_DK ** -0.5)
    kn = k * lax.rsqrt(_group_sumsq(k, g64) + NORM_EPS)
    q_ref[0] = qn.astype(BF16)
    k_ref[0] = kn.astype(BF16)
    v_ref[0] = v.astype(BF16)
    kt_ref[0] = kn.T.astype(BF16)
    ba = ba_ref[0]
    lane = lax.broadcasted_iota(jnp.int32, ba.shape, 1)
    beta = jax.nn.sigmoid(ba)
    g = -jnp.exp(av_ref[0:1, :]) * _softplus(ba + av_ref[1:2, :])
    bg = jnp.where(lane < 2 * DN_HEADS, beta, jnp.where(lane < 4 * DN_HEADS, g, 0.0))
    bg_ref[0] = bg
    bgt_ref[0] = bg.T


def _gdn_prep(dqkv, dba, conv_w, avec, g64, n_ctx_tiles):
    b, s, w3 = dqkv.shape
    nt = s // TQ
    nb8 = s // 8
    tok = lambda i, bb: (bb, i, 0)
    tokt = lambda i, bb: (bb, 0, i)
    const2 = lambda i, bb: (0, 0)
    outs = pl.pallas_call(
        functools.partial(_gdn_prep_kernel, n_ctx_tiles=n_ctx_tiles, n_tiles=nt),
        grid=(nt, b),
        in_specs=[
            pl.BlockSpec((1, TQ, w3), tok),
            pl.BlockSpec((1, 8, w3), lambda i, bb: (bb, jnp.maximum(i * (TQ // 8) - 1, 0), 0)),
            pl.BlockSpec((1, 8, w3), lambda i, bb: (bb, jnp.minimum((i + 1) * (TQ // 8), nb8 - 1), 0)),
            pl.BlockSpec((1, TQ, 128), tok),
            pl.BlockSpec((3, w3), const2),
            pl.BlockSpec((2, 128), const2),
            pl.BlockSpec((MIXER_W, MIXER_W), const2),
        ],
        out_specs=[pl.BlockSpec((1, TQ, MIXER_W), tok)] * 3 + [
            pl.BlockSpec((1, MIXER_W, TQ), tokt), pl.BlockSpec((1, TQ, 128), tok), pl.BlockSpec((1, 128, TQ), tokt)],
        out_shape=[jax.ShapeDtypeStruct((b, s, MIXER_W), BF16)] * 3 + [
            jax.ShapeDtypeStruct((b, MIXER_W, s), BF16), jax.ShapeDtypeStruct((b, s, 128), F32),
            jax.ShapeDtypeStruct((b, 128, s), F32)],
        compiler_params=_params(("arbitrary", "arbitrary")), name="gdn_prep",
    )(dqkv, dqkv, dqkv, dba, conv_w, avec, g64)
    return outs


def _gdn_scan_kernel(qf, kf, vf, ktf, bgf, bgtf, qb, kb, vb, ktb, bgb, bgtb, of_ref, ob_ref, st_ref):
    @pl.when(pl.program_id(1) == 0)
    def _():
        st_ref[...] = jnp.zeros_like(st_ref)

    pp, cc, w = GDN_PAIR, DN_CHUNK, MIXER_W
    dirs = (0, 1)
    heads = range(DN_HEADS)
    chains = [(d, hd) for d in dirs for hd in heads]
    q_refs, k_refs, v_refs, kt_refs = (qf, qb), (kf, kb), (vf, vb), (ktf, ktb)
    bg_refs, bgt_refs, o_refs = (bgf, bgb), (bgtf, bgtb), (of_ref, ob_ref)

    ii = lax.broadcasted_iota(jnp.int32, (pp, pp), 0)
    jj = lax.broadcasted_iota(jnp.int32, (pp, pp), 1)
    same = (ii // cc) == (jj // cc)
    incl = (same & (jj <= ii), same & (jj >= ii))
    strict = (same & (jj < ii), same & (jj > ii))
    incl_b = [_onehot(m) for m in incl]
    incl_tb = [_onehot(same & (ii <= jj)), _onehot(same & (ii >= jj))]
    eye = jnp.where(ii == jj, 1.0, 0.0)
    merge_masks = [((ii // (2 * sz)) == (jj // (2 * sz))) & ((ii // sz) != (jj // sz))
                   for sz in (2 ** e for e in range(int(math.log2(cc))))]

    src = lax.broadcasted_iota(jnp.int32, (128, w), 0)
    lane_w = lax.broadcasted_iota(jnp.int32, (128, w), 1)
    src2 = lax.broadcasted_iota(jnp.int32, (128, DN_HEADS * pp), 0)
    lane2 = lax.broadcasted_iota(jnp.int32, (128, DN_HEADS * pp), 1)
    bg = [r[0] for r in bg_refs]
    beta_x = [_dot_sel_r(bg[d], _onehot(src == DN_HEADS * d + lane_w // DN_DV)) for d in dirs]
    g_x = [_dot_sel_r(bg[d], _onehot(src == 2 * DN_HEADS + DN_HEADS * d + lane_w // DN_DV)) for d in dirs]
    g_x2 = [_dot_sel_r(bg[d], _onehot(src2 == 2 * DN_HEADS + DN_HEADS * d + lane2 // pp)) for d in dirs]
    cg_rows = [_dot_sel_r(bgt_refs[d][0], incl_tb[d]) for d in dirs]
    cg_x = [_dot_sel_l(incl_b[d], g_x[d]) for d in dirs]
    cg_x2 = [_dot_sel_l(incl_b[d], g_x2[d]) for d in dirs]

    q = [r[0] for r in q_refs]
    k = [r[0] for r in k_refs]
    kf32 = [t.astype(F32) for t in k]
    e_cg = [jnp.exp(t) for t in cg_x]
    rhs_v = [v_refs[d][0].astype(F32) * beta_x[d] for d in dirs]
    rhs_k = [kf32[d] * (beta_x[d] * e_cg[d]) for d in dirs]
    k_beta = [kf32[d] * beta_x[d] for d in dirs]
    lane_head = lax.broadcasted_iota(jnp.int32, (pp, w), 1) // DN_DV

    kk = [_dot_nt(jnp.where(lane_head == hd, k_beta[d], 0.0).astype(BF16), k[d]) for d, hd in chains]
    qk = [_dot_nt(_keep(lane_head == hd, q[d]), k[d]) for d, hd in chains]
    a, qkd = [], []
    for ci, (d, hd) in enumerate(chains):
        gl = 2 * DN_HEADS + DN_HEADS * d + hd
        diff = jnp.where(incl[d], cg_x2[d][:, hd * pp:(hd + 1) * pp] - cg_rows[d][gl:gl + 1, :], 0.0)
        decay = jnp.where(incl[d], jnp.exp(diff), 0.0)
        a.append(jnp.where(strict[d], kk[ci] * decay, 0.0))
        qkd.append((qk[ci] * decay).astype(BF16))

    t_inv = [eye - jnp.where(merge_masks[0], a_c, 0.0) for a_c in a]
    for mask in merge_masks[1:]:
        tb = [t.astype(BF16) for t in t_inv]
        lm = [_dot(jnp.where(mask, a_c, 0.0).astype(BF16), tb_c).astype(BF16) for a_c, tb_c in zip(a, tb)]
        t_inv = [t - _dot(tb_c, lm_c) for t, tb_c, lm_c in zip(t_inv, tb, lm)]
    tb = [t.astype(BF16) for t in t_inv]
    u_part = [_dot(tb[ci], jnp.where(lane_head == hd, rhs_v[d], 0.0).astype(BF16)) for ci, (d, hd) in enumerate(chains)]
    w_part = [_dot(tb[ci], jnp.where(lane_head == hd, rhs_k[d], 0.0).astype(BF16)) for ci, (d, hd) in enumerate(chains)]
    u_all = [sum(u_part[d * DN_HEADS + hd] for hd in heads) for d in dirs]
    w_all = [sum(w_part[d * DN_HEADS + hd] for hd in heads) for d in dirs]

    st = [st_ref[d] for d in dirs]
    blk = (lax.broadcasted_iota(jnp.int32, (w, w), 0) // DN_DK) == (lax.broadcasted_iota(jnp.int32, (w, w), 1) // DN_DV)
    kt = [r[0] for r in kt_refs]
    order = (((0, cc), (cc, pp)), ((cc, pp), (0, cc)))
    zeros_c = jnp.zeros((cc, w), F32)
    lane_head_c = lax.broadcasted_iota(jnp.int32, (cc, w), 1) // DN_DV

    def place(lo, t):
        return jnp.concatenate([t, zeros_c] if lo == 0 else [zeros_c, t], axis=0)

    nv_acc = [None, None]
    for step in range(2):
        lo = [order[d][step][0] for d in dirs]
        rows = [slice(*order[d][step]) for d in dirs]
        stb = [t.astype(BF16) for t in st]
        w_s = [_dot(w_all[d][rows[d]].astype(BF16), stb[d]) for d in dirs]
        q_s = [_dot(q[d][rows[d]], stb[d]) for d in dirs]
        nv = [u_all[d][rows[d]] - w_s[d] for d in dirs]
        for d in dirs:
            full = place(lo[d], nv[d])
            nv_acc[d] = full if nv_acc[d] is None else nv_acc[d] + full
        nvb = [t.astype(BF16) for t in nv_acc]
        intra = [_dot(qkd[ci][rows[d]], nvb[d]) for ci, (d, hd) in enumerate(chains)]
        last = [order[0][step][1] - 1, order[1][step][0]]
        g_end = [cg_x[d][last[d]:last[d] + 1, :] for d in dirs]
        nvs = [place(lo[d], nv[d] * jnp.exp(g_end[d] - cg_x[d][rows[d]])).astype(BF16) for d in dirs]
        upd = [_dot(kt[d], nvs[d]) for d in dirs]
        for d in dirs:
            o = e_cg[d][rows[d]] * q_s[d]
            for hd in heads:
                o = o + jnp.where(lane_head_c == hd, intra[d * DN_HEADS + hd], 0.0)
            o_refs[d][0, rows[d], :] = o
            st[d] = st[d] * jnp.exp(g_end[d]) + jnp.where(blk, upd[d], 0.0)
    for d in dirs:
        st_ref[d] = st[d]


def _gdn_scan(q, k, v, kt, bg, bgt, lc):
    b, s, w = q.shape
    n_pairs = s // GDN_PAIR
    ncp = lc // GDN_PAIR

    def fwd(bb, i):
        return i

    def bwd(bb, i):
        return jnp.where(i < ncp, ncp - 1 - i, n_pairs - 1 + ncp - i)

    def specs(pos):
        tok = lambda bb, i: (bb, pos(bb, i), 0)
        tokt = lambda bb, i: (bb, 0, pos(bb, i))
        return [pl.BlockSpec((1, GDN_PAIR, w), tok)] * 3 + [
            pl.BlockSpec((1, w, GDN_PAIR), tokt), pl.BlockSpec((1, GDN_PAIR, 128), tok),
            pl.BlockSpec((1, 128, GDN_PAIR), tokt)]

    return pl.pallas_call(
        _gdn_scan_kernel, grid=(b, n_pairs),
        in_specs=specs(fwd) + specs(bwd),
        out_specs=[pl.BlockSpec((1, GDN_PAIR, w), lambda bb, i: (bb, fwd(bb, i), 0)),
                   pl.BlockSpec((1, GDN_PAIR, w), lambda bb, i: (bb, bwd(bb, i), 0))],
        out_shape=[jax.ShapeDtypeStruct((b, s, w), F32)] * 2,
        scratch_shapes=[pltpu.VMEM((2, w, w), F32)],
        compiler_params=_params(("arbitrary", "arbitrary")), name="gdn_scan",
    )(q, k, v, kt, bg, bgt, q, k, v, kt, bg, bgt)


def _outproj_kernel(x_ref, oa_ref, ob_ref, oc_ref, of_ref, obw_ref, dz_ref, mod_ref, w_ref, gout_ref,
                    g64_ref, gffn_ref, wr_ref, br_ref, xo_ref, f_ref, lg_ref):
    o = of_ref[0] + obw_ref[0]
    r = lax.rsqrt(_group_sumsq(o, g64_ref[...]) * (1.0 / DN_DV) + NORM_EPS)
    od = (o * r * gout_ref[...] * _silu(dz_ref[0])).astype(BF16)
    y = (_dot(oa_ref[0], w_ref[0]) + _dot(ob_ref[0], w_ref[1]) + _dot(oc_ref[0], w_ref[2]) + _dot(od, w_ref[3]))
    x = x_ref[0] + mod_ref[0, 0, 2:3, :] * y
    xo_ref[0] = x
    ms = jnp.mean(x * x, axis=-1, keepdims=True)
    f = (x * lax.rsqrt(ms + NORM_EPS) * gffn_ref[...]) * (1.0 + mod_ref[0, 0, 4:5, :]) + mod_ref[0, 0, 3:4, :]
    f_ref[...] = f
    lg_ref[...] = lax.dot_general(wr_ref[...], f, (((1,), (1,)), ((), ())), precision=HIGHEST,
                                  preferred_element_type=F32) + br_ref[:, 0:1]


def _outproj(layer, x, oa, ob, oc, o_f, o_b, dz, mod_t, w_out4, gout, g64, gffn, wr_t, br, n_ctx_tiles):
    b, s, d = x.shape
    nt = s // TQ
    ctx_row = b
    tok = lambda i, bb: (bb, i, 0)
    const2 = lambda i, bb: (0, 0)
    slab = pl.BlockSpec((1, TQ, MIXER_W), tok)
    return pl.pallas_call(
        _outproj_kernel, grid=(nt, b),
        in_specs=[pl.BlockSpec((1, TQ, d), tok), slab, slab, slab, slab, slab, slab,
                  pl.BlockSpec((1, 1, 6, d), lambda i, bb: (layer, jnp.where(i < n_ctx_tiles, ctx_row, bb), 0, 0)),
                  pl.BlockSpec((4, MIXER_W, d), lambda i, bb: (0, 0, 0)),
                  pl.BlockSpec((1, MIXER_W), const2), pl.BlockSpec((MIXER_W, MIXER_W), const2),
                  pl.BlockSpec((1, d), const2), pl.BlockSpec((N_EXPERTS, d), const2),
                  pl.BlockSpec((N_EXPERTS, 128), const2)],
        out_specs=[pl.BlockSpec((1, TQ, d), tok), pl.BlockSpec((TQ, d), lambda i, bb: (bb * nt + i, 0)),
                   pl.BlockSpec((N_EXPERTS, TQ), lambda i, bb: (0, bb * nt + i))],
        out_shape=[jax.ShapeDtypeStruct((b, s, d), F32), jax.ShapeDtypeStruct((b * s, d), F32),
                   jax.ShapeDtypeStruct((N_EXPERTS, b * s), F32)],
        compiler_params=_params(("arbitrary", "arbitrary")), name="out_proj",
    )(x, oa, ob, oc, o_f, o_b, dz, mod_t, w_out4, gout, g64, gffn, wr_t, br)


def _route_kernel(lg_ref, tri_ref, pos_ref, gate_ref, cnt_ref):
    x = lg_ref[...]
    e_iota = lax.broadcasted_iota(jnp.int32, x.shape, 0).astype(F32)
    work = x
    chosen = jnp.zeros(x.shape, F32)
    top = None
    den = None
    for kk in range(TOP_K):
        m = jnp.max(work, axis=0, keepdims=True)
        idx = jnp.min(jnp.where(work == m, e_iota, float(N_EXPERTS)), axis=0, keepdims=True)
        pick = e_iota == idx
        chosen = jnp.where(pick, 1.0, chosen)
        if kk == 0:
            top = m
            den = jnp.ones_like(m)
        else:
            den = den + jnp.exp(m - top)
        work = jnp.where(pick, -jnp.inf, work)
    sel = chosen > 0.5
    gate_ref[0] = jnp.where(sel, jnp.exp(x - top) / den, 0.0)
    rank = _dot(chosen.astype(BF16), tri_ref[...])
    pos_ref[0] = jnp.where(sel, rank.astype(jnp.int32), -1)
    cnt = jnp.sum(chosen, axis=1, keepdims=True).astype(jnp.int32)
    cnt_ref[0] = jnp.broadcast_to(cnt, cnt_ref.shape[1:])


def _route(logits_t, tri):
    n_exp, t = logits_t.shape
    n_tiles = t // MOE_TM
    return pl.pallas_call(
        _route_kernel, grid=(n_tiles,),
        in_specs=[pl.BlockSpec((n_exp, MOE_TM), lambda i: (0, i)), pl.BlockSpec((MOE_TM, MOE_TM), lambda i: (0, 0))],
        out_specs=[pl.BlockSpec((1, n_exp, MOE_TM), lambda i: (i, 0, 0)),
                   pl.BlockSpec((1, n_exp, MOE_TM), lambda i: (i, 0, 0)),
                   pl.BlockSpec((1, n_exp, 128), lambda i: (i, 0, 0))],
        out_shape=[jax.ShapeDtypeStruct((n_tiles, n_exp, MOE_TM), jnp.int32),
                   jax.ShapeDtypeStruct((n_tiles, n_exp, MOE_TM), F32),
                   jax.ShapeDtypeStruct((n_tiles, n_exp, 128), jnp.int32)],
        compiler_params=_params(("arbitrary",)), name="route",
    )(logits_t, tri)


def _slots_kernel(pos_ref, gate_ref, base_ref, tril_ref, dest_ref, gk_ref):
    pos = pos_ref[0]
    chosen = pos >= 0
    slot = (base_ref[0][:, 0:1] + pos).astype(F32)
    choice = _dot(tril_ref[...], _onehot(chosen))
    gate = gate_ref[0]
    pad = jnp.zeros((8 - TOP_K, pos.shape[1]), F32)
    d_rows, g_rows = [], []
    for kk in range(TOP_K):
        mine = chosen & (choice == float(kk))
        d_rows.append(jnp.sum(jnp.where(mine, slot, 0.0), axis=0, keepdims=True))
        g_rows.append(jnp.sum(jnp.where(mine, gate, 0.0), axis=0, keepdims=True))
    dest_ref[0] = jnp.concatenate(d_rows + [pad], axis=0).astype(jnp.int32)
    gk_ref[0] = jnp.concatenate(g_rows + [pad], axis=0)


def _slots(pos_t, gate_t, base_b, tril):
    n_tiles, n_exp, tm = pos_t.shape
    tile = lambda i: (i, 0, 0)
    return pl.pallas_call(
        _slots_kernel, grid=(n_tiles,),
        in_specs=[pl.BlockSpec((1, n_exp, tm), tile), pl.BlockSpec((1, n_exp, tm), tile),
                  pl.BlockSpec((1, n_exp, 128), tile), pl.BlockSpec((n_exp, n_exp), lambda i: (0, 0))],
        out_specs=[pl.BlockSpec((1, 8, tm), tile), pl.BlockSpec((1, 8, tm), tile)],
        out_shape=[jax.ShapeDtypeStruct((n_tiles, 8, tm), jnp.int32), jax.ShapeDtypeStruct((n_tiles, 8, tm), F32)],
        compiler_params=_params(("arbitrary",)), name="moe_slots",
    )(pos_t, gate_t, base_b, tril)


def _dispatch_kernel(pad_ref, f_ref, dest_ref, xs_ref, zbuf, zsem, sem):
    tm = f_ref.shape[0]
    n_blocks = xs_ref.shape[0] // MOE_BM

    @pl.when(pl.program_id(0) == 0)
    def _():
        zbuf[...] = jnp.zeros_like(zbuf)
        live = pad_ref[2 * N_EXPERTS]

        def zero_block(row):
            return pltpu.make_async_copy(zbuf, xs_ref.at[pl.ds(pl.multiple_of(row, MOE_BM), MOE_BM)], zsem)

        for wait in (False, True):
            for e in range(N_EXPERTS):
                for cond, row in ((pad_ref[N_EXPERTS + e] > 0, pad_ref[e]), (live + e < n_blocks, (live + e) * MOE_BM)):
                    @pl.when(cond)
                    def _():
                        zero_block(row).wait() if wait else zero_block(row).start()

    def rows(tok, carry):
        for kk in range(TOP_K):
            pltpu.make_async_copy(f_ref.at[pl.ds(tok, 1)], xs_ref.at[pl.ds(dest_ref[0, kk, tok], 1)],
                                  sem).start(priority=kk % 2)
        return carry

    lax.fori_loop(0, tm, rows, 0, unroll=8)
    for kk in range(TOP_K):
        pltpu.make_async_copy(f_ref, xs_ref.at[pl.ds(0, tm)], sem).wait()


def _dispatch(pad_info, f, dest, n_slots):
    t, d = f.shape
    n_tiles, _, tm = dest.shape
    grid_spec = pltpu.PrefetchScalarGridSpec(
        num_scalar_prefetch=1, grid=(n_tiles,),
        in_specs=[pl.BlockSpec((tm, d), lambda i, p: (i, 0)),
                  pl.BlockSpec((1, 8, tm), lambda i, p: (i, 0, 0), memory_space=pltpu.SMEM)],
        out_specs=pl.BlockSpec(memory_space=pl.ANY),
        scratch_shapes=[pltpu.VMEM((MOE_BM, d), F32), pltpu.SemaphoreType.DMA(()), pltpu.SemaphoreType.DMA(())],
    )
    return pl.pallas_call(
        _dispatch_kernel, grid_spec=grid_spec, out_shape=jax.ShapeDtypeStruct((n_slots, d), F32),
        compiler_params=_params(("arbitrary",)), name="moe_dispatch",
    )(pad_info, f, dest)


def _experts_kernel(meta_ref, xs_ref, wu_ref, bu_ref, wd_ref, bd_ref, ys_ref, wu_bf, wd_bf):
    i = pl.program_id(0)
    nb = pl.num_programs(0)
    live = i < meta_ref[2 * nb]
    de = wd_bf.shape[0]

    @pl.when(live & (meta_ref[nb + i] > 0))
    def _():
        wu_bf[...] = wu_ref[0, 0].astype(BF16)
        wd_bf[...] = wd_ref[0, 0].astype(BF16)

    @pl.when(live)
    def _():
        hgu = _dot(xs_ref[...].astype(BF16), wu_bf[...]) + bu_ref[0, 0]
        gate = jnp.minimum(hgu[:, :de], SWIGLU_LIMIT)
        up = jnp.clip(hgu[:, de:], -SWIGLU_LIMIT, SWIGLU_LIMIT)
        hid = gate * jax.nn.sigmoid(SWIGLU_ALPHA * gate) * (up + 1.0)
        ys_ref[...] = _dot(hid.astype(BF16), wd_bf[...]) + bd_ref[0, 0]

    @pl.when(jnp.logical_not(live))
    def _():
        ys_ref[...] = jnp.zeros_like(ys_ref)


def _experts(layer, meta, xs, w_up, b_up, w_down, b_down):
    n_slots, d = xs.shape
    nb = n_slots // MOE_BM
    depth, n_exp, _, de2 = w_up.shape
    de = de2 // 2
    blk = lambda i, m: (jnp.minimum(i, m[2 * nb] - 1), 0)
    exp = lambda i, m: (layer, m[i], 0, 0)
    grid_spec = pltpu.PrefetchScalarGridSpec(
        num_scalar_prefetch=1, grid=(nb,),
        in_specs=[pl.BlockSpec((MOE_BM, d), blk),
                  pl.BlockSpec((1, 1, d, de2), exp), pl.BlockSpec((1, 1, 1, de2), exp),
                  pl.BlockSpec((1, 1, de, d), exp), pl.BlockSpec((1, 1, 1, d), exp)],
        out_specs=pl.BlockSpec((MOE_BM, d), lambda i, m: (i, 0)),
        scratch_shapes=[pltpu.VMEM((d, de2), BF16), pltpu.VMEM((de, d), BF16)],
    )
    return pl.pallas_call(
        _experts_kernel, grid_spec=grid_spec, out_shape=jax.ShapeDtypeStruct((n_slots, d), F32),
        compiler_params=_params(("arbitrary",)), name="moe_experts",
    )(meta, xs, w_up, b_up.reshape(depth, n_exp, 1, de2), w_down, b_down.reshape(depth, n_exp, 1, d))


def _combine_kernel(dest_ref, gk_ref, ys_ref, o_ref, ybuf, sem):
    tc, d = o_ref.shape

    def rows(tok, carry):
        for kk in range(TOP_K):
            pltpu.make_async_copy(ys_ref.at[pl.ds(dest_ref[0, kk, tok], 1)], ybuf.at[kk, pl.ds(tok, 1)],
                                  sem).start(priority=kk % 2)
        return carry

    lax.fori_loop(0, tc, rows, 0, unroll=8)
    for kk in range(TOP_K):
        pltpu.make_async_copy(ys_ref.at[pl.ds(0, tc)], ybuf.at[kk], sem).wait()

    hi, mid, lo = _split3(gk_ref[0])
    pick = lax.broadcasted_iota(jnp.int32, (8, 128), 0)
    cols = []
    for kk in range(TOP_K):
        sel = _onehot(pick == kk)
        cols.append(_dot_tn(hi, sel) + _dot_tn(mid, sel) + _dot_tn(lo, sel))
    for j in range(d // 128):
        lanes = slice(j * 128, (j + 1) * 128)
        acc = cols[0] * ybuf[0, :, lanes]
        for kk in range(1, TOP_K):
            acc = acc + cols[kk] * ybuf[kk, :, lanes]
        o_ref[:, lanes] = acc


def _combine(dest, gk, ys):
    n_tiles, _, tm = dest.shape
    d = ys.shape[1]
    per = tm // MOE_TC
    blk = lambda i: (i // per, 0, i % per)
    return pl.pallas_call(
        _combine_kernel, grid=(n_tiles * per,),
        in_specs=[pl.BlockSpec((1, 8, MOE_TC), blk, memory_space=pltpu.SMEM), pl.BlockSpec((1, 8, MOE_TC), blk),
                  pl.BlockSpec(memory_space=pl.ANY)],
        out_specs=pl.BlockSpec((MOE_TC, d), lambda i: (i, 0)),
        out_shape=jax.ShapeDtypeStruct((n_tiles * tm, d), F32),
        scratch_shapes=[pltpu.VMEM((TOP_K, MOE_TC, d), F32), pltpu.SemaphoreType.DMA(())],
        compiler_params=_params(("arbitrary",)), name="moe_combine",
    )(dest, gk, ys)


def _moe(layer, f, pos_t, gate_t, cnt, tril, w_up, b_up, w_down, b_down):
    t = f.shape[0]
    nb = -(-(t * TOP_K + N_EXPERTS * (MOE_BM - 1)) // MOE_BM)
    per_expert = jnp.sum(cnt, axis=0)
    blocks_e = (per_expert + MOE_BM - 1) // MOE_BM
    blk_end = jnp.cumsum(blocks_e)
    blk_start = blk_end - blocks_e
    base = (blk_start * MOE_BM)[None, :] + jnp.cumsum(cnt, axis=0) - cnt
    live = blk_end[-1]
    blk_id = jnp.minimum(jnp.arange(nb, dtype=jnp.int32), live - 1)
    blk_expert = jnp.sum((blk_end[None, :] <= blk_id[:, None]).astype(jnp.int32), axis=1)
    first = (jnp.arange(nb, dtype=jnp.int32) == blk_start[blk_expert]).astype(jnp.int32)
    meta = jnp.concatenate([blk_expert, first, live[None]]).astype(jnp.int32)
    pad_info = jnp.concatenate([jnp.maximum(blk_end - 1, 0) * MOE_BM, (blocks_e > 0).astype(jnp.int32),
                                live[None]]).astype(jnp.int32)
    base_b = jnp.broadcast_to(base[:, :, None], base.shape + (128,)).astype(jnp.int32)

    dest, gk = _slots(pos_t, gate_t, base_b, tril)
    xs = _dispatch(pad_info, f, dest, nb * MOE_BM)
    ys = _experts(layer, meta, xs, w_up, b_up, w_down, b_down)
    return _combine(dest, gk, ys)


def _final_kernel(x_ref, ffn_ref, mod_ref, g_ref, o_ref):
    x = x_ref[0] + mod_ref[0, 0, 5:6, :] * ffn_ref[0]
    ms = jnp.mean(x * x, axis=-1, keepdims=True)
    o_ref[0] = x * lax.rsqrt(ms + NORM_EPS) * g_ref[...]


def _final(layer, x, ffn, mod_t, gain, n_ctx_tiles, l):
    b, s, d = x.shape
    tok = lambda bb, i: (bb, i + n_ctx_tiles, 0)
    return pl.pallas_call(
        _final_kernel, grid=(b, l // TQ),
        in_specs=[pl.BlockSpec((1, TQ, d), tok), pl.BlockSpec((1, TQ, d), tok),
                  pl.BlockSpec((1, 1, 6, d), lambda bb, i: (layer, bb, 0, 0)),
                  pl.BlockSpec((1, d), lambda bb, i: (0, 0))],
        out_specs=pl.BlockSpec((1, TQ, d), lambda bb, i: (bb, i, 0)),
        out_shape=jax.ShapeDtypeStruct((b, l, d), F32),
        compiler_params=_params(("arbitrary", "arbitrary")), name="final_norm",
    )(x, ffn, mod_t, gain)


def _swap_halves(n_groups, width):
    base = np.arange(n_groups * width).reshape(n_groups, width)
    return np.concatenate([base[:, width // 2:], base[:, :width // 2]], axis=1).reshape(-1)


def _in_columns():
    sizes = (GQA_HEADS * HEAD_DIM, GQA_KV_HEADS * HEAD_DIM, GQA_KV_HEADS * HEAD_DIM, MIXER_W, MIXER_W,
             MIXER_W, MIXER_W, MIXER_W, 3 * MIXER_W, MIXER_W, 2 * DN_HEADS, 2 * DN_HEADS)
    starts = np.concatenate([[0], np.cumsum(sizes)[:-1]])
    aq, ak, av, bu, bv, cq, ck, cv, dqkv, dz, db, da = (np.arange(n) + o for n, o in zip(sizes, starts))
    grp = GQA_HEADS // GQA_KV_HEADS
    expand = np.concatenate([np.arange(HEAD_DIM) + (hd // grp) * HEAD_DIM for hd in range(GQA_HEADS)])
    sw64 = _swap_halves(GQA_HEADS, HEAD_DIM)
    sw32 = _swap_halves(2 * DIFF_HEADS, DIFF_DIM)
    ak_x, av_x = ak[expand], av[expand]
    cols = [aq, aq[sw64], ak_x, ak_x[sw64], av_x, bu, bv, cq, cq[sw32], ck, ck[sw32], cv, dqkv, dz,
            db, da, np.full(128 - 4 * DN_HEADS, -1)]
    return np.concatenate(cols), expand, sw64


def _take_cols(w, cols):
    safe = np.where(cols < 0, 0, cols)
    return jnp.where(jnp.asarray(cols >= 0)[None, :], w[:, safe], 0.0)


def _rope_tables(l, lc):
    rows = l // GRID_W
    r_idx, c_idx = np.meshgrid(np.arange(rows), np.arange(GRID_W), indexing="ij")
    row_pos = jnp.asarray(r_idx.reshape(-1), F32)
    col_pos = jnp.asarray(c_idx.reshape(-1), F32)

    def table(dim, reps):
        n = dim // 4
        inv = jnp.power(ROPE_THETA, -jnp.arange(n, dtype=F32) / n)
        ang = jnp.concatenate([row_pos[:, None] * inv, col_pos[:, None] * inv], axis=-1)
        cos, sin = jnp.cos(ang), jnp.sin(ang)
        cos_t = jnp.tile(jnp.concatenate([cos, cos], axis=-1), (1, reps))
        sin_t = jnp.tile(jnp.concatenate([-sin, sin], axis=-1), (1, reps))
        return (jnp.concatenate([jnp.ones((lc, cos_t.shape[1]), F32), cos_t], axis=0),
                jnp.concatenate([jnp.zeros((lc, sin_t.shape[1]), F32), sin_t], axis=0))

    cos_a, sin_a = table(HEAD_DIM, GQA_HEADS)
    cos_d, sin_d = table(DIFF_DIM, 2 * DIFF_HEADS)
    return jnp.stack([cos_a, sin_a, cos_d, sin_d])


def kernel(x, c, ctx, c_ctx, w_ada, b_ada, norm_mix, norm_ffn, w_in, w_out, gqa_q_norm, gqa_k_norm, gmlp_v_norm, gmlp_w_s, gmlp_b_s, diff_lambda_q1, diff_lambda_k1, diff_lambda_q2, diff_lambda_k2, diff_subln, dn_conv_w, dn_a_log, dn_dt_bias, dn_out_norm, router_w, router_b, exp_w_up, exp_b_up, exp_w_down, exp_b_down, final_norm):
    b, l, d = x.shape
    lc = ctx.shape[1]
    depth = w_ada.shape[0]
    s = lc + l
    assert lc % TQ == 0 and l % TQ == 0 and (b * s) % MOE_TM == 0 and l % GRID_W == 0
    n_ctx_tiles = lc // TQ

    rows = -(-(b + 1) // 8) * 8
    c_all = jnp.zeros((rows, d), F32).at[:b].set(c).at[b].set(c_ctx)
    mod_t = _modulation(c_all, w_ada, b_ada).transpose(0, 2, 1, 3)

    cols, expand, sw64 = _in_columns()
    rope = _rope_tables(l, lc)
    lane = np.arange(MIXER_W)
    g64 = jnp.asarray((lane[:, None] // 64) == (lane[None, :] // 64), BF16)
    tri = jnp.asarray(np.arange(MOE_TM)[:, None] < np.arange(MOE_TM)[None, :], BF16)
    tril = jnp.asarray(np.arange(N_EXPERTS)[:, None] > np.arange(N_EXPERTS)[None, :], BF16)

    xs = jnp.concatenate([ctx, x], axis=1)
    ffn = None
    for layer in range(depth):
        lam_init = 0.8 - 0.6 * math.exp(-0.3 * layer)
        w_ext = _take_cols(w_in[layer], cols).astype(BF16)
        gq = jnp.tile(gqa_q_norm[layer], GQA_HEADS)
        gk = jnp.tile(gqa_k_norm[layer], GQA_HEADS)
        vecs = jnp.zeros((8, MIXER_W), F32).at[0].set(gq).at[1].set(gq[sw64]).at[2].set(gk).at[3].set(gk[sw64])
        vecs = vecs.at[4].set(gmlp_v_norm[layer])
        bst = jnp.repeat(gmlp_b_s[layer].T, GMLP_CH, axis=1)
        xs, (qa, ka, va, out_b, qd, kd, vd, dqkv, dz, dba) = _inproj(
            layer, xs, ffn, mod_t, norm_mix[layer][None, :], w_ext, vecs, g64, rope,
            gmlp_w_s[layer].astype(BF16), bst, n_ctx_tiles)

        out_a = _attention("gqa", qa, ka, va, n_ctx_tiles, lc)
        lam_p = jnp.stack([diff_lambda_q1[layer], diff_lambda_k1[layer], diff_lambda_q2[layer], diff_lambda_k2[layer]])
        sub = jnp.tile(diff_subln[layer], DIFF_HEADS)[None, :]
        out_c = _attention("diff", qd, kd, vd, n_ctx_tiles, lc, extra=(lam_p, sub, g64), lam_init=lam_init)

        avec = jnp.zeros((2, 128), F32)
        avec = avec.at[0, 2 * DN_HEADS:4 * DN_HEADS].set(dn_a_log[layer].reshape(-1))
        avec = avec.at[1, 2 * DN_HEADS:4 * DN_HEADS].set(dn_dt_bias[layer].reshape(-1))
        gq_, gk_, gv_, gkt, gbg, gbgt = _gdn_prep(dqkv, dba, dn_conv_w[layer], avec, g64, n_ctx_tiles)
        o_f, o_b = _gdn_scan(gq_, gk_, gv_, gkt, gbg, gbgt, lc)

        xs, f, logits_t = _outproj(
            layer, xs, out_a, out_b, out_c, o_f, o_b, dz, mod_t,
            w_out[layer].reshape(4, MIXER_W, d).astype(BF16), jnp.tile(dn_out_norm[layer], DN_HEADS)[None, :], g64,
            norm_ffn[layer][None, :], router_w[layer].T, jnp.broadcast_to(router_b[layer][:, None], (N_EXPERTS, 128)),
            n_ctx_tiles)

        pos_t, gate_t, cnt = _route(logits_t, tri)
        ffn = _moe(layer, f, pos_t, gate_t, cnt[:, :, 0], tril, exp_w_up, exp_b_up, exp_w_down, exp_b_down)
        ffn = ffn.reshape(b, s, d)

    return _final(depth - 1, xs, ffn, mod_t, final_norm[None, :], n_ctx_tiles, l)
```

```python
import functools
import math

import numpy as np
import jax
import jax.numpy as jnp
from jax import lax
from jax.experimental import pallas as pl
from jax.experimental.pallas import tpu as pltpu

F32 = jnp.float32
BF16 = jnp.bfloat16
HIGHEST = lax.Precision.HIGHEST

GRID_W = 64
NORM_EPS = 1e-6
ROPE_THETA = 10000.0
HEAD_DIM = 64
GQA_HEADS = 4
GQA_KV_HEADS = 2
GMLP_GROUPS = 4
GMLP_CH = 64
GMLP_CHUNK = 128
DIFF_HEADS = 4
DIFF_DIM = 32
DN_HEADS = 4
DN_DK = 64
DN_DV = 64
DN_CHUNK = 64
N_EXPERTS = 32
TOP_K = 4
SWIGLU_LIMIT = 7.0
SWIGLU_ALPHA = 1.702

MIXER_W = 256
TQ = 256
GDN_PAIR = 2 * DN_CHUNK
MOE_TM = 1024
MOE_BM = 256
MOE_TC = 512
ATT_KC = 1024
ATT_MW = 256
VMEM_LIMIT = 56 * 1024 * 1024

_SEGS = ("aq", "aqs", "ak", "aks", "av", "bu", "bv", "cq", "cqs", "ck", "cks", "cv")
OFF = {name: i * MIXER_W for i, name in enumerate(_SEGS)}
OFF["dqkv"] = len(_SEGS) * MIXER_W
OFF["dz"] = OFF["dqkv"] + 3 * MIXER_W
OFF["dba"] = OFF["dz"] + MIXER_W
W_EXT = OFF["dba"] + 128


def _dot(a, b):
    return jnp.dot(a, b, preferred_element_type=F32)


def _dot_nt(a, b):
    return lax.dot_general(a, b, (((1,), (1,)), ((), ())), preferred_element_type=F32)


def _dot_tn(a, b):
    return lax.dot_general(a, b, (((0,), (0,)), ((), ())), preferred_element_type=F32)


def _split3(x):
    hi = x.astype(BF16)
    r1 = x - hi.astype(F32)
    mid = r1.astype(BF16)
    lo = (r1 - mid.astype(F32)).astype(BF16)
    return hi, mid, lo


def _dot_sel_r(x, sel):
    hi, mid, lo = _split3(x)
    return _dot(hi, sel) + _dot(mid, sel) + _dot(lo, sel)


def _dot_sel_l(sel, x):
    hi, mid, lo = _split3(x)
    return _dot(sel, hi) + _dot(sel, mid) + _dot(sel, lo)


def _group_sumsq(x, g_same):
    x2 = x * x
    hi = x2.astype(BF16)
    lo = (x2 - hi.astype(F32)).astype(BF16)
    return _dot(hi, g_same) + _dot(lo, g_same)


def _onehot(cond):
    return jnp.where(cond, 1.0, 0.0).astype(BF16)


def _keep(cond, x):
    return jnp.where(cond, x.astype(F32), 0.0).astype(BF16)


def _silu(x):
    return x * jax.nn.sigmoid(x)


def _softplus(x):
    return jnp.maximum(x, 0.0) + jnp.log1p(jnp.exp(-jnp.abs(x)))


def _params(sem):
    return pltpu.CompilerParams(dimension_semantics=sem, vmem_limit_bytes=VMEM_LIMIT)


def _mod_kernel(c_ref, w_ref, b_ref, o_ref):
    s = _silu(c_ref[...])
    o_ref[0, 0] = jnp.dot(s, w_ref[0], precision=HIGHEST, preferred_element_type=F32) + b_ref[0]


def _modulation(c_all, w_ada, b_ada):
    depth, d, _ = w_ada.shape
    r = c_all.shape[0]
    return pl.pallas_call(
        _mod_kernel,
        grid=(depth, 6),
        in_specs=[
            pl.BlockSpec((r, d), lambda l, j: (0, 0)),
            pl.BlockSpec((1, d, d), lambda l, j: (l, 0, j)),
            pl.BlockSpec((1, 1, d), lambda l, j: (l, 0, j)),
        ],
        out_specs=pl.BlockSpec((1, 1, r, d), lambda l, j: (l, j, 0, 0)),
        out_shape=jax.ShapeDtypeStruct((depth, 6, r, d), F32),
        compiler_params=_params(("arbitrary", "arbitrary")),
        name="adaln_mod",
    )(c_all, w_ada, b_ada.reshape(depth, 1, 6 * d))


def _inproj_kernel(*refs, has_prev, scale_a, scale_d):
    if has_prev:
        x_ref, ffn_ref, modp_ref, refs = refs[0], refs[1], refs[2], refs[3:]
    else:
        x_ref, refs = refs[0], refs[1:]
    (mod_ref, gmix_ref, w_ref, vec_ref, g64_ref, rope_ref, ws_ref, bst_ref) = refs[:8]
    outs = refs[8:]
    if has_prev:
        xo_ref, outs = outs[0], outs[1:]
    (qa_ref, ka_ref, va_ref, ob_ref, qd_ref, kd_ref, vd_ref, dqkv_ref, dz_ref, dba_ref) = outs

    x = x_ref[0]
    if has_prev:
        x = x + modp_ref[0, 0, 5:6, :] * ffn_ref[0]
        xo_ref[0] = x
    ms = jnp.mean(x * x, axis=-1, keepdims=True)
    xn = x * lax.rsqrt(ms + NORM_EPS) * gmix_ref[...]
    h = xn * (1.0 + mod_ref[0, 0, 1:2, :]) + mod_ref[0, 0, 0:1, :]
    p = _dot(h.astype(BF16), w_ref[...])

    def seg(name, width=MIXER_W):
        return p[:, OFF[name]:OFF[name] + width]

    g64 = g64_ref[...]
    cos_a, sin_a, cos_d, sin_d = rope_ref[0], rope_ref[1], rope_ref[2], rope_ref[3]

    def norm_rope(x0, xs, gain, gain_s, scale):
        r = lax.rsqrt(_group_sumsq(x0, g64) * (1.0 / HEAD_DIM) + NORM_EPS)
        return ((x0 * r * gain) * cos_a + (xs * r * gain_s) * sin_a) * scale

    qa_ref[0] = norm_rope(seg("aq"), seg("aqs"), vec_ref[0:1, :], vec_ref[1:2, :], scale_a).astype(BF16)
    ka_ref[0] = norm_rope(seg("ak"), seg("aks"), vec_ref[2:3, :], vec_ref[3:4, :], 1.0).astype(BF16)
    va_ref[0] = seg("av").astype(BF16)

    u = jax.nn.gelu(seg("bu"))
    v = jax.nn.gelu(seg("bv"))
    vn = v * lax.rsqrt(jnp.mean(v * v, axis=-1, keepdims=True) + NORM_EPS) * vec_ref[4:5, :]
    lane_grp = lax.broadcasted_iota(jnp.int32, (GMLP_CHUNK, MIXER_W), 1) // GMLP_CH
    for ci in range(TQ // GMLP_CHUNK):
        rows = slice(ci * GMLP_CHUNK, (ci + 1) * GMLP_CHUNK)
        vc = vn[rows]
        sp = bst_ref[...]
        for g in range(GMLP_GROUPS):
            sp = sp + _dot(ws_ref[g], jnp.where(lane_grp == g, vc, 0.0).astype(BF16))
        ob_ref[0, rows, :] = (u[rows] * sp).astype(BF16)

    qd_ref[0] = ((seg("cq") * cos_d + seg("cqs") * sin_d) * scale_d).astype(BF16)
    kd_ref[0] = (seg("ck") * cos_d + seg("cks") * sin_d).astype(BF16)
    vd_ref[0] = seg("cv").astype(BF16)

    dqkv_ref[0] = seg("dqkv", 3 * MIXER_W)
    dz_ref[0] = seg("dz")
    dba_ref[0] = seg("dba", 128)


def _inproj(layer, x, prev, mod_t, gmix, w_ext, vecs, g64, rope, ws, bst, n_ctx_tiles):
    b, s, d = x.shape
    nt = s // TQ
    ctx_row = b

    def mod_map(l):
        return lambda i, bb: (l, jnp.where(i < n_ctx_tiles, ctx_row, bb), 0, 0)

    tok = lambda i, bb: (bb, i, 0)
    const2 = lambda i, bb: (0, 0)
    const3 = lambda i, bb: (0, 0, 0)
    in_specs = [pl.BlockSpec((1, TQ, d), tok)]
    args = [x]
    if prev is not None:
        in_specs += [pl.BlockSpec((1, TQ, d), tok), pl.BlockSpec((1, 1, 6, d), mod_map(layer - 1))]
        args += [prev, mod_t]
    in_specs += [
        pl.BlockSpec((1, 1, 6, d), mod_map(layer)),
        pl.BlockSpec((1, d), const2),
        pl.BlockSpec((d, W_EXT), const2),
        pl.BlockSpec((8, MIXER_W), const2),
        pl.BlockSpec((MIXER_W, MIXER_W), const2),
        pl.BlockSpec((4, TQ, MIXER_W), lambda i, bb: (0, i, 0)),
        pl.BlockSpec((GMLP_GROUPS, GMLP_CHUNK, GMLP_CHUNK), const3),
        pl.BlockSpec((GMLP_CHUNK, MIXER_W), const2),
    ]
    args += [mod_t, gmix, w_ext, vecs, g64, rope, ws, bst]
    bf = lambda w: jax.ShapeDtypeStruct((b, s, w), BF16)
    ff = lambda w: jax.ShapeDtypeStruct((b, s, w), F32)
    out_shape = [bf(MIXER_W)] * 7 + [ff(3 * MIXER_W), ff(MIXER_W), ff(128)]
    out_specs = [pl.BlockSpec((1, TQ, MIXER_W), tok)] * 7 + [
        pl.BlockSpec((1, TQ, 3 * MIXER_W), tok), pl.BlockSpec((1, TQ, MIXER_W), tok), pl.BlockSpec((1, TQ, 128), tok)]
    if prev is not None:
        out_shape = [ff(d)] + out_shape
        out_specs = [pl.BlockSpec((1, TQ, d), tok)] + out_specs
    outs = pl.pallas_call(
        functools.partial(_inproj_kernel, has_prev=prev is not None,
                          scale_a=HEAD_DIM ** -0.5, scale_d=DIFF_DIM ** -0.5),
        grid=(nt, b), in_specs=in_specs, out_specs=out_specs, out_shape=out_shape,
        compiler_params=_params(("arbitrary", "arbitrary")), name="in_proj",
    )(*args)
    if prev is not None:
        return outs[0], outs[1:]
    return x, outs


def _ones_outside(v, keep_lanes):
    keep = jnp.where(keep_lanes, 1.0, 0.0)
    return v * keep.astype(BF16) + (1.0 - keep).astype(BF16)


def _key_chunks(n_keys):
    cuts = list(range(0, n_keys, ATT_KC)) + [n_keys]
    return list(zip(cuts[:-1], cuts[1:]))


def _attend_groups(qms, k_ref, v1_ref, n_keys, s_scr):
    chunks = _key_chunks(n_keys)

    def scores(g, lo, hi, mx):
        s_c = _dot_nt(qms[g], k_ref[0, lo:hi, :])
        s_scr[g % 2, :, lo:hi] = s_c
        for a in range(lo, hi, ATT_MW):
            piece = s_c[:, a - lo:a - lo + ATT_MW]
            mx = piece if mx is None else jnp.maximum(mx, piece)
        return mx

    def values(g, lo, hi, m, acc):
        e = jnp.exp(s_scr[g % 2, :, lo:hi] - m).astype(BF16)
        part = _dot(e, v1_ref[g, lo:hi, :])
        return part if acc is None else acc + part

    mx = None
    for lo, hi in chunks:
        mx = scores(0, lo, hi, mx)
    outs = []
    for g in range(len(qms)):
        m = jnp.max(mx, axis=-1, keepdims=True)
        mx, acc = None, None
        for lo, hi in chunks:
            if g + 1 < len(qms):
                mx = scores(g + 1, lo, hi, mx)
            acc = values(g, lo, hi, m, acc)
        outs.append(acc / jnp.concatenate([acc[:, 128:], acc[:, :128]], axis=1))
    return outs


def _gqa_kernel(q_ref, k_ref, v_ref, o_ref, v1_scr, s_scr, *, n_ctx_tiles, lc):
    i = pl.program_id(1)
    lane_head = lax.broadcasted_iota(jnp.int32, (TQ, MIXER_W), 1) // HEAD_DIM
    lane_half = lax.broadcasted_iota(jnp.int32, (1, MIXER_W), 1) // 128
    pairs = GQA_HEADS // 2

    @pl.when(i == 0)
    def _():
        for pair in range(pairs):
            v1_scr[pair] = _ones_outside(v_ref[0], lane_half == pair)

    def attend(n_keys):
        q = q_ref[0]
        qms = [jnp.concatenate([_keep(lane_head == 2 * p, q), _keep(lane_head == 2 * p + 1, q)], axis=0)
               for p in range(pairs)]
        outs = _attend_groups(qms, k_ref, v1_scr, n_keys, s_scr)
        acc = jnp.zeros((TQ, MIXER_W), F32)
        for p in range(pairs):
            acc = jnp.where(lane_head == 2 * p, outs[p][:TQ], jnp.where(lane_head == 2 * p + 1, outs[p][TQ:], acc))
        o_ref[0] = acc.astype(BF16)

    @pl.when(i < n_ctx_tiles)
    def _():
        attend(lc)

    @pl.when(i >= n_ctx_tiles)
    def _():
        attend(k_ref.shape[1])


def _diff_kernel(q_ref, k_ref, v_ref, lam_ref, sub_ref, g64_ref, o_ref, v1_scr, s_scr, *, n_ctx_tiles, lc, lam_init):
    i = pl.program_id(1)
    lane = lax.broadcasted_iota(jnp.int32, (TQ, MIXER_W), 1)
    lane_head = lane // (2 * DIFF_DIM)
    lane_map = lane // DIFF_DIM
    head_of_lane = lax.broadcasted_iota(jnp.int32, (1, MIXER_W), 1) // (2 * DIFF_DIM)
    lp = lam_ref[...]
    lam = (jnp.exp(jnp.sum(lp[0:1] * lp[1:2], axis=-1, keepdims=True))
           - jnp.exp(jnp.sum(lp[2:3] * lp[3:4], axis=-1, keepdims=True)) + lam_init)

    @pl.when(i == 0)
    def _():
        for hd in range(DIFF_HEADS):
            v1_scr[hd] = _ones_outside(v_ref[0], head_of_lane == hd)

    def attend(n_keys):
        q = q_ref[0]
        qms = [jnp.concatenate([_keep(lane_map == 2 * hd, q), _keep(lane_map == 2 * hd + 1, q)], axis=0)
               for hd in range(DIFF_HEADS)]
        outs = _attend_groups(qms, k_ref, v1_scr, n_keys, s_scr)
        acc = jnp.zeros((TQ, MIXER_W), F32)
        for hd in range(DIFF_HEADS):
            acc = jnp.where(lane_head == hd, outs[hd][:TQ] - lam * outs[hd][TQ:], acc)
        r = lax.rsqrt(_group_sumsq(acc, g64_ref[...]) * (1.0 / (2 * DIFF_DIM)) + NORM_EPS)
        o_ref[0] = (acc * r * sub_ref[...] * (1.0 - lam_init)).astype(BF16)

    @pl.when(i < n_ctx_tiles)
    def _():
        attend(lc)

    @pl.when(i >= n_ctx_tiles)
    def _():
        attend(k_ref.shape[1])


def _attention(kind, q, k, v, n_ctx_tiles, lc, extra=(), lam_init=0.0):
    b, s, w = q.shape
    nt = s // TQ
    tok = lambda bb, i: (bb, i, 0)
    row = lambda bb, i: (bb, 0, 0)
    in_specs = [pl.BlockSpec((1, TQ, w), tok), pl.BlockSpec((1, s, w), row), pl.BlockSpec((1, s, w), row)]
    if kind == "gqa":
        groups = GQA_HEADS // 2
        body = functools.partial(_gqa_kernel, n_ctx_tiles=n_ctx_tiles, lc=lc)
    else:
        groups = DIFF_HEADS
        body = functools.partial(_diff_kernel, n_ctx_tiles=n_ctx_tiles, lc=lc, lam_init=lam_init)
        in_specs += [pl.BlockSpec(e.shape, lambda bb, i: (0, 0)) for e in extra]
    return pl.pallas_call(
        body, grid=(b, nt), in_specs=in_specs, out_specs=pl.BlockSpec((1, TQ, w), tok),
        out_shape=jax.ShapeDtypeStruct((b, s, w), BF16),
        scratch_shapes=[pltpu.VMEM((groups, s, w), BF16), pltpu.VMEM((2, 2 * TQ, s), F32)],
        compiler_params=_params(("arbitrary", "arbitrary")), name=kind + "_attention",
    )(q, k, v, *extra)


def _gdn_prep_kernel(x_ref, xp_ref, xn_ref, ba_ref, cw_ref, av_ref, g64_ref,
                     q_ref, k_ref, v_ref, kt_ref, bg_ref, bgt_ref, *, n_ctx_tiles, n_tiles):
    i = pl.program_id(0)
    x = x_ref[0]
    has_prev = jnp.where((i != 0) & (i != n_ctx_tiles), 1.0, 0.0)
    has_next = jnp.where((i != n_ctx_tiles - 1) & (i != n_tiles - 1), 1.0, 0.0)
    row = lax.broadcasted_iota(jnp.int32, x.shape, 0)
    x_m1 = jnp.where(row == 0, xp_ref[0, 7:8, :] * has_prev, pltpu.roll(x, 1, 0))
    x_p1 = jnp.where(row == TQ - 1, xn_ref[0, 0:1, :] * has_next, pltpu.roll(x, TQ - 1, 0))
    y = _silu(cw_ref[0:1, :] * x_m1 + cw_ref[1:2, :] * x + cw_ref[2:3, :] * x_p1)
    q, k, v = y[:, :MIXER_W], y[:, MIXER_W:2 * MIXER_W], y[:, 2 * MIXER_W:]
    g64 = g64_ref[...]
    qn = q * lax.rsqrt(_group_sumsq(q, g64) + NORM_EPS) * (DN_DK ** -0.5)
    kn = k * lax.rsqrt(_group_sumsq(k, g64) + NORM_EPS)
    q_ref[0] = qn.astype(BF16)
    k_ref[0] = kn.astype(BF16)
    v_ref[0] = v.astype(BF16)
    kt_ref[0] = kn.T.astype(BF16)
    ba = ba_ref[0]
    lane = lax.broadcasted_iota(jnp.int32, ba.shape, 1)
    beta = jax.nn.sigmoid(ba)
    g = -jnp.exp(av_ref[0:1, :]) * _softplus(ba + av_ref[1:2, :])
    bg = jnp.where(lane < 2 * DN_HEADS, beta, jnp.where(lane < 4 * DN_HEADS, g, 0.0))
    bg_ref[0] = bg
    bgt_ref[0] = bg.T


def _gdn_prep(dqkv, dba, conv_w, avec, g64, n_ctx_tiles):
    b, s, w3 = dqkv.shape
    nt = s // TQ
    nb8 = s // 8
    tok = lambda i, bb: (bb, i, 0)
    tokt = lambda i, bb: (bb, 0, i)
    const2 = lambda i, bb: (0, 0)
    outs = pl.pallas_call(
        functools.partial(_gdn_prep_kernel, n_ctx_tiles=n_ctx_tiles, n_tiles=nt),
        grid=(nt, b),
        in_specs=[
            pl.BlockSpec((1, TQ, w3), tok),
            pl.BlockSpec((1, 8, w3), lambda i, bb: (bb, jnp.maximum(i * (TQ // 8) - 1, 0), 0)),
            pl.BlockSpec((1, 8, w3), lambda i, bb: (bb, jnp.minimum((i + 1) * (TQ // 8), nb8 - 1), 0)),
            pl.BlockSpec((1, TQ, 128), tok),
            pl.BlockSpec((3, w3), const2),
            pl.BlockSpec((2, 128), const2),
            pl.BlockSpec((MIXER_W, MIXER_W), const2),
        ],
        out_specs=[pl.BlockSpec((1, TQ, MIXER_W), tok)] * 3 + [
            pl.BlockSpec((1, MIXER_W, TQ), tokt), pl.BlockSpec((1, TQ, 128), tok), pl.BlockSpec((1, 128, TQ), tokt)],
        out_shape=[jax.ShapeDtypeStruct((b, s, MIXER_W), BF16)] * 3 + [
            jax.ShapeDtypeStruct((b, MIXER_W, s), BF16), jax.ShapeDtypeStruct((b, s, 128), F32),
            jax.ShapeDtypeStruct((b, 128, s), F32)],
        compiler_params=_params(("arbitrary", "arbitrary")), name="gdn_prep",
    )(dqkv, dqkv, dqkv, dba, conv_w, avec, g64)
    return outs


def _gdn_scan_kernel(qf, kf, vf, ktf, bgf, bgtf, qb, kb, vb, ktb, bgb, bgtb, of_ref, ob_ref, st_ref):
    @pl.when(pl.program_id(1) == 0)
    def _():
        st_ref[...] = jnp.zeros_like(st_ref)

    pp, cc, w = GDN_PAIR, DN_CHUNK, MIXER_W
    dirs = (0, 1)
    heads = range(DN_HEADS)
    chains = [(d, hd) for d in dirs for hd in heads]
    q_refs, k_refs, v_refs, kt_refs = (qf, qb), (kf, kb), (vf, vb), (ktf, ktb)
    bg_refs, bgt_refs, o_refs = (bgf, bgb), (bgtf, bgtb), (of_ref, ob_ref)

    ii = lax.broadcasted_iota(jnp.int32, (pp, pp), 0)
    jj = lax.broadcasted_iota(jnp.int32, (pp, pp), 1)
    same = (ii // cc) == (jj // cc)
    incl = (same & (jj <= ii), same & (jj >= ii))
    strict = (same & (jj < ii), same & (jj > ii))
    incl_b = [_onehot(m) for m in incl]
    incl_tb = [_onehot(same & (ii <= jj)), _onehot(same & (ii >= jj))]
    eye = jnp.where(ii == jj, 1.0, 0.0)
    merge_masks = [((ii // (2 * sz)) == (jj // (2 * sz))) & ((ii // sz) != (jj // sz))
                   for sz in (2 ** e for e in range(int(math.log2(cc))))]

    src = lax.broadcasted_iota(jnp.int32, (128, w), 0)
    lane_w = lax.broadcasted_iota(jnp.int32, (128, w), 1)
    src2 = lax.broadcasted_iota(jnp.int32, (128, DN_HEADS * pp), 0)
    lane2 = lax.broadcasted_iota(jnp.int32, (128, DN_HEADS * pp), 1)
    bg = [r[0] for r in bg_refs]
    beta_x = [_dot_sel_r(bg[d], _onehot(src == DN_HEADS * d + lane_w // DN_DV)) for d in dirs]
    g_x = [_dot_sel_r(bg[d], _onehot(src == 2 * DN_HEADS + DN_HEADS * d + lane_w // DN_DV)) for d in dirs]
    g_x2 = [_dot_sel_r(bg[d], _onehot(src2 == 2 * DN_HEADS + DN_HEADS * d + lane2 // pp)) for d in dirs]
    cg_rows = [_dot_sel_r(bgt_refs[d][0], incl_tb[d]) for d in dirs]
    cg_x = [_dot_sel_l(incl_b[d], g_x[d]) for d in dirs]
    cg_x2 = [_dot_sel_l(incl_b[d], g_x2[d]) for d in dirs]

    q = [r[0] for r in q_refs]
    k = [r[0] for r in k_refs]
    kf32 = [t.astype(F32) for t in k]
    e_cg = [jnp.exp(t) for t in cg_x]
    rhs_v = [v_refs[d][0].astype(F32) * beta_x[d] for d in dirs]
    rhs_k = [kf32[d] * (beta_x[d] * e_cg[d]) for d in dirs]
    k_beta = [kf32[d] * beta_x[d] for d in dirs]
    lane_head = lax.broadcasted_iota(jnp.int32, (pp, w), 1) // DN_DV

    kk = [_dot_nt(jnp.where(lane_head == hd, k_beta[d], 0.0).astype(BF16), k[d]) for d, hd in chains]
    qk = [_dot_nt(_keep(lane_head == hd, q[d]), k[d]) for d, hd in chains]
    a, qkd = [], []
    for ci, (d, hd) in enumerate(chains):
        gl = 2 * DN_HEADS + DN_HEADS * d + hd
        diff = jnp.where(incl[d], cg_x2[d][:, hd * pp:(hd + 1) * pp] - cg_rows[d][gl:gl + 1, :], 0.0)
        decay = jnp.where(incl[d], jnp.exp(diff), 0.0)
        a.append(jnp.where(strict[d], kk[ci] * decay, 0.0))
        qkd.append((qk[ci] * decay).astype(BF16))

    t_inv = [eye - jnp.where(merge_masks[0], a_c, 0.0) for a_c in a]
    for mask in merge_masks[1:]:
        tb = [t.astype(BF16) for t in t_inv]
        lm = [_dot(jnp.where(mask, a_c, 0.0).astype(BF16), tb_c).astype(BF16) for a_c, tb_c in zip(a, tb)]
        t_inv = [t - _dot(tb_c, lm_c) for t, tb_c, lm_c in zip(t_inv, tb, lm)]
    tb = [t.astype(BF16) for t in t_inv]
    u_part = [_dot(tb[ci], jnp.where(lane_head == hd, rhs_v[d], 0.0).astype(BF16)) for ci, (d, hd) in enumerate(chains)]
    w_part = [_dot(tb[ci], jnp.where(lane_head == hd, rhs_k[d], 0.0).astype(BF16)) for ci, (d, hd) in enumerate(chains)]
    u_all = [sum(u_part[d * DN_HEADS + hd] for hd in heads) for d in dirs]
    w_all = [sum(w_part[d * DN_HEADS + hd] for hd in heads) for d in dirs]

    st = [st_ref[d] for d in dirs]
    blk = (lax.broadcasted_iota(jnp.int32, (w, w), 0) // DN_DK) == (lax.broadcasted_iota(jnp.int32, (w, w), 1) // DN_DV)
    kt = [r[0] for r in kt_refs]
    order = (((0, cc), (cc, pp)), ((cc, pp), (0, cc)))
    zeros_c = jnp.zeros((cc, w), F32)
    lane_head_c = lax.broadcasted_iota(jnp.int32, (cc, w), 1) // DN_DV

    def place(lo, t):
        return jnp.concatenate([t, zeros_c] if lo == 0 else [zeros_c, t], axis=0)

    nv_acc = [None, None]
    for step in range(2):
        lo = [order[d][step][0] for d in dirs]
        rows = [slice(*order[d][step]) for d in dirs]
        stb = [t.astype(BF16) for t in st]
        w_s = [_dot(w_all[d][rows[d]].astype(BF16), stb[d]) for d in dirs]
        q_s = [_dot(q[d][rows[d]], stb[d]) for d in dirs]
        nv = [u_all[d][rows[d]] - w_s[d] for d in dirs]
        for d in dirs:
            full = place(lo[d], nv[d])
            nv_acc[d] = full if nv_acc[d] is None else nv_acc[d] + full
        nvb = [t.astype(BF16) for t in nv_acc]
        intra = [_dot(qkd[ci][rows[d]], nvb[d]) for ci, (d, hd) in enumerate(chains)]
        last = [order[0][step][1] - 1, order[1][step][0]]
        g_end = [cg_x[d][last[d]:last[d] + 1, :] for d in dirs]
        nvs = [place(lo[d], nv[d] * jnp.exp(g_end[d] - cg_x[d][rows[d]])).astype(BF16) for d in dirs]
        upd = [_dot(kt[d], nvs[d]) for d in dirs]
        for d in dirs:
            o = e_cg[d][rows[d]] * q_s[d]
            for hd in heads:
                o = o + jnp.where(lane_head_c == hd, intra[d * DN_HEADS + hd], 0.0)
            o_refs[d][0, rows[d], :] = o
            st[d] = st[d] * jnp.exp(g_end[d]) + jnp.where(blk, upd[d], 0.0)
    for d in dirs:
        st_ref[d] = st[d]


def _gdn_scan(q, k, v, kt, bg, bgt, lc):
    b, s, w = q.shape
    n_pairs = s // GDN_PAIR
    ncp = lc // GDN_PAIR

    def fwd(bb, i):
        return i

    def bwd(bb, i):
        return jnp.where(i < ncp, ncp - 1 - i, n_pairs - 1 + ncp - i)

    def specs(pos):
        tok = lambda bb, i: (bb, pos(bb, i), 0)
        tokt = lambda bb, i: (bb, 0, pos(bb, i))
        return [pl.BlockSpec((1, GDN_PAIR, w), tok)] * 3 + [
            pl.BlockSpec((1, w, GDN_PAIR), tokt), pl.BlockSpec((1, GDN_PAIR, 128), tok),
            pl.BlockSpec((1, 128, GDN_PAIR), tokt)]

    return pl.pallas_call(
        _gdn_scan_kernel, grid=(b, n_pairs),
        in_specs=specs(fwd) + specs(bwd),
        out_specs=[pl.BlockSpec((1, GDN_PAIR, w), lambda bb, i: (bb, fwd(bb, i), 0)),
                   pl.BlockSpec((1, GDN_PAIR, w), lambda bb, i: (bb, bwd(bb, i), 0))],
        out_shape=[jax.ShapeDtypeStruct((b, s, w), F32)] * 2,
        scratch_shapes=[pltpu.VMEM((2, w, w), F32)],
        compiler_params=_params(("arbitrary", "arbitrary")), name="gdn_scan",
    )(q, k, v, kt, bg, bgt, q, k, v, kt, bg, bgt)


def _outproj_kernel(x_ref, oa_ref, ob_ref, oc_ref, of_ref, obw_ref, dz_ref, mod_ref, w_ref, gout_ref,
                    g64_ref, gffn_ref, wr_ref, br_ref, xo_ref, f_ref, lg_ref):
    o = of_ref[0] + obw_ref[0]
    r = lax.rsqrt(_group_sumsq(o, g64_ref[...]) * (1.0 / DN_DV) + NORM_EPS)
    od = (o * r * gout_ref[...] * _silu(dz_ref[0])).astype(BF16)
    y = (_dot(oa_ref[0], w_ref[0]) + _dot(ob_ref[0], w_ref[1]) + _dot(oc_ref[0], w_ref[2]) + _dot(od, w_ref[3]))
    x = x_ref[0] + mod_ref[0, 0, 2:3, :] * y
    xo_ref[0] = x
    ms = jnp.mean(x * x, axis=-1, keepdims=True)
    f = (x * lax.rsqrt(ms + NORM_EPS) * gffn_ref[...]) * (1.0 + mod_ref[0, 0, 4:5, :]) + mod_ref[0, 0, 3:4, :]
    f_ref[...] = f
    lg_ref[...] = lax.dot_general(wr_ref[...], f, (((1,), (1,)), ((), ())), precision=HIGHEST,
                                  preferred_element_type=F32) + br_ref[:, 0:1]


def _outproj(layer, x, oa, ob, oc, o_f, o_b, dz, mod_t, w_out4, gout, g64, gffn, wr_t, br, n_ctx_tiles):
    b, s, d = x.shape
    nt = s // TQ
    ctx_row = b
    tok = lambda i, bb: (bb, i, 0)
    const2 = lambda i, bb: (0, 0)
    slab = pl.BlockSpec((1, TQ, MIXER_W), tok)
    return pl.pallas_call(
        _outproj_kernel, grid=(nt, b),
        in_specs=[pl.BlockSpec((1, TQ, d), tok), slab, slab, slab, slab, slab, slab,
                  pl.BlockSpec((1, 1, 6, d), lambda i, bb: (layer, jnp.where(i < n_ctx_tiles, ctx_row, bb), 0, 0)),
                  pl.BlockSpec((4, MIXER_W, d), lambda i, bb: (0, 0, 0)),
                  pl.BlockSpec((1, MIXER_W), const2), pl.BlockSpec((MIXER_W, MIXER_W), const2),
                  pl.BlockSpec((1, d), const2), pl.BlockSpec((N_EXPERTS, d), const2),
                  pl.BlockSpec((N_EXPERTS, 128), const2)],
        out_specs=[pl.BlockSpec((1, TQ, d), tok), pl.BlockSpec((TQ, d), lambda i, bb: (bb * nt + i, 0)),
                   pl.BlockSpec((N_EXPERTS, TQ), lambda i, bb: (0, bb * nt + i))],
        out_shape=[jax.ShapeDtypeStruct((b, s, d), F32), jax.ShapeDtypeStruct((b * s, d), F32),
                   jax.ShapeDtypeStruct((N_EXPERTS, b * s), F32)],
        compiler_params=_params(("arbitrary", "arbitrary")), name="out_proj",
    )(x, oa, ob, oc, o_f, o_b, dz, mod_t, w_out4, gout, g64, gffn, wr_t, br)


def _route_kernel(lg_ref, tri_ref, pos_ref, gate_ref, cnt_ref):
    x = lg_ref[...]
    e_iota = lax.broadcasted_iota(jnp.int32, x.shape, 0).astype(F32)
    work = x
    chosen = jnp.zeros(x.shape, F32)
    top = None
    den = None
    for kk in range(TOP_K):
        m = jnp.max(work, axis=0, keepdims=True)
        idx = jnp.min(jnp.where(work == m, e_iota, float(N_EXPERTS)), axis=0, keepdims=True)
        pick = e_iota == idx
        chosen = jnp.where(pick, 1.0, chosen)
        if kk == 0:
            top = m
            den = jnp.ones_like(m)
        else:
            den = den + jnp.exp(m - top)
        work = jnp.where(pick, -jnp.inf, work)
    sel = chosen > 0.5
    gate_ref[0] = jnp.where(sel, jnp.exp(x - top) / den, 0.0)
    rank = _dot(chosen.astype(BF16), tri_ref[...])
    pos_ref[0] = jnp.where(sel, rank.astype(jnp.int32), -1)
    cnt = jnp.sum(chosen, axis=1, keepdims=True).astype(jnp.int32)
    cnt_ref[0] = jnp.broadcast_to(cnt, cnt_ref.shape[1:])


def _route(logits_t, tri):
    n_exp, t = logits_t.shape
    n_tiles = t // MOE_TM
    return pl.pallas_call(
        _route_kernel, grid=(n_tiles,),
        in_specs=[pl.BlockSpec((n_exp, MOE_TM), lambda i: (0, i)), pl.BlockSpec((MOE_TM, MOE_TM), lambda i: (0, 0))],
        out_specs=[pl.BlockSpec((1, n_exp, MOE_TM), lambda i: (i, 0, 0)),
                   pl.BlockSpec((1, n_exp, MOE_TM), lambda i: (i, 0, 0)),
                   pl.BlockSpec((1, n_exp, 128), lambda i: (i, 0, 0))],
        out_shape=[jax.ShapeDtypeStruct((n_tiles, n_exp, MOE_TM), jnp.int32),
                   jax.ShapeDtypeStruct((n_tiles, n_exp, MOE_TM), F32),
                   jax.ShapeDtypeStruct((n_tiles, n_exp, 128), jnp.int32)],
        compiler_params=_params(("arbitrary",)), name="route",
    )(logits_t, tri)


def _slots_kernel(pos_ref, gate_ref, base_ref, tril_ref, dest_ref, gk_ref):
    pos = pos_ref[0]
    chosen = pos >= 0
    slot = (base_ref[0][:, 0:1] + pos).astype(F32)
    choice = _dot(tril_ref[...], _onehot(chosen))
    gate = gate_ref[0]
    pad = jnp.zeros((8 - TOP_K, pos.shape[1]), F32)
    d_rows, g_rows = [], []
    for kk in range(TOP_K):
        mine = chosen & (choice == float(kk))
        d_rows.append(jnp.sum(jnp.where(mine, slot, 0.0), axis=0, keepdims=True))
        g_rows.append(jnp.sum(jnp.where(mine, gate, 0.0), axis=0, keepdims=True))
    dest_ref[0] = jnp.concatenate(d_rows + [pad], axis=0).astype(jnp.int32)
    gk_ref[0] = jnp.concatenate(g_rows + [pad], axis=0)


def _slots(pos_t, gate_t, base_b, tril):
    n_tiles, n_exp, tm = pos_t.shape
    tile = lambda i: (i, 0, 0)
    return pl.pallas_call(
        _slots_kernel, grid=(n_tiles,),
        in_specs=[pl.BlockSpec((1, n_exp, tm), tile), pl.BlockSpec((1, n_exp, tm), tile),
                  pl.BlockSpec((1, n_exp, 128), tile), pl.BlockSpec((n_exp, n_exp), lambda i: (0, 0))],
        out_specs=[pl.BlockSpec((1, 8, tm), tile), pl.BlockSpec((1, 8, tm), tile)],
        out_shape=[jax.ShapeDtypeStruct((n_tiles, 8, tm), jnp.int32), jax.ShapeDtypeStruct((n_tiles, 8, tm), F32)],
        compiler_params=_params(("arbitrary",)), name="moe_slots",
    )(pos_t, gate_t, base_b, tril)


def _dispatch_kernel(pad_ref, f_ref, dest_ref, xs_ref, zbuf, zsem, sem):
    tm = f_ref.shape[0]
    n_blocks = xs_ref.shape[0] // MOE_BM

    @pl.when(pl.program_id(0) == 0)
    def _():
        zbuf[...] = jnp.zeros_like(zbuf)
        live = pad_ref[2 * N_EXPERTS]

        def zero_block(row):
            return pltpu.make_async_copy(zbuf, xs_ref.at[pl.ds(pl.multiple_of(row, MOE_BM), MOE_BM)], zsem)

        for wait in (False, True):
            for e in range(N_EXPERTS):
                for cond, row in ((pad_ref[N_EXPERTS + e] > 0, pad_ref[e]), (live + e < n_blocks, (live + e) * MOE_BM)):
                    @pl.when(cond)
                    def _():
                        zero_block(row).wait() if wait else zero_block(row).start()

    def rows(tok, carry):
        for kk in range(TOP_K):
            pltpu.make_async_copy(f_ref.at[pl.ds(tok, 1)], xs_ref.at[pl.ds(dest_ref[0, kk, tok], 1)],
                                  sem).start(priority=kk % 2)
        return carry

    lax.fori_loop(0, tm, rows, 0, unroll=8)
    for kk in range(TOP_K):
        pltpu.make_async_copy(f_ref, xs_ref.at[pl.ds(0, tm)], sem).wait()


def _dispatch(pad_info, f, dest, n_slots):
    t, d = f.shape
    n_tiles, _, tm = dest.shape
    grid_spec = pltpu.PrefetchScalarGridSpec(
        num_scalar_prefetch=1, grid=(n_tiles,),
        in_specs=[pl.BlockSpec((tm, d), lambda i, p: (i, 0)),
                  pl.BlockSpec((1, 8, tm), lambda i, p: (i, 0, 0), memory_space=pltpu.SMEM)],
        out_specs=pl.BlockSpec(memory_space=pl.ANY),
        scratch_shapes=[pltpu.VMEM((MOE_BM, d), F32), pltpu.SemaphoreType.DMA(()), pltpu.SemaphoreType.DMA(())],
    )
    return pl.pallas_call(
        _dispatch_kernel, grid_spec=grid_spec, out_shape=jax.ShapeDtypeStruct((n_slots, d), F32),
        compiler_params=_params(("arbitrary",)), name="moe_dispatch",
    )(pad_info, f, dest)


def _experts_kernel(meta_ref, xs_ref, wu_ref, bu_ref, wd_ref, bd_ref, ys_ref, wu_bf, wd_bf):
    i = pl.program_id(0)
    nb = pl.num_programs(0)
    live = i < meta_ref[2 * nb]
    de = wd_bf.shape[0]

    @pl.when(live & (meta_ref[nb + i] > 0))
    def _():
        wu_bf[...] = wu_ref[0, 0].astype(BF16)
        wd_bf[...] = wd_ref[0, 0].astype(BF16)

    @pl.when(live)
    def _():
        hgu = _dot(xs_ref[...].astype(BF16), wu_bf[...]) + bu_ref[0, 0]
        gate = jnp.minimum(hgu[:, :de], SWIGLU_LIMIT)
        up = jnp.clip(hgu[:, de:], -SWIGLU_LIMIT, SWIGLU_LIMIT)
        hid = gate * jax.nn.sigmoid(SWIGLU_ALPHA * gate) * (up + 1.0)
        ys_ref[...] = _dot(hid.astype(BF16), wd_bf[...]) + bd_ref[0, 0]

    @pl.when(jnp.logical_not(live))
    def _():
        ys_ref[...] = jnp.zeros_like(ys_ref)


def _experts(layer, meta, xs, w_up, b_up, w_down, b_down):
    n_slots, d = xs.shape
    nb = n_slots // MOE_BM
    depth, n_exp, _, de2 = w_up.shape
    de = de2 // 2
    blk = lambda i, m: (jnp.minimum(i, m[2 * nb] - 1), 0)
    exp = lambda i, m: (layer, m[i], 0, 0)
    grid_spec = pltpu.PrefetchScalarGridSpec(
        num_scalar_prefetch=1, grid=(nb,),
        in_specs=[pl.BlockSpec((MOE_BM, d), blk),
                  pl.BlockSpec((1, 1, d, de2), exp), pl.BlockSpec((1, 1, 1, de2), exp),
                  pl.BlockSpec((1, 1, de, d), exp), pl.BlockSpec((1, 1, 1, d), exp)],
        out_specs=pl.BlockSpec((MOE_BM, d), lambda i, m: (i, 0)),
        scratch_shapes=[pltpu.VMEM((d, de2), BF16), pltpu.VMEM((de, d), BF16)],
    )
    return pl.pallas_call(
        _experts_kernel, grid_spec=grid_spec, out_shape=jax.ShapeDtypeStruct((n_slots, d), F32),
        compiler_params=_params(("arbitrary",)), name="moe_experts",
    )(meta, xs, w_up, b_up.reshape(depth, n_exp, 1, de2), w_down, b_down.reshape(depth, n_exp, 1, d))


def _combine_kernel(dest_ref, gk_ref, ys_ref, o_ref, ybuf, sem):
    tc, d = o_ref.shape

    def rows(tok, carry):
        for kk in range(TOP_K):
            pltpu.make_async_copy(ys_ref.at[pl.ds(dest_ref[0, kk, tok], 1)], ybuf.at[kk, pl.ds(tok, 1)],
                                  sem).start(priority=kk % 2)
        return carry

    lax.fori_loop(0, tc, rows, 0, unroll=8)
    for kk in range(TOP_K):
        pltpu.make_async_copy(ys_ref.at[pl.ds(0, tc)], ybuf.at[kk], sem).wait()

    hi, mid, lo = _split3(gk_ref[0])
    pick = lax.broadcasted_iota(jnp.int32, (8, 128), 0)
    cols = []
    for kk in range(TOP_K):
        sel = _onehot(pick == kk)
        cols.append(_dot_tn(hi, sel) + _dot_tn(mid, sel) + _dot_tn(lo, sel))
    for j in range(d // 128):
        lanes = slice(j * 128, (j + 1) * 128)
        acc = cols[0] * ybuf[0, :, lanes]
        for kk in range(1, TOP_K):
            acc = acc + cols[kk] * ybuf[kk, :, lanes]
        o_ref[:, lanes] = acc


def _combine(dest, gk, ys):
    n_tiles, _, tm = dest.shape
    d = ys.shape[1]
    per = tm // MOE_TC
    blk = lambda i: (i // per, 0, i % per)
    return pl.pallas_call(
        _combine_kernel, grid=(n_tiles * per,),
        in_specs=[pl.BlockSpec((1, 8, MOE_TC), blk, memory_space=pltpu.SMEM), pl.BlockSpec((1, 8, MOE_TC), blk),
                  pl.BlockSpec(memory_space=pl.ANY)],
        out_specs=pl.BlockSpec((MOE_TC, d), lambda i: (i, 0)),
        out_shape=jax.ShapeDtypeStruct((n_tiles * tm, d), F32),
        scratch_shapes=[pltpu.VMEM((TOP_K, MOE_TC, d), F32), pltpu.SemaphoreType.DMA(())],
        compiler_params=_params(("arbitrary",)), name="moe_combine",
    )(dest, gk, ys)


def _moe(layer, f, pos_t, gate_t, cnt, tril, w_up, b_up, w_down, b_down):
    t = f.shape[0]
    nb = -(-(t * TOP_K + N_EXPERTS * (MOE_BM - 1)) // MOE_BM)
    per_expert = jnp.sum(cnt, axis=0)
    blocks_e = (per_expert + MOE_BM - 1) // MOE_BM
    blk_end = jnp.cumsum(blocks_e)
    blk_start = blk_end - blocks_e
    base = (blk_start * MOE_BM)[None, :] + jnp.cumsum(cnt, axis=0) - cnt
    live = blk_end[-1]
    blk_id = jnp.minimum(jnp.arange(nb, dtype=jnp.int32), live - 1)
    blk_expert = jnp.sum((blk_end[None, :] <= blk_id[:, None]).astype(jnp.int32), axis=1)
    first = (jnp.arange(nb, dtype=jnp.int32) == blk_start[blk_expert]).astype(jnp.int32)
    meta = jnp.concatenate([blk_expert, first, live[None]]).astype(jnp.int32)
    pad_info = jnp.concatenate([jnp.maximum(blk_end - 1, 0) * MOE_BM, (blocks_e > 0).astype(jnp.int32),
                                live[None]]).astype(jnp.int32)
    base_b = jnp.broadcast_to(base[:, :, None], base.shape + (128,)).astype(jnp.int32)

    dest, gk = _slots(pos_t, gate_t, base_b, tril)
    xs = _dispatch(pad_info, f, dest, nb * MOE_BM)
    ys = _experts(layer, meta, xs, w_up, b_up, w_down, b_down)
    return _combine(dest, gk, ys)


def _final_kernel(x_ref, ffn_ref, mod_ref, g_ref, o_ref):
    x = x_ref[0] + mod_ref[0, 0, 5:6, :] * ffn_ref[0]
    ms = jnp.mean(x * x, axis=-1, keepdims=True)
    o_ref[0] = x * lax.rsqrt(ms + NORM_EPS) * g_ref[...]


def _final(layer, x, ffn, mod_t, gain, n_ctx_tiles, l):
    b, s, d = x.shape
    tok = lambda bb, i: (bb, i + n_ctx_tiles, 0)
    return pl.pallas_call(
        _final_kernel, grid=(b, l // TQ),
        in_specs=[pl.BlockSpec((1, TQ, d), tok), pl.BlockSpec((1, TQ, d), tok),
                  pl.BlockSpec((1, 1, 6, d), lambda bb, i: (layer, bb, 0, 0)),
                  pl.BlockSpec((1, d), lambda bb, i: (0, 0))],
        out_specs=pl.BlockSpec((1, TQ, d), lambda bb, i: (bb, i, 0)),
        out_shape=jax.ShapeDtypeStruct((b, l, d), F32),
        compiler_params=_params(("arbitrary", "arbitrary")), name="final_norm",
    )(x, ffn, mod_t, gain)


def _swap_halves(n_groups, width):
    base = np.arange(n_groups * width).reshape(n_groups, width)
    return np.concatenate([base[:, width // 2:], base[:, :width // 2]], axis=1).reshape(-1)


def _in_columns():
    sizes = (GQA_HEADS * HEAD_DIM, GQA_KV_HEADS * HEAD_DIM, GQA_KV_HEADS * HEAD_DIM, MIXER_W, MIXER_W,
             MIXER_W, MIXER_W, MIXER_W, 3 * MIXER_W, MIXER_W, 2 * DN_HEADS, 2 * DN_HEADS)
    starts = np.concatenate([[0], np.cumsum(sizes)[:-1]])
    aq, ak, av, bu, bv, cq, ck, cv, dqkv, dz, db, da = (np.arange(n) + o for n, o in zip(sizes, starts))
    grp = GQA_HEADS // GQA_KV_HEADS
    expand = np.concatenate([np.arange(HEAD_DIM) + (hd // grp) * HEAD_DIM for hd in range(GQA_HEADS)])
    sw64 = _swap_halves(GQA_HEADS, HEAD_DIM)
    sw32 = _swap_halves(2 * DIFF_HEADS, DIFF_DIM)
    ak_x, av_x = ak[expand], av[expand]
    cols = [aq, aq[sw64], ak_x, ak_x[sw64], av_x, bu, bv, cq, cq[sw32], ck, ck[sw32], cv, dqkv, dz,
            db, da, np.full(128 - 4 * DN_HEADS, -1)]
    return np.concatenate(cols), expand, sw64


def _take_cols(w, cols):
    safe = np.where(cols < 0, 0, cols)
    return jnp.where(jnp.asarray(cols >= 0)[None, :], w[:, safe], 0.0)


def _rope_tables(l, lc):
    rows = l // GRID_W
    r_idx, c_idx = np.meshgrid(np.arange(rows), np.arange(GRID_W), indexing="ij")
    row_pos = jnp.asarray(r_idx.reshape(-1), F32)
    col_pos = jnp.asarray(c_idx.reshape(-1), F32)

    def table(dim, reps):
        n = dim // 4
        inv = jnp.power(ROPE_THETA, -jnp.arange(n, dtype=F32) / n)
        ang = jnp.concatenate([row_pos[:, None] * inv, col_pos[:, None] * inv], axis=-1)
        cos, sin = jnp.cos(ang), jnp.sin(ang)
        cos_t = jnp.tile(jnp.concatenate([cos, cos], axis=-1), (1, reps))
        sin_t = jnp.tile(jnp.concatenate([-sin, sin], axis=-1), (1, reps))
        return (jnp.concatenate([jnp.ones((lc, cos_t.shape[1]), F32), cos_t], axis=0),
                jnp.concatenate([jnp.zeros((lc, sin_t.shape[1]), F32), sin_t], axis=0))

    cos_a, sin_a = table(HEAD_DIM, GQA_HEADS)
    cos_d, sin_d = table(DIFF_DIM, 2 * DIFF_HEADS)
    return jnp.stack([cos_a, sin_a, cos_d, sin_d])


def kernel(x, c, ctx, c_ctx, w_ada, b_ada, norm_mix, norm_ffn, w_in, w_out, gqa_q_norm, gqa_k_norm, gmlp_v_norm, gmlp_w_s, gmlp_b_s, diff_lambda_q1, diff_lambda_k1, diff_lambda_q2, diff_lambda_k2, diff_subln, dn_conv_w, dn_a_log, dn_dt_bias, dn_out_norm, router_w, router_b, exp_w_up, exp_b_up, exp_w_down, exp_b_down, final_norm):
    b, l, d = x.shape
    lc = ctx.shape[1]
    depth = w_ada.shape[0]
    s = lc + l
    assert lc % TQ == 0 and l % TQ == 0 and (b * s) % MOE_TM == 0 and l % GRID_W == 0
    n_ctx_tiles = lc // TQ

    rows = -(-(b + 1) // 8) * 8
    c_all = jnp.zeros((rows, d), F32).at[:b].set(c).at[b].set(c_ctx)
    mod_t = _modulation(c_all, w_ada, b_ada).transpose(0, 2, 1, 3)

    cols, expand, sw64 = _in_columns()
    rope = _rope_tables(l, lc)
    lane = np.arange(MIXER_W)
    g64 = jnp.asarray((lane[:, None] // 64) == (lane[None, :] // 64), BF16)
    tri = jnp.asarray(np.arange(MOE_TM)[:, None] < np.arange(MOE_TM)[None, :], BF16)
    tril = jnp.asarray(np.arange(N_EXPERTS)[:, None] > np.arange(N_EXPERTS)[None, :], BF16)

    xs = jnp.concatenate([ctx, x], axis=1)
    ffn = None
    for layer in range(depth):
        lam_init = 0.8 - 0.6 * math.exp(-0.3 * layer)
        w_ext = _take_cols(w_in[layer], cols).astype(BF16)
        gq = jnp.tile(gqa_q_norm[layer], GQA_HEADS)
        gk = jnp.tile(gqa_k_norm[layer], GQA_HEADS)
        vecs = jnp.zeros((8, MIXER_W), F32).at[0].set(gq).at[1].set(gq[sw64]).at[2].set(gk).at[3].set(gk[sw64])
        vecs = vecs.at[4].set(gmlp_v_norm[layer])
        bst = jnp.repeat(gmlp_b_s[layer].T, GMLP_CH, axis=1)
        xs, (qa, ka, va, out_b, qd, kd, vd, dqkv, dz, dba) = _inproj(
            layer, xs, ffn, mod_t, norm_mix[layer][None, :], w_ext, vecs, g64, rope,
            gmlp_w_s[layer].astype(BF16), bst, n_ctx_tiles)

        out_a = _attention("gqa", qa, ka, va, n_ctx_tiles, lc)
        lam_p = jnp.stack([diff_lambda_q1[layer], diff_lambda_k1[layer], diff_lambda_q2[layer], diff_lambda_k2[layer]])
        sub = jnp.tile(diff_subln[layer], DIFF_HEADS)[None, :]
        out_c = _attention("diff", qd, kd, vd, n_ctx_tiles, lc, extra=(lam_p, sub, g64), lam_init=lam_init)

        avec = jnp.zeros((2, 128), F32)
        avec = avec.at[0, 2 * DN_HEADS:4 * DN_HEADS].set(dn_a_log[layer].reshape(-1))
        avec = avec.at[1, 2 * DN_HEADS:4 * DN_HEADS].set(dn_dt_bias[layer].reshape(-1))
        gq_, gk_, gv_, gkt, gbg, gbgt = _gdn_prep(dqkv, dba, dn_conv_w[layer], avec, g64, n_ctx_tiles)
        o_f, o_b = _gdn_scan(gq_, gk_, gv_, gkt, gbg, gbgt, lc)

        xs, f, logits_t = _outproj(
            layer, xs, out_a, out_b, out_c, o_f, o_b, dz, mod_t,
            w_out[layer].reshape(4, MIXER_W, d).astype(BF16), jnp.tile(dn_out_norm[layer], DN_HEADS)[None, :], g64,
            norm_ffn[layer][None, :], router_w[layer].T, jnp.broadcast_to(router_b[layer][:, None], (N_EXPERTS, 128)),
            n_ctx_tiles)

        pos_t, gate_t, cnt = _route(logits_t, tri)
        ffn = _moe(layer, f, pos_t, gate_t, cnt[:, :, 0], tril, exp_w_up, exp_b_up, exp_w_down, exp_b_down)
        ffn = ffn.reshape(b, s, d)

    return _final(depth - 1, xs, ffn, mod_t, final_norm[None, :], n_ctx_tiles, l)
```

```python
import functools
import math

import numpy as np
import jax
import jax.numpy as jnp
from jax import lax
from jax.experimental import pallas as pl
from jax.experimental.pallas import tpu as pltpu

F32 = jnp.float32
BF16 = jnp.bfloat16
HIGHEST = lax.Precision.HIGHEST

GRID_W = 64
NORM_EPS = 1e-6
ROPE_THETA = 10000.0
HEAD_DIM = 64
GQA_HEADS = 4
GQA_KV_HEADS = 2
GMLP_GROUPS = 4
GMLP_CH = 64
GMLP_CHUNK = 128
DIFF_HEADS = 4
DIFF_DIM = 32
DN_HEADS = 4
DN_DK = 64
DN_DV = 64
DN_CHUNK = 64
N_EXPERTS = 32
TOP_K = 4
SWIGLU_LIMIT = 7.0
SWIGLU_ALPHA = 1.702

MIXER_W = 256
TQ = 256
GDN_PAIR = 2 * DN_CHUNK
MOE_TM = 1024
MOE_BM = 512
MOE_TC = 512
ATT_KC = 1024
ATT_MW = 256
VMEM_LIMIT = 56 * 1024 * 1024

_SEGS = ("aq", "aqs", "ak", "aks", "av", "bu", "bv", "cq", "cqs", "ck", "cks", "cv")
OFF = {name: i * MIXER_W for i, name in enumerate(_SEGS)}
OFF["dqkv"] = len(_SEGS) * MIXER_W
OFF["dz"] = OFF["dqkv"] + 3 * MIXER_W
OFF["dba"] = OFF["dz"] + MIXER_W
W_EXT = OFF["dba"] + 128


def _dot(a, b):
    return jnp.dot(a, b, preferred_element_type=F32)


def _dot_nt(a, b):
    return lax.dot_general(a, b, (((1,), (1,)), ((), ())), preferred_element_type=F32)


def _dot_tn(a, b):
    return lax.dot_general(a, b, (((0,), (0,)), ((), ())), preferred_element_type=F32)


def _split3(x):
    hi = x.astype(BF16)
    r1 = x - hi.astype(F32)
    mid = r1.astype(BF16)
    lo = (r1 - mid.astype(F32)).astype(BF16)
    return hi, mid, lo


def _dot_sel_r(x, sel):
    hi, mid, lo = _split3(x)
    return _dot(hi, sel) + _dot(mid, sel) + _dot(lo, sel)


def _dot_sel_l(sel, x):
    hi, mid, lo = _split3(x)
    return _dot(sel, hi) + _dot(sel, mid) + _dot(sel, lo)


def _group_sumsq(x, g_same):
    x2 = x * x
    hi = x2.astype(BF16)
    lo = (x2 - hi.astype(F32)).astype(BF16)
    return _dot(hi, g_same) + _dot(lo, g_same)


def _onehot(cond):
    return jnp.where(cond, 1.0, 0.0).astype(BF16)


def _keep(cond, x):
    return jnp.where(cond, x.astype(F32), 0.0).astype(BF16)


def _silu(x):
    return x * jax.nn.sigmoid(x)


def _softplus(x):
    return jnp.maximum(x, 0.0) + jnp.log1p(jnp.exp(-jnp.abs(x)))


def _params(sem):
    return pltpu.CompilerParams(dimension_semantics=sem, vmem_limit_bytes=VMEM_LIMIT)


def _mod_kernel(c_ref, w_ref, b_ref, o_ref):
    s = _silu(c_ref[...])
    o_ref[0, 0] = jnp.dot(s, w_ref[0], precision=HIGHEST, preferred_element_type=F32) + b_ref[0]


def _modulation(c_all, w_ada, b_ada):
    depth, d, _ = w_ada.shape
    r = c_all.shape[0]
    return pl.pallas_call(
        _mod_kernel,
        grid=(depth, 6),
        in_specs=[
            pl.BlockSpec((r, d), lambda l, j: (0, 0)),
            pl.BlockSpec((1, d, d), lambda l, j: (l, 0, j)),
            pl.BlockSpec((1, 1, d), lambda l, j: (l, 0, j)),
        ],
        out_specs=pl.BlockSpec((1, 1, r, d), lambda l, j: (l, j, 0, 0)),
        out_shape=jax.ShapeDtypeStruct((depth, 6, r, d), F32),
        compiler_params=_params(("arbitrary", "arbitrary")),
        name="adaln_mod",
    )(c_all, w_ada, b_ada.reshape(depth, 1, 6 * d))


def _inproj_kernel(*refs, has_prev, scale_a, scale_d):
    if has_prev:
        x_ref, ffn_ref, modp_ref, refs = refs[0], refs[1], refs[2], refs[3:]
    else:
        x_ref, refs = refs[0], refs[1:]
    (mod_ref, gmix_ref, w_ref, vec_ref, g64_ref, rope_ref, ws_ref, bst_ref) = refs[:8]
    outs = refs[8:]
    if has_prev:
        xo_ref, outs = outs[0], outs[1:]
    (qa_ref, ka_ref, va_ref, ob_ref, qd_ref, kd_ref, vd_ref, dqkv_ref, dz_ref, dba_ref) = outs

    x = x_ref[0]
    if has_prev:
        x = x + modp_ref[0, 0, 5:6, :] * ffn_ref[0]
        xo_ref[0] = x
    ms = jnp.mean(x * x, axis=-1, keepdims=True)
    xn = x * lax.rsqrt(ms + NORM_EPS) * gmix_ref[...]
    h = xn * (1.0 + mod_ref[0, 0, 1:2, :]) + mod_ref[0, 0, 0:1, :]
    p = _dot(h.astype(BF16), w_ref[...])

    def seg(name, width=MIXER_W):
        return p[:, OFF[name]:OFF[name] + width]

    g64 = g64_ref[...]
    cos_a, sin_a, cos_d, sin_d = rope_ref[0], rope_ref[1], rope_ref[2], rope_ref[3]

    def norm_rope(x0, xs, gain, gain_s, scale):
        r = lax.rsqrt(_group_sumsq(x0, g64) * (1.0 / HEAD_DIM) + NORM_EPS)
        return ((x0 * r * gain) * cos_a + (xs * r * gain_s) * sin_a) * scale

    qa_ref[0] = norm_rope(seg("aq"), seg("aqs"), vec_ref[0:1, :], vec_ref[1:2, :], scale_a).astype(BF16)
    ka_ref[0] = norm_rope(seg("ak"), seg("aks"), vec_ref[2:3, :], vec_ref[3:4, :], 1.0).astype(BF16)
    va_ref[0] = seg("av").astype(BF16)

    u = jax.nn.gelu(seg("bu"))
    v = jax.nn.gelu(seg("bv"))
    vn = v * lax.rsqrt(jnp.mean(v * v, axis=-1, keepdims=True) + NORM_EPS) * vec_ref[4:5, :]
    lane_grp = lax.broadcasted_iota(jnp.int32, (GMLP_CHUNK, MIXER_W), 1) // GMLP_CH
    for ci in range(TQ // GMLP_CHUNK):
        rows = slice(ci * GMLP_CHUNK, (ci + 1) * GMLP_CHUNK)
        vc = vn[rows]
        sp = bst_ref[...]
        for g in range(GMLP_GROUPS):
            sp = sp + _dot(ws_ref[g], jnp.where(lane_grp == g, vc, 0.0).astype(BF16))
        ob_ref[0, rows, :] = (u[rows] * sp).astype(BF16)

    qd_ref[0] = ((seg("cq") * cos_d + seg("cqs") * sin_d) * scale_d).astype(BF16)
    kd_ref[0] = (seg("ck") * cos_d + seg("cks") * sin_d).astype(BF16)
    vd_ref[0] = seg("cv").astype(BF16)

    dqkv_ref[0] = seg("dqkv", 3 * MIXER_W)
    dz_ref[0] = seg("dz")
    dba_ref[0] = seg("dba", 128)


def _inproj(layer, x, prev, mod_t, gmix, w_ext, vecs, g64, rope, ws, bst, n_ctx_tiles):
    b, s, d = x.shape
    nt = s // TQ
    ctx_row = b

    def mod_map(l):
        return lambda i, bb: (l, jnp.where(i < n_ctx_tiles, ctx_row, bb), 0, 0)

    tok = lambda i, bb: (bb, i, 0)
    const2 = lambda i, bb: (0, 0)
    const3 = lambda i, bb: (0, 0, 0)
    in_specs = [pl.BlockSpec((1, TQ, d), tok)]
    args = [x]
    if prev is not None:
        in_specs += [pl.BlockSpec((1, TQ, d), tok), pl.BlockSpec((1, 1, 6, d), mod_map(layer - 1))]
        args += [prev, mod_t]
    in_specs += [
        pl.BlockSpec((1, 1, 6, d), mod_map(layer)),
        pl.BlockSpec((1, d), const2),
        pl.BlockSpec((d, W_EXT), const2),
        pl.BlockSpec((8, MIXER_W), const2),
        pl.BlockSpec((MIXER_W, MIXER_W), const2),
        pl.BlockSpec((4, TQ, MIXER_W), lambda i, bb: (0, i, 0)),
        pl.BlockSpec((GMLP_GROUPS, GMLP_CHUNK, GMLP_CHUNK), const3),
        pl.BlockSpec((GMLP_CHUNK, MIXER_W), const2),
    ]
    args += [mod_t, gmix, w_ext, vecs, g64, rope, ws, bst]
    bf = lambda w: jax.ShapeDtypeStruct((b, s, w), BF16)
    ff = lambda w: jax.ShapeDtypeStruct((b, s, w), F32)
    out_shape = [bf(MIXER_W)] * 7 + [ff(3 * MIXER_W), ff(MIXER_W), ff(128)]
    out_specs = [pl.BlockSpec((1, TQ, MIXER_W), tok)] * 7 + [
        pl.BlockSpec((1, TQ, 3 * MIXER_W), tok), pl.BlockSpec((1, TQ, MIXER_W), tok), pl.BlockSpec((1, TQ, 128), tok)]
    if prev is not None:
        out_shape = [ff(d)] + out_shape
        out_specs = [pl.BlockSpec((1, TQ, d), tok)] + out_specs
    outs = pl.pallas_call(
        functools.partial(_inproj_kernel, has_prev=prev is not None,
                          scale_a=HEAD_DIM ** -0.5, scale_d=DIFF_DIM ** -0.5),
        grid=(nt, b), in_specs=in_specs, out_specs=out_specs, out_shape=out_shape,
        compiler_params=_params(("arbitrary", "arbitrary")), name="in_proj",
    )(*args)
    if prev is not None:
        return outs[0], outs[1:]
    return x, outs


def _ones_outside(v, keep_lanes):
    keep = jnp.where(keep_lanes, 1.0, 0.0)
    return v * keep.astype(BF16) + (1.0 - keep).astype(BF16)


def _key_chunks(n_keys):
    cuts = list(range(0, n_keys, ATT_KC)) + [n_keys]
    return list(zip(cuts[:-1], cuts[1:]))


def _attend_groups(qms, k_ref, v1_ref, n_keys, s_scr):
    chunks = _key_chunks(n_keys)

    def scores(g, lo, hi, mx):
        s_c = _dot_nt(qms[g], k_ref[0, lo:hi, :])
        s_scr[g % 2, :, lo:hi] = s_c
        for a in range(lo, hi, ATT_MW):
            piece = s_c[:, a - lo:a - lo + ATT_MW]
            mx = piece if mx is None else jnp.maximum(mx, piece)
        return mx

    def values(g, lo, hi, m, acc):
        e = jnp.exp(s_scr[g % 2, :, lo:hi] - m).astype(BF16)
        part = _dot(e, v1_ref[g, lo:hi, :])
        return part if acc is None else acc + part

    mx = None
    for lo, hi in chunks:
        mx = scores(0, lo, hi, mx)
    outs = []
    for g in range(len(qms)):
        m = jnp.max(mx, axis=-1, keepdims=True)
        mx, acc = None, None
        for lo, hi in chunks:
            if g + 1 < len(qms):
                mx = scores(g + 1, lo, hi, mx)
            acc = values(g, lo, hi, m, acc)
        outs.append(acc / jnp.concatenate([acc[:, 128:], acc[:, :128]], axis=1))
    return outs


def _gqa_kernel(q_ref, k_ref, v_ref, o_ref, v1_scr, s_scr, *, n_ctx_tiles, lc):
    i = pl.program_id(1)
    lane_head = lax.broadcasted_iota(jnp.int32, (TQ, MIXER_W), 1) // HEAD_DIM
    lane_half = lax.broadcasted_iota(jnp.int32, (1, MIXER_W), 1) // 128
    pairs = GQA_HEADS // 2

    @pl.when(i == 0)
    def _():
        for pair in range(pairs):
            v1_scr[pair] = _ones_outside(v_ref[0], lane_half == pair)

    def attend(n_keys):
        q = q_ref[0]
        qms = [jnp.concatenate([_keep(lane_head == 2 * p, q), _keep(lane_head == 2 * p + 1, q)], axis=0)
               for p in range(pairs)]
        outs = _attend_groups(qms, k_ref, v1_scr, n_keys, s_scr)
        acc = jnp.zeros((TQ, MIXER_W), F32)
        for p in range(pairs):
            acc = jnp.where(lane_head == 2 * p, outs[p][:TQ], jnp.where(lane_head == 2 * p + 1, outs[p][TQ:], acc))
        o_ref[0] = acc.astype(BF16)

    @pl.when(i < n_ctx_tiles)
    def _():
        attend(lc)

    @pl.when(i >= n_ctx_tiles)
    def _():
        attend(k_ref.shape[1])


def _diff_kernel(q_ref, k_ref, v_ref, lam_ref, sub_ref, g64_ref, o_ref, v1_scr, s_scr, *, n_ctx_tiles, lc, lam_init):
    i = pl.program_id(1)
    lane = lax.broadcasted_iota(jnp.int32, (TQ, MIXER_W), 1)
    lane_head = lane // (2 * DIFF_DIM)
    lane_map = lane // DIFF_DIM
    head_of_lane = lax.broadcasted_iota(jnp.int32, (1, MIXER_W), 1) // (2 * DIFF_DIM)
    lp = lam_ref[...]
    lam = (jnp.exp(jnp.sum(lp[0:1] * lp[1:2], axis=-1, keepdims=True))
           - jnp.exp(jnp.sum(lp[2:3] * lp[3:4], axis=-1, keepdims=True)) + lam_init)

    @pl.when(i == 0)
    def _():
        for hd in range(DIFF_HEADS):
            v1_scr[hd] = _ones_outside(v_ref[0], head_of_lane == hd)

    def attend(n_keys):
        q = q_ref[0]
        qms = [jnp.concatenate([_keep(lane_map == 2 * hd, q), _keep(lane_map == 2 * hd + 1, q)], axis=0)
               for hd in range(DIFF_HEADS)]
        outs = _attend_groups(qms, k_ref, v1_scr, n_keys, s_scr)
        acc = jnp.zeros((TQ, MIXER_W), F32)
        for hd in range(DIFF_HEADS):
            acc = jnp.where(lane_head == hd, outs[hd][:TQ] - lam * outs[hd][TQ:], acc)
        r = lax.rsqrt(_group_sumsq(acc, g64_ref[...]) * (1.0 / (2 * DIFF_DIM)) + NORM_EPS)
        o_ref[0] = (acc * r * sub_ref[...] * (1.0 - lam_init)).astype(BF16)

    @pl.when(i < n_ctx_tiles)
    def _():
        attend(lc)

    @pl.when(i >= n_ctx_tiles)
    def _():
        attend(k_ref.shape[1])


def _attention(kind, q, k, v, n_ctx_tiles, lc, extra=(), lam_init=0.0):
    b, s, w = q.shape
    nt = s // TQ
    tok = lambda bb, i: (bb, i, 0)
    row = lambda bb, i: (bb, 0, 0)
    in_specs = [pl.BlockSpec((1, TQ, w), tok), pl.BlockSpec((1, s, w), row), pl.BlockSpec((1, s, w), row)]
    if kind == "gqa":
        groups = GQA_HEADS // 2
        body = functools.partial(_gqa_kernel, n_ctx_tiles=n_ctx_tiles, lc=lc)
    else:
        groups = DIFF_HEADS
        body = functools.partial(_diff_kernel, n_ctx_tiles=n_ctx_tiles, lc=lc, lam_init=lam_init)
        in_specs += [pl.BlockSpec(e.shape, lambda bb, i: (0, 0)) for e in extra]
    return pl.pallas_call(
        body, grid=(b, nt), in_specs=in_specs, out_specs=pl.BlockSpec((1, TQ, w), tok),
        out_shape=jax.ShapeDtypeStruct((b, s, w), BF16),
        scratch_shapes=[pltpu.VMEM((groups, s, w), BF16), pltpu.VMEM((2, 2 * TQ, s), F32)],
        compiler_params=_params(("arbitrary", "arbitrary")), name=kind + "_attention",
    )(q, k, v, *extra)


def _gdn_prep_kernel(x_ref, xp_ref, xn_ref, ba_ref, cw_ref, av_ref, g64_ref,
                     q_ref, k_ref, v_ref, kt_ref, bg_ref, bgt_ref, *, n_ctx_tiles, n_tiles):
    i = pl.program_id(0)
    x = x_ref[0]
    has_prev = jnp.where((i != 0) & (i != n_ctx_tiles), 1.0, 0.0)
    has_next = jnp.where((i != n_ctx_tiles - 1) & (i != n_tiles - 1), 1.0, 0.0)
    row = lax.broadcasted_iota(jnp.int32, x.shape, 0)
    x_m1 = jnp.where(row == 0, xp_ref[0, 7:8, :] * has_prev, pltpu.roll(x, 1, 0))
    x_p1 = jnp.where(row == TQ - 1, xn_ref[0, 0:1, :] * has_next, pltpu.roll(x, TQ - 1, 0))
    y = _silu(cw_ref[0:1, :] * x_m1 + cw_ref[1:2, :] * x + cw_ref[2:3, :] * x_p1)
    q, k, v = y[:, :MIXER_W], y[:, MIXER_W:2 * MIXER_W], y[:, 2 * MIXER_W:]
    g64 = g64_ref[...]
    qn = q * lax.rsqrt(_group_sumsq(q, g64) + NORM_EPS) * (DN_DK ** -0.5)
    kn = k * lax.rsqrt(_group_sumsq(k, g64) + NORM_EPS)
    q_ref[0] = qn.astype(BF16)
    k_ref[0] = kn.astype(BF16)
    v_ref[0] = v.astype(BF16)
    kt_ref[0] = kn.T.astype(BF16)
    ba = ba_ref[0]
    lane = lax.broadcasted_iota(jnp.int32, ba.shape, 1)
    beta = jax.nn.sigmoid(ba)
    g = -jnp.exp(av_ref[0:1, :]) * _softplus(ba + av_ref[1:2, :])
    bg = jnp.where(lane < 2 * DN_HEADS, beta, jnp.where(lane < 4 * DN_HEADS, g, 0.0))
    bg_ref[0] = bg
    bgt_ref[0] = bg.T


def _gdn_prep(dqkv, dba, conv_w, avec, g64, n_ctx_tiles):
    b, s, w3 = dqkv.shape
    nt = s // TQ
    nb8 = s // 8
    tok = lambda i, bb: (bb, i, 0)
    tokt = lambda i, bb: (bb, 0, i)
    const2 = lambda i, bb: (0, 0)
    outs = pl.pallas_call(
        functools.partial(_gdn_prep_kernel, n_ctx_tiles=n_ctx_tiles, n_tiles=nt),
        grid=(nt, b),
        in_specs=[
            pl.BlockSpec((1, TQ, w3), tok),
            pl.BlockSpec((1, 8, w3), lambda i, bb: (bb, jnp.maximum(i * (TQ // 8) - 1, 0), 0)),
            pl.BlockSpec((1, 8, w3), lambda i, bb: (bb, jnp.minimum((i + 1) * (TQ // 8), nb8 - 1), 0)),
            pl.BlockSpec((1, TQ, 128), tok),
            pl.BlockSpec((3, w3), const2),
            pl.BlockSpec((2, 128), const2),
            pl.BlockSpec((MIXER_W, MIXER_W), const2),
        ],
        out_specs=[pl.BlockSpec((1, TQ, MIXER_W), tok)] * 3 + [
            pl.BlockSpec((1, MIXER_W, TQ), tokt), pl.BlockSpec((1, TQ, 128), tok), pl.BlockSpec((1, 128, TQ), tokt)],
        out_shape=[jax.ShapeDtypeStruct((b, s, MIXER_W), BF16)] * 3 + [
            jax.ShapeDtypeStruct((b, MIXER_W, s), BF16), jax.ShapeDtypeStruct((b, s, 128), F32),
            jax.ShapeDtypeStruct((b, 128, s), F32)],
        compiler_params=_params(("arbitrary", "arbitrary")), name="gdn_prep",
    )(dqkv, dqkv, dqkv, dba, conv_w, avec, g64)
    return outs


def _gdn_scan_kernel(qf, kf, vf, ktf, bgf, bgtf, qb, kb, vb, ktb, bgb, bgtb, of_ref, ob_ref, st_ref):
    @pl.when(pl.program_id(1) == 0)
    def _():
        st_ref[...] = jnp.zeros_like(st_ref)

    pp, cc, w = GDN_PAIR, DN_CHUNK, MIXER_W
    dirs = (0, 1)
    heads = range(DN_HEADS)
    chains = [(d, hd) for d in dirs for hd in heads]
    q_refs, k_refs, v_refs, kt_refs = (qf, qb), (kf, kb), (vf, vb), (ktf, ktb)
    bg_refs, bgt_refs, o_refs = (bgf, bgb), (bgtf, bgtb), (of_ref, ob_ref)

    ii = lax.broadcasted_iota(jnp.int32, (pp, pp), 0)
    jj = lax.broadcasted_iota(jnp.int32, (pp, pp), 1)
    same = (ii // cc) == (jj // cc)
    incl = (same & (jj <= ii), same & (jj >= ii))
    strict = (same & (jj < ii), same & (jj > ii))
    incl_b = [_onehot(m) for m in incl]
    incl_tb = [_onehot(same & (ii <= jj)), _onehot(same & (ii >= jj))]
    eye = jnp.where(ii == jj, 1.0, 0.0)
    merge_masks = [((ii // (2 * sz)) == (jj // (2 * sz))) & ((ii // sz) != (jj // sz))
                   for sz in (2 ** e for e in range(int(math.log2(cc))))]

    src = lax.broadcasted_iota(jnp.int32, (128, w), 0)
    lane_w = lax.broadcasted_iota(jnp.int32, (128, w), 1)
    src2 = lax.broadcasted_iota(jnp.int32, (128, DN_HEADS * pp), 0)
    lane2 = lax.broadcasted_iota(jnp.int32, (128, DN_HEADS * pp), 1)
    bg = [r[0] for r in bg_refs]
    beta_x = [_dot_sel_r(bg[d], _onehot(src == DN_HEADS * d + lane_w // DN_DV)) for d in dirs]
    g_x = [_dot_sel_r(bg[d], _onehot(src == 2 * DN_HEADS + DN_HEADS * d + lane_w // DN_DV)) for d in dirs]
    g_x2 = [_dot_sel_r(bg[d], _onehot(src2 == 2 * DN_HEADS + DN_HEADS * d + lane2 // pp)) for d in dirs]
    cg_rows = [_dot_sel_r(bgt_refs[d][0], incl_tb[d]) for d in dirs]
    cg_x = [_dot_sel_l(incl_b[d], g_x[d]) for d in dirs]
    cg_x2 = [_dot_sel_l(incl_b[d], g_x2[d]) for d in dirs]

    q = [r[0] for r in q_refs]
    k = [r[0] for r in k_refs]
    kf32 = [t.astype(F32) for t in k]
    e_cg = [jnp.exp(t) for t in cg_x]
    rhs_v = [v_refs[d][0].astype(F32) * beta_x[d] for d in dirs]
    rhs_k = [kf32[d] * (beta_x[d] * e_cg[d]) for d in dirs]
    k_beta = [kf32[d] * beta_x[d] for d in dirs]
    lane_head = lax.broadcasted_iota(jnp.int32, (pp, w), 1) // DN_DV

    kk = [_dot_nt(jnp.where(lane_head == hd, k_beta[d], 0.0).astype(BF16), k[d]) for d, hd in chains]
    qk = [_dot_nt(_keep(lane_head == hd, q[d]), k[d]) for d, hd in chains]
    a, qkd = [], []
    for ci, (d, hd) in enumerate(chains):
        gl = 2 * DN_HEADS + DN_HEADS * d + hd
        diff = jnp.where(incl[d], cg_x2[d][:, hd * pp:(hd + 1) * pp] - cg_rows[d][gl:gl + 1, :], 0.0)
        decay = jnp.where(incl[d], jnp.exp(diff), 0.0)
        a.append(jnp.where(strict[d], kk[ci] * decay, 0.0))
        qkd.append((qk[ci] * decay).astype(BF16))

    t_inv = [eye - jnp.where(merge_masks[0], a_c, 0.0) for a_c in a]
    for mask in merge_masks[1:]:
        tb = [t.astype(BF16) for t in t_inv]
        lm = [_dot(jnp.where(mask, a_c, 0.0).astype(BF16), tb_c).astype(BF16) for a_c, tb_c in zip(a, tb)]
        t_inv = [t - _dot(tb_c, lm_c) for t, tb_c, lm_c in zip(t_inv, tb, lm)]
    tb = [t.astype(BF16) for t in t_inv]
    u_part = [_dot(tb[ci], jnp.where(lane_head == hd, rhs_v[d], 0.0).astype(BF16)) for ci, (d, hd) in enumerate(chains)]
    w_part = [_dot(tb[ci], jnp.where(lane_head == hd, rhs_k[d], 0.0).astype(BF16)) for ci, (d, hd) in enumerate(chains)]
    u_all = [sum(u_part[d * DN_HEADS + hd] for hd in heads) for d in dirs]
    w_all = [sum(w_part[d * DN_HEADS + hd] for hd in heads) for d in dirs]

    st = [st_ref[d] for d in dirs]
    blk = (lax.broadcasted_iota(jnp.int32, (w, w), 0) // DN_DK) == (lax.broadcasted_iota(jnp.int32, (w, w), 1) // DN_DV)
    kt = [r[0] for r in kt_refs]
    order = (((0, cc), (cc, pp)), ((cc, pp), (0, cc)))
    zeros_c = jnp.zeros((cc, w), F32)
    lane_head_c = lax.broadcasted_iota(jnp.int32, (cc, w), 1) // DN_DV

    def place(lo, t):
        return jnp.concatenate([t, zeros_c] if lo == 0 else [zeros_c, t], axis=0)

    nv_acc = [None, None]
    for step in range(2):
        lo = [order[d][step][0] for d in dirs]
        rows = [slice(*order[d][step]) for d in dirs]
        stb = [t.astype(BF16) for t in st]
        w_s = [_dot(w_all[d][rows[d]].astype(BF16), stb[d]) for d in dirs]
        q_s = [_dot(q[d][rows[d]], stb[d]) for d in dirs]
        nv = [u_all[d][rows[d]] - w_s[d] for d in dirs]
        for d in dirs:
            full = place(lo[d], nv[d])
            nv_acc[d] = full if nv_acc[d] is None else nv_acc[d] + full
        nvb = [t.astype(BF16) for t in nv_acc]
        intra = [_dot(qkd[ci][rows[d]], nvb[d]) for ci, (d, hd) in enumerate(chains)]
        last = [order[0][step][1] - 1, order[1][step][0]]
        g_end = [cg_x[d][last[d]:last[d] + 1, :] for d in dirs]
        nvs = [place(lo[d], nv[d] * jnp.exp(g_end[d] - cg_x[d][rows[d]])).astype(BF16) for d in dirs]
        upd = [_dot(kt[d], nvs[d]) for d in dirs]
        for d in dirs:
            o = e_cg[d][rows[d]] * q_s[d]
            for hd in heads:
                o = o + jnp.where(lane_head_c == hd, intra[d * DN_HEADS + hd], 0.0)
            o_refs[d][0, rows[d], :] = o
            st[d] = st[d] * jnp.exp(g_end[d]) + jnp.where(blk, upd[d], 0.0)
    for d in dirs:
        st_ref[d] = st[d]


def _gdn_scan(q, k, v, kt, bg, bgt, lc):
    b, s, w = q.shape
    n_pairs = s // GDN_PAIR
    ncp = lc // GDN_PAIR

    def fwd(bb, i):
        return i

    def bwd(bb, i):
        return jnp.where(i < ncp, ncp - 1 - i, n_pairs - 1 + ncp - i)

    def specs(pos):
        tok = lambda bb, i: (bb, pos(bb, i), 0)
        tokt = lambda bb, i: (bb, 0, pos(bb, i))
        return [pl.BlockSpec((1, GDN_PAIR, w), tok)] * 3 + [
            pl.BlockSpec((1, w, GDN_PAIR), tokt), pl.BlockSpec((1, GDN_PAIR, 128), tok),
            pl.BlockSpec((1, 128, GDN_PAIR), tokt)]

    return pl.pallas_call(
        _gdn_scan_kernel, grid=(b, n_pairs),
        in_specs=specs(fwd) + specs(bwd),
        out_specs=[pl.BlockSpec((1, GDN_PAIR, w), lambda bb, i: (bb, fwd(bb, i), 0)),
                   pl.BlockSpec((1, GDN_PAIR, w), lambda bb, i: (bb, bwd(bb, i), 0))],
        out_shape=[jax.ShapeDtypeStruct((b, s, w), F32)] * 2,
        scratch_shapes=[pltpu.VMEM((2, w, w), F32)],
        compiler_params=_params(("arbitrary", "arbitrary")), name="gdn_scan",
    )(q, k, v, kt, bg, bgt, q, k, v, kt, bg, bgt)


def _outproj_kernel(x_ref, oa_ref, ob_ref, oc_ref, of_ref, obw_ref, dz_ref, mod_ref, w_ref, gout_ref,
                    g64_ref, gffn_ref, wr_ref, br_ref, xo_ref, f_ref, lg_ref):
    o = of_ref[0] + obw_ref[0]
    r = lax.rsqrt(_group_sumsq(o, g64_ref[...]) * (1.0 / DN_DV) + NORM_EPS)
    od = (o * r * gout_ref[...] * _silu(dz_ref[0])).astype(BF16)
    y = (_dot(oa_ref[0], w_ref[0]) + _dot(ob_ref[0], w_ref[1]) + _dot(oc_ref[0], w_ref[2]) + _dot(od, w_ref[3]))
    x = x_ref[0] + mod_ref[0, 0, 2:3, :] * y
    xo_ref[0] = x
    ms = jnp.mean(x * x, axis=-1, keepdims=True)
    f = (x * lax.rsqrt(ms + NORM_EPS) * gffn_ref[...]) * (1.0 + mod_ref[0, 0, 4:5, :]) + mod_ref[0, 0, 3:4, :]
    f_ref[...] = f
    lg_ref[...] = lax.dot_general(wr_ref[...], f, (((1,), (1,)), ((), ())), precision=HIGHEST,
                                  preferred_element_type=F32) + br_ref[:, 0:1]


def _outproj(layer, x, oa, ob, oc, o_f, o_b, dz, mod_t, w_out4, gout, g64, gffn, wr_t, br, n_ctx_tiles):
    b, s, d = x.shape
    nt = s // TQ
    ctx_row = b
    tok = lambda i, bb: (bb, i, 0)
    const2 = lambda i, bb: (0, 0)
    slab = pl.BlockSpec((1, TQ, MIXER_W), tok)
    return pl.pallas_call(
        _outproj_kernel, grid=(nt, b),
        in_specs=[pl.BlockSpec((1, TQ, d), tok), slab, slab, slab, slab, slab, slab,
                  pl.BlockSpec((1, 1, 6, d), lambda i, bb: (layer, jnp.where(i < n_ctx_tiles, ctx_row, bb), 0, 0)),
                  pl.BlockSpec((4, MIXER_W, d), lambda i, bb: (0, 0, 0)),
                  pl.BlockSpec((1, MIXER_W), const2), pl.BlockSpec((MIXER_W, MIXER_W), const2),
                  pl.BlockSpec((1, d), const2), pl.BlockSpec((N_EXPERTS, d), const2),
                  pl.BlockSpec((N_EXPERTS, 128), const2)],
        out_specs=[pl.BlockSpec((1, TQ, d), tok), pl.BlockSpec((TQ, d), lambda i, bb: (bb * nt + i, 0)),
                   pl.BlockSpec((N_EXPERTS, TQ), lambda i, bb: (0, bb * nt + i))],
        out_shape=[jax.ShapeDtypeStruct((b, s, d), F32), jax.ShapeDtypeStruct((b * s, d), F32),
                   jax.ShapeDtypeStruct((N_EXPERTS, b * s), F32)],
        compiler_params=_params(("arbitrary", "arbitrary")), name="out_proj",
    )(x, oa, ob, oc, o_f, o_b, dz, mod_t, w_out4, gout, g64, gffn, wr_t, br)


def _route_kernel(lg_ref, tri_ref, pos_ref, gate_ref, cnt_ref):
    x = lg_ref[...]
    e_iota = lax.broadcasted_iota(jnp.int32, x.shape, 0).astype(F32)
    work = x
    chosen = jnp.zeros(x.shape, F32)
    top = None
    den = None
    for kk in range(TOP_K):
        m = jnp.max(work, axis=0, keepdims=True)
        idx = jnp.min(jnp.where(work == m, e_iota, float(N_EXPERTS)), axis=0, keepdims=True)
        pick = e_iota == idx
        chosen = jnp.where(pick, 1.0, chosen)
        if kk == 0:
            top = m
            den = jnp.ones_like(m)
        else:
            den = den + jnp.exp(m - top)
        work = jnp.where(pick, -jnp.inf, work)
    sel = chosen > 0.5
    gate_ref[0] = jnp.where(sel, jnp.exp(x - top) / den, 0.0)
    rank = _dot(chosen.astype(BF16), tri_ref[...])
    pos_ref[0] = jnp.where(sel, rank.astype(jnp.int32), -1)
    cnt = jnp.sum(chosen, axis=1, keepdims=True).astype(jnp.int32)
    cnt_ref[0] = jnp.broadcast_to(cnt, cnt_ref.shape[1:])


def _route(logits_t, tri):
    n_exp, t = logits_t.shape
    n_tiles = t // MOE_TM
    return pl.pallas_call(
        _route_kernel, grid=(n_tiles,),
        in_specs=[pl.BlockSpec((n_exp, MOE_TM), lambda i: (0, i)), pl.BlockSpec((MOE_TM, MOE_TM), lambda i: (0, 0))],
        out_specs=[pl.BlockSpec((1, n_exp, MOE_TM), lambda i: (i, 0, 0)),
                   pl.BlockSpec((1, n_exp, MOE_TM), lambda i: (i, 0, 0)),
                   pl.BlockSpec((1, n_exp, 128), lambda i: (i, 0, 0))],
        out_shape=[jax.ShapeDtypeStruct((n_tiles, n_exp, MOE_TM), jnp.int32),
                   jax.ShapeDtypeStruct((n_tiles, n_exp, MOE_TM), F32),
                   jax.ShapeDtypeStruct((n_tiles, n_exp, 128), jnp.int32)],
        compiler_params=_params(("arbitrary",)), name="route",
    )(logits_t, tri)


def _slots_kernel(pos_ref, gate_ref, base_ref, tril_ref, dest_ref, gk_ref):
    pos = pos_ref[0]
    chosen = pos >= 0
    slot = (base_ref[0][:, 0:1] + pos).astype(F32)
    choice = _dot(tril_ref[...], _onehot(chosen))
    gate = gate_ref[0]
    pad = jnp.zeros((8 - TOP_K, pos.shape[1]), F32)
    d_rows, g_rows = [], []
    for kk in range(TOP_K):
        mine = chosen & (choice == float(kk))
        d_rows.append(jnp.sum(jnp.where(mine, slot, 0.0), axis=0, keepdims=True))
        g_rows.append(jnp.sum(jnp.where(mine, gate, 0.0), axis=0, keepdims=True))
    dest_ref[0] = jnp.concatenate(d_rows + [pad], axis=0).astype(jnp.int32)
    gk_ref[0] = jnp.concatenate(g_rows + [pad], axis=0)


def _slots(pos_t, gate_t, base_b, tril):
    n_tiles, n_exp, tm = pos_t.shape
    tile = lambda i: (i, 0, 0)
    return pl.pallas_call(
        _slots_kernel, grid=(n_tiles,),
        in_specs=[pl.BlockSpec((1, n_exp, tm), tile), pl.BlockSpec((1, n_exp, tm), tile),
                  pl.BlockSpec((1, n_exp, 128), tile), pl.BlockSpec((n_exp, n_exp), lambda i: (0, 0))],
        out_specs=[pl.BlockSpec((1, 8, tm), tile), pl.BlockSpec((1, 8, tm), tile)],
        out_shape=[jax.ShapeDtypeStruct((n_tiles, 8, tm), jnp.int32), jax.ShapeDtypeStruct((n_tiles, 8, tm), F32)],
        compiler_params=_params(("arbitrary",)), name="moe_slots",
    )(pos_t, gate_t, base_b, tril)


def _dispatch_kernel(pad_ref, f_ref, dest_ref, xs_ref, zbuf, zsem, sem):
    tm = f_ref.shape[0]
    n_blocks = xs_ref.shape[0] // MOE_BM

    @pl.when(pl.program_id(0) == 0)
    def _():
        zbuf[...] = jnp.zeros_like(zbuf)
        live = pad_ref[2 * N_EXPERTS]

        def zero_block(row):
            return pltpu.make_async_copy(zbuf, xs_ref.at[pl.ds(pl.multiple_of(row, MOE_BM), MOE_BM)], zsem)

        for wait in (False, True):
            for e in range(N_EXPERTS):
                for cond, row in ((pad_ref[N_EXPERTS + e] > 0, pad_ref[e]), (live + e < n_blocks, (live + e) * MOE_BM)):
                    @pl.when(cond)
                    def _():
                        zero_block(row).wait() if wait else zero_block(row).start()

    def rows(tok, carry):
        for kk in range(TOP_K):
            pltpu.make_async_copy(f_ref.at[pl.ds(tok, 1)], xs_ref.at[pl.ds(dest_ref[0, kk, tok], 1)],
                                  sem).start(priority=kk % 2)
        return carry

    lax.fori_loop(0, tm, rows, 0, unroll=8)
    for kk in range(TOP_K):
        pltpu.make_async_copy(f_ref, xs_ref.at[pl.ds(0, tm)], sem).wait()


def _dispatch(pad_info, f, dest, n_slots):
    t, d = f.shape
    n_tiles, _, tm = dest.shape
    grid_spec = pltpu.PrefetchScalarGridSpec(
        num_scalar_prefetch=1, grid=(n_tiles,),
        in_specs=[pl.BlockSpec((tm, d), lambda i, p: (i, 0)),
                  pl.BlockSpec((1, 8, tm), lambda i, p: (i, 0, 0), memory_space=pltpu.SMEM)],
        out_specs=pl.BlockSpec(memory_space=pl.ANY),
        scratch_shapes=[pltpu.VMEM((MOE_BM, d), F32), pltpu.SemaphoreType.DMA(()), pltpu.SemaphoreType.DMA(())],
    )
    return pl.pallas_call(
        _dispatch_kernel, grid_spec=grid_spec, out_shape=jax.ShapeDtypeStruct((n_slots, d), F32),
        compiler_params=_params(("arbitrary",)), name="moe_dispatch",
    )(pad_info, f, dest)


def _experts_kernel(meta_ref, xs_ref, wu_ref, bu_ref, wd_ref, bd_ref, ys_ref, wu_bf, wd_bf):
    i = pl.program_id(0)
    nb = pl.num_programs(0)
    live = i < meta_ref[2 * nb]
    de = wd_bf.shape[0]

    @pl.when(live & (meta_ref[nb + i] > 0))
    def _():
        wu_bf[...] = wu_ref[0, 0].astype(BF16)
        wd_bf[...] = wd_ref[0, 0].astype(BF16)

    @pl.when(live)
    def _():
        hgu = _dot(xs_ref[...].astype(BF16), wu_bf[...]) + bu_ref[0, 0]
        gate = jnp.minimum(hgu[:, :de], SWIGLU_LIMIT)
        up = jnp.clip(hgu[:, de:], -SWIGLU_LIMIT, SWIGLU_LIMIT)
        hid = gate * jax.nn.sigmoid(SWIGLU_ALPHA * gate) * (up + 1.0)
        ys_ref[...] = _dot(hid.astype(BF16), wd_bf[...]) + bd_ref[0, 0]

    @pl.when(jnp.logical_not(live))
    def _():
        ys_ref[...] = jnp.zeros_like(ys_ref)


def _experts(layer, meta, xs, w_up, b_up, w_down, b_down):
    n_slots, d = xs.shape
    nb = n_slots // MOE_BM
    depth, n_exp, _, de2 = w_up.shape
    de = de2 // 2
    blk = lambda i, m: (jnp.minimum(i, m[2 * nb] - 1), 0)
    exp = lambda i, m: (layer, m[i], 0, 0)
    grid_spec = pltpu.PrefetchScalarGridSpec(
        num_scalar_prefetch=1, grid=(nb,),
        in_specs=[pl.BlockSpec((MOE_BM, d), blk),
                  pl.BlockSpec((1, 1, d, de2), exp), pl.BlockSpec((1, 1, 1, de2), exp),
                  pl.BlockSpec((1, 1, de, d), exp), pl.BlockSpec((1, 1, 1, d), exp)],
        out_specs=pl.BlockSpec((MOE_BM, d), lambda i, m: (i, 0)),
        scratch_shapes=[pltpu.VMEM((d, de2), BF16), pltpu.VMEM((de, d), BF16)],
    )
    return pl.pallas_call(
        _experts_kernel, grid_spec=grid_spec, out_shape=jax.ShapeDtypeStruct((n_slots, d), F32),
        compiler_params=_params(("arbitrary",)), name="moe_experts",
    )(meta, xs, w_up, b_up.reshape(depth, n_exp, 1, de2), w_down, b_down.reshape(depth, n_exp, 1, d))


def _combine_kernel(dest_ref, dest_next_ref, gk_ref, ys_ref, o_ref, ybuf, sems):
    tc, d = o_ref.shape
    i = pl.program_id(0)
    n = pl.num_programs(0)
    cur = i % 2

    def fetch(idx_ref, buf):
        def rows(tok, carry):
            for kk in range(TOP_K):
                pltpu.make_async_copy(ys_ref.at[pl.ds(idx_ref[0, kk, tok], 1)], ybuf.at[buf, kk, pl.ds(tok, 1)],
                                      sems.at[buf]).start(priority=kk % 2)
            return carry

        lax.fori_loop(0, tc, rows, 0, unroll=8)

    @pl.when(i == 0)
    def _():
        fetch(dest_ref, 0)

    @pl.when(i + 1 < n)
    def _():
        fetch(dest_next_ref, 1 - cur)

    for kk in range(TOP_K):
        pltpu.make_async_copy(ys_ref.at[pl.ds(0, tc)], ybuf.at[cur, kk], sems.at[cur]).wait()
    ybuf = ybuf.at[cur]

    hi, mid, lo = _split3(gk_ref[0])
    pick = lax.broadcasted_iota(jnp.int32, (8, 128), 0)
    cols = []
    for kk in range(TOP_K):
        sel = _onehot(pick == kk)
        cols.append(_dot_tn(hi, sel) + _dot_tn(mid, sel) + _dot_tn(lo, sel))
    for j in range(d // 128):
        lanes = slice(j * 128, (j + 1) * 128)
        acc = cols[0] * ybuf[0, :, lanes]
        for kk in range(1, TOP_K):
            acc = acc + cols[kk] * ybuf[kk, :, lanes]
        o_ref[:, lanes] = acc


def _combine(dest, gk, ys):
    n_tiles, _, tm = dest.shape
    d = ys.shape[1]
    per = tm // MOE_TC
    n = n_tiles * per
    blk = lambda i: (i // per, 0, i % per)
    nxt = lambda i: blk(jnp.minimum(i + 1, n - 1))
    return pl.pallas_call(
        _combine_kernel, grid=(n,),
        in_specs=[pl.BlockSpec((1, 8, MOE_TC), blk, memory_space=pltpu.SMEM),
                  pl.BlockSpec((1, 8, MOE_TC), nxt, memory_space=pltpu.SMEM), pl.BlockSpec((1, 8, MOE_TC), blk),
                  pl.BlockSpec(memory_space=pl.ANY)],
        out_specs=pl.BlockSpec((MOE_TC, d), lambda i: (i, 0)),
        out_shape=jax.ShapeDtypeStruct((n_tiles * tm, d), F32),
        scratch_shapes=[pltpu.VMEM((2, TOP_K, MOE_TC, d), F32), pltpu.SemaphoreType.DMA((2,))],
        compiler_params=_params(("arbitrary",)), name="moe_combine",
    )(dest, dest, gk, ys)


def _moe(layer, f, pos_t, gate_t, cnt, tril, w_up, b_up, w_down, b_down):
    t = f.shape[0]
    nb = -(-(t * TOP_K + N_EXPERTS * (MOE_BM - 1)) // MOE_BM)
    per_expert = jnp.sum(cnt, axis=0)
    blocks_e = (per_expert + MOE_BM - 1) // MOE_BM
    blk_end = jnp.cumsum(blocks_e)
    blk_start = blk_end - blocks_e
    base = (blk_start * MOE_BM)[None, :] + jnp.cumsum(cnt, axis=0) - cnt
    live = blk_end[-1]
    blk_id = jnp.minimum(jnp.arange(nb, dtype=jnp.int32), live - 1)
    blk_expert = jnp.sum((blk_end[None, :] <= blk_id[:, None]).astype(jnp.int32), axis=1)
    first = (jnp.arange(nb, dtype=jnp.int32) == blk_start[blk_expert]).astype(jnp.int32)
    meta = jnp.concatenate([blk_expert, first, live[None]]).astype(jnp.int32)
    pad_info = jnp.concatenate([jnp.maximum(blk_end - 1, 0) * MOE_BM, (blocks_e > 0).astype(jnp.int32),
                                live[None]]).astype(jnp.int32)
    base_b = jnp.broadcast_to(base[:, :, None], base.shape + (128,)).astype(jnp.int32)

    dest, gk = _slots(pos_t, gate_t, base_b, tril)
    xs = _dispatch(pad_info, f, dest, nb * MOE_BM)
    ys = _experts(layer, meta, xs, w_up, b_up, w_down, b_down)
    return _combine(dest, gk, ys)


def _final_kernel(x_ref, ffn_ref, mod_ref, g_ref, o_ref):
    x = x_ref[0] + mod_ref[0, 0, 5:6, :] * ffn_ref[0]
    ms = jnp.mean(x * x, axis=-1, keepdims=True)
    o_ref[0] = x * lax.rsqrt(ms + NORM_EPS) * g_ref[...]


def _final(layer, x, ffn, mod_t, gain, n_ctx_tiles, l):
    b, s, d = x.shape
    tok = lambda bb, i: (bb, i + n_ctx_tiles, 0)
    return pl.pallas_call(
        _final_kernel, grid=(b, l // TQ),
        in_specs=[pl.BlockSpec((1, TQ, d), tok), pl.BlockSpec((1, TQ, d), tok),
                  pl.BlockSpec((1, 1, 6, d), lambda bb, i: (layer, bb, 0, 0)),
                  pl.BlockSpec((1, d), lambda bb, i: (0, 0))],
        out_specs=pl.BlockSpec((1, TQ, d), lambda bb, i: (bb, i, 0)),
        out_shape=jax.ShapeDtypeStruct((b, l, d), F32),
        compiler_params=_params(("arbitrary", "arbitrary")), name="final_norm",
    )(x, ffn, mod_t, gain)


def _swap_halves(n_groups, width):
    base = np.arange(n_groups * width).reshape(n_groups, width)
    return np.concatenate([base[:, width // 2:], base[:, :width // 2]], axis=1).reshape(-1)


def _in_columns():
    sizes = (GQA_HEADS * HEAD_DIM, GQA_KV_HEADS * HEAD_DIM, GQA_KV_HEADS * HEAD_DIM, MIXER_W, MIXER_W,
             MIXER_W, MIXER_W, MIXER_W, 3 * MIXER_W, MIXER_W, 2 * DN_HEADS, 2 * DN_HEADS)
    starts = np.concatenate([[0], np.cumsum(sizes)[:-1]])
    aq, ak, av, bu, bv, cq, ck, cv, dqkv, dz, db, da = (np.arange(n) + o for n, o in zip(sizes, starts))
    grp = GQA_HEADS // GQA_KV_HEADS
    expand = np.concatenate([np.arange(HEAD_DIM) + (hd // grp) * HEAD_DIM for hd in range(GQA_HEADS)])
    sw64 = _swap_halves(GQA_HEADS, HEAD_DIM)
    sw32 = _swap_halves(2 * DIFF_HEADS, DIFF_DIM)
    ak_x, av_x = ak[expand], av[expand]
    cols = [aq, aq[sw64], ak_x, ak_x[sw64], av_x, bu, bv, cq, cq[sw32], ck, ck[sw32], cv, dqkv, dz,
            db, da, np.full(128 - 4 * DN_HEADS, -1)]
    return np.concatenate(cols), expand, sw64


def _take_cols(w, cols):
    safe = np.where(cols < 0, 0, cols)
    return jnp.where(jnp.asarray(cols >= 0)[None, :], w[:, safe], 0.0)


def _rope_tables(l, lc):
    rows = l // GRID_W
    r_idx, c_idx = np.meshgrid(np.arange(rows), np.arange(GRID_W), indexing="ij")
    row_pos = jnp.asarray(r_idx.reshape(-1), F32)
    col_pos = jnp.asarray(c_idx.reshape(-1), F32)

    def table(dim, reps):
        n = dim // 4
        inv = jnp.power(ROPE_THETA, -jnp.arange(n, dtype=F32) / n)
        ang = jnp.concatenate([row_pos[:, None] * inv, col_pos[:, None] * inv], axis=-1)
        cos, sin = jnp.cos(ang), jnp.sin(ang)
        cos_t = jnp.tile(jnp.concatenate([cos, cos], axis=-1), (1, reps))
        sin_t = jnp.tile(jnp.concatenate([-sin, sin], axis=-1), (1, reps))
        return (jnp.concatenate([jnp.ones((lc, cos_t.shape[1]), F32), cos_t], axis=0),
                jnp.concatenate([jnp.zeros((lc, sin_t.shape[1]), F32), sin_t], axis=0))

    cos_a, sin_a = table(HEAD_DIM, GQA_HEADS)
    cos_d, sin_d = table(DIFF_DIM, 2 * DIFF_HEADS)
    return jnp.stack([cos_a, sin_a, cos_d, sin_d])


def kernel(x, c, ctx, c_ctx, w_ada, b_ada, norm_mix, norm_ffn, w_in, w_out, gqa_q_norm, gqa_k_norm, gmlp_v_norm, gmlp_w_s, gmlp_b_s, diff_lambda_q1, diff_lambda_k1, diff_lambda_q2, diff_lambda_k2, diff_subln, dn_conv_w, dn_a_log, dn_dt_bias, dn_out_norm, router_w, router_b, exp_w_up, exp_b_up, exp_w_down, exp_b_down, final_norm):
    b, l, d = x.shape
    lc = ctx.shape[1]
    depth = w_ada.shape[0]
    s = lc + l
    assert lc % TQ == 0 and l % TQ == 0 and (b * s) % MOE_TM == 0 and l % GRID_W == 0
    n_ctx_tiles = lc // TQ

    rows = -(-(b + 1) // 8) * 8
    c_all = jnp.zeros((rows, d), F32).at[:b].set(c).at[b].set(c_ctx)
    mod_t = _modulation(c_all, w_ada, b_ada).transpose(0, 2, 1, 3)

    cols, expand, sw64 = _in_columns()
    rope = _rope_tables(l, lc)
    lane = np.arange(MIXER_W)
    g64 = jnp.asarray((lane[:, None] // 64) == (lane[None, :] // 64), BF16)
    tri = jnp.asarray(np.arange(MOE_TM)[:, None] < np.arange(MOE_TM)[None, :], BF16)
    tril = jnp.asarray(np.arange(N_EXPERTS)[:, None] > np.arange(N_EXPERTS)[None, :], BF16)

    xs = jnp.concatenate([ctx, x], axis=1)
    ffn = None
    for layer in range(depth):
        lam_init = 0.8 - 0.6 * math.exp(-0.3 * layer)
        w_ext = _take_cols(w_in[layer], cols).astype(BF16)
        gq = jnp.tile(gqa_q_norm[layer], GQA_HEADS)
        gk = jnp.tile(gqa_k_norm[layer], GQA_HEADS)
        vecs = jnp.zeros((8, MIXER_W), F32).at[0].set(gq).at[1].set(gq[sw64]).at[2].set(gk).at[3].set(gk[sw64])
        vecs = vecs.at[4].set(gmlp_v_norm[layer])
        bst = jnp.repeat(gmlp_b_s[layer].T, GMLP_CH, axis=1)
        xs, (qa, ka, va, out_b, qd, kd, vd, dqkv, dz, dba) = _inproj(
            layer, xs, ffn, mod_t, norm_mix[layer][None, :], w_ext, vecs, g64, rope,
            gmlp_w_s[layer].astype(BF16), bst, n_ctx_tiles)

        out_a = _attention("gqa", qa, ka, va, n_ctx_tiles, lc)
        lam_p = jnp.stack([diff_lambda_q1[layer], diff_lambda_k1[layer], diff_lambda_q2[layer], diff_lambda_k2[layer]])
        sub = jnp.tile(diff_subln[layer], DIFF_HEADS)[None, :]
        out_c = _attention("diff", qd, kd, vd, n_ctx_tiles, lc, extra=(lam_p, sub, g64), lam_init=lam_init)

        avec = jnp.zeros((2, 128), F32)
        avec = avec.at[0, 2 * DN_HEADS:4 * DN_HEADS].set(dn_a_log[layer].reshape(-1))
        avec = avec.at[1, 2 * DN_HEADS:4 * DN_HEADS].set(dn_dt_bias[layer].reshape(-1))
        gq_, gk_, gv_, gkt, gbg, gbgt = _gdn_prep(dqkv, dba, dn_conv_w[layer], avec, g64, n_ctx_tiles)
        o_f, o_b = _gdn_scan(gq_, gk_, gv_, gkt, gbg, gbgt, lc)

        xs, f, logits_t = _outproj(
            layer, xs, out_a, out_b, out_c, o_f, o_b, dz, mod_t,
            w_out[layer].reshape(4, MIXER_W, d).astype(BF16), jnp.tile(dn_out_norm[layer], DN_HEADS)[None, :], g64,
            norm_ffn[layer][None, :], router_w[layer].T, jnp.broadcast_to(router_b[layer][:, None], (N_EXPERTS, 128)),
            n_ctx_tiles)

        pos_t, gate_t, cnt = _route(logits_t, tri)
        ffn = _moe(layer, f, pos_t, gate_t, cnt[:, :, 0], tril, exp_w_up, exp_b_up, exp_w_down, exp_b_down)
        ffn = ffn.reshape(b, s, d)

    return _final(depth - 1, xs, ffn, mod_t, final_norm[None, :], n_ctx_tiles, l)
```

```python
import functools
import math

import numpy as np
import jax
import jax.numpy as jnp
from jax import lax
from jax.experimental import pallas as pl
from jax.experimental.pallas import tpu as pltpu

F32 = jnp.float32
BF16 = jnp.bfloat16
HIGHEST = lax.Precision.HIGHEST

GRID_W = 64
NORM_EPS = 1e-6
ROPE_THETA = 10000.0
HEAD_DIM = 64
GQA_HEADS = 4
GQA_KV_HEADS = 2
GMLP_GROUPS = 4
GMLP_CH = 64
GMLP_CHUNK = 128
DIFF_HEADS = 4
DIFF_DIM = 32
DN_HEADS = 4
DN_DK = 64
DN_DV = 64
DN_CHUNK = 64
N_EXPERTS = 32
TOP_K = 4
SWIGLU_LIMIT = 7.0
SWIGLU_ALPHA = 1.702

MIXER_W = 256
TQ = 256
GDN_PAIR = 2 * DN_CHUNK
MOE_TM = 1024
MOE_BM = 512
MOE_TC = 512
ATT_KC = 1024
ATT_MW = 256
VMEM_LIMIT = 56 * 1024 * 1024

_SEGS = ("aq", "aqs", "ak", "aks", "av", "bu", "bv", "cq", "cqs", "ck", "cks", "cv")
OFF = {name: i * MIXER_W for i, name in enumerate(_SEGS)}
OFF["dqkv"] = len(_SEGS) * MIXER_W
OFF["dz"] = OFF["dqkv"] + 3 * MIXER_W
OFF["dba"] = OFF["dz"] + MIXER_W
W_EXT = OFF["dba"] + 128


def _dot(a, b):
    return jnp.dot(a, b, preferred_element_type=F32)


def _dot_nt(a, b):
    return lax.dot_general(a, b, (((1,), (1,)), ((), ())), preferred_element_type=F32)


def _dot_tn(a, b):
    return lax.dot_general(a, b, (((0,), (0,)), ((), ())), preferred_element_type=F32)


def _split3(x):
    hi = x.astype(BF16)
    r1 = x - hi.astype(F32)
    mid = r1.astype(BF16)
    lo = (r1 - mid.astype(F32)).astype(BF16)
    return hi, mid, lo


def _split2(x):
    hi = x.astype(BF16)
    return hi, (x - hi.astype(F32)).astype(BF16)


def _dot_sel_r(x, sel):
    hi, lo = _split2(x)
    return _dot(hi, sel) + _dot(lo, sel)


def _dot_sel_l(sel, x):
    hi, lo = _split2(x)
    return _dot(sel, hi) + _dot(sel, lo)


def _group_sumsq(x, g_same):
    x2 = x * x
    hi = x2.astype(BF16)
    lo = (x2 - hi.astype(F32)).astype(BF16)
    return _dot(hi, g_same) + _dot(lo, g_same)


def _onehot(cond):
    return jnp.where(cond, 1.0, 0.0).astype(BF16)


def _keep(cond, x):
    return jnp.where(cond, x.astype(F32), 0.0).astype(BF16)


def _silu(x):
    return x * jax.nn.sigmoid(x)


def _softplus(x):
    return jnp.maximum(x, 0.0) + jnp.log1p(jnp.exp(-jnp.abs(x)))


def _params(sem):
    return pltpu.CompilerParams(dimension_semantics=sem, vmem_limit_bytes=VMEM_LIMIT)


def _mod_kernel(c_ref, w_ref, b_ref, o_ref):
    s = _silu(c_ref[...])
    o_ref[0, 0] = jnp.dot(s, w_ref[0], precision=HIGHEST, preferred_element_type=F32) + b_ref[0]


def _modulation(c_all, w_ada, b_ada):
    depth, d, _ = w_ada.shape
    r = c_all.shape[0]
    return pl.pallas_call(
        _mod_kernel,
        grid=(depth, 6),
        in_specs=[
            pl.BlockSpec((r, d), lambda l, j: (0, 0)),
            pl.BlockSpec((1, d, d), lambda l, j: (l, 0, j)),
            pl.BlockSpec((1, 1, d), lambda l, j: (l, 0, j)),
        ],
        out_specs=pl.BlockSpec((1, 1, r, d), lambda l, j: (l, j, 0, 0)),
        out_shape=jax.ShapeDtypeStruct((depth, 6, r, d), F32),
        compiler_params=_params(("arbitrary", "arbitrary")),
        name="adaln_mod",
    )(c_all, w_ada, b_ada.reshape(depth, 1, 6 * d))


def _inproj_kernel(*refs, has_prev, scale_a, scale_d):
    if has_prev:
        x_ref, ffn_ref, modp_ref, refs = refs[0], refs[1], refs[2], refs[3:]
    else:
        x_ref, refs = refs[0], refs[1:]
    (mod_ref, gmix_ref, w_ref, vec_ref, g64_ref, rope_ref, ws_ref, bst_ref) = refs[:8]
    outs = refs[8:]
    if has_prev:
        xo_ref, outs = outs[0], outs[1:]
    (qa_ref, ka_ref, va_ref, ob_ref, qd_ref, kd_ref, vd_ref, dqkv_ref, dz_ref, dba_ref) = outs

    x = x_ref[0]
    if has_prev:
        x = x + modp_ref[0, 0, 5:6, :] * ffn_ref[0]
        xo_ref[0] = x
    ms = jnp.mean(x * x, axis=-1, keepdims=True)
    xn = x * lax.rsqrt(ms + NORM_EPS) * gmix_ref[...]
    h = xn * (1.0 + mod_ref[0, 0, 1:2, :]) + mod_ref[0, 0, 0:1, :]
    p = _dot(h.astype(BF16), w_ref[...])

    def seg(name, width=MIXER_W):
        return p[:, OFF[name]:OFF[name] + width]

    g64 = g64_ref[...]
    cos_a, sin_a, cos_d, sin_d = rope_ref[0], rope_ref[1], rope_ref[2], rope_ref[3]

    def norm_rope(x0, xs, gain, gain_s, scale):
        r = lax.rsqrt(_group_sumsq(x0, g64) * (1.0 / HEAD_DIM) + NORM_EPS)
        return ((x0 * r * gain) * cos_a + (xs * r * gain_s) * sin_a) * scale

    qa_ref[0] = norm_rope(seg("aq"), seg("aqs"), vec_ref[0:1, :], vec_ref[1:2, :], scale_a).astype(BF16)
    ka_ref[0] = norm_rope(seg("ak"), seg("aks"), vec_ref[2:3, :], vec_ref[3:4, :], 1.0).astype(BF16)
    va_ref[0] = seg("av").astype(BF16)

    u = jax.nn.gelu(seg("bu"))
    v = jax.nn.gelu(seg("bv"))
    vn = v * lax.rsqrt(jnp.mean(v * v, axis=-1, keepdims=True) + NORM_EPS) * vec_ref[4:5, :]
    lane_grp = lax.broadcasted_iota(jnp.int32, (GMLP_CHUNK, MIXER_W), 1) // GMLP_CH
    for ci in range(TQ // GMLP_CHUNK):
        rows = slice(ci * GMLP_CHUNK, (ci + 1) * GMLP_CHUNK)
        vc = vn[rows]
        sp = bst_ref[...]
        for g in range(GMLP_GROUPS):
            sp = sp + _dot(ws_ref[g], jnp.where(lane_grp == g, vc, 0.0).astype(BF16))
        ob_ref[0, rows, :] = (u[rows] * sp).astype(BF16)

    qd_ref[0] = ((seg("cq") * cos_d + seg("cqs") * sin_d) * scale_d).astype(BF16)
    kd_ref[0] = (seg("ck") * cos_d + seg("cks") * sin_d).astype(BF16)
    vd_ref[0] = seg("cv").astype(BF16)

    dqkv_ref[0] = seg("dqkv", 3 * MIXER_W)
    dz_ref[0] = seg("dz")
    dba_ref[0] = seg("dba", 128)


def _inproj(layer, x, prev, mod_t, gmix, w_ext, vecs, g64, rope, ws, bst, n_ctx_tiles):
    b, s, d = x.shape
    nt = s // TQ
    ctx_row = b

    def mod_map(l):
        return lambda i, bb: (l, jnp.where(i < n_ctx_tiles, ctx_row, bb), 0, 0)

    tok = lambda i, bb: (bb, i, 0)
    const2 = lambda i, bb: (0, 0)
    const3 = lambda i, bb: (0, 0, 0)
    in_specs = [pl.BlockSpec((1, TQ, d), tok)]
    args = [x]
    if prev is not None:
        in_specs += [pl.BlockSpec((1, TQ, d), tok), pl.BlockSpec((1, 1, 6, d), mod_map(layer - 1))]
        args += [prev, mod_t]
    in_specs += [
        pl.BlockSpec((1, 1, 6, d), mod_map(layer)),
        pl.BlockSpec((1, d), const2),
        pl.BlockSpec((d, W_EXT), const2),
        pl.BlockSpec((8, MIXER_W), const2),
        pl.BlockSpec((MIXER_W, MIXER_W), const2),
        pl.BlockSpec((4, TQ, MIXER_W), lambda i, bb: (0, i, 0)),
        pl.BlockSpec((GMLP_GROUPS, GMLP_CHUNK, GMLP_CHUNK), const3),
        pl.BlockSpec((GMLP_CHUNK, MIXER_W), const2),
    ]
    args += [mod_t, gmix, w_ext, vecs, g64, rope, ws, bst]
    bf = lambda w: jax.ShapeDtypeStruct((b, s, w), BF16)
    ff = lambda w: jax.ShapeDtypeStruct((b, s, w), F32)
    out_shape = [bf(MIXER_W)] * 7 + [ff(3 * MIXER_W), ff(MIXER_W), ff(128)]
    out_specs = [pl.BlockSpec((1, TQ, MIXER_W), tok)] * 7 + [
        pl.BlockSpec((1, TQ, 3 * MIXER_W), tok), pl.BlockSpec((1, TQ, MIXER_W), tok), pl.BlockSpec((1, TQ, 128), tok)]
    if prev is not None:
        out_shape = [ff(d)] + out_shape
        out_specs = [pl.BlockSpec((1, TQ, d), tok)] + out_specs
    outs = pl.pallas_call(
        functools.partial(_inproj_kernel, has_prev=prev is not None,
                          scale_a=HEAD_DIM ** -0.5, scale_d=DIFF_DIM ** -0.5),
        grid=(nt, b), in_specs=in_specs, out_specs=out_specs, out_shape=out_shape,
        compiler_params=_params(("arbitrary", "arbitrary")), name="in_proj",
    )(*args)
    if prev is not None:
        return outs[0], outs[1:]
    return x, outs


def _ones_outside(v, keep_lanes):
    keep = jnp.where(keep_lanes, 1.0, 0.0)
    return v * keep.astype(BF16) + (1.0 - keep).astype(BF16)


def _key_chunks(n_keys):
    cuts = list(range(0, n_keys, ATT_KC)) + [n_keys]
    return list(zip(cuts[:-1], cuts[1:]))


def _attend_groups(qms, k_ref, v1_ref, n_keys, s_scr):
    chunks = _key_chunks(n_keys)

    def scores(g, lo, hi, mx):
        s_c = _dot_nt(qms[g], k_ref[0, lo:hi, :])
        s_scr[g % 2, :, lo:hi] = s_c
        for a in range(lo, hi, ATT_MW):
            piece = s_c[:, a - lo:a - lo + ATT_MW]
            mx = piece if mx is None else jnp.maximum(mx, piece)
        return mx

    def values(g, lo, hi, m, acc):
        e = jnp.exp(s_scr[g % 2, :, lo:hi] - m).astype(BF16)
        part = _dot(e, v1_ref[g, lo:hi, :])
        return part if acc is None else acc + part

    mx = None
    for lo, hi in chunks:
        mx = scores(0, lo, hi, mx)
    outs = []
    for g in range(len(qms)):
        m = jnp.max(mx, axis=-1, keepdims=True)
        mx, acc = None, None
        for lo, hi in chunks:
            if g + 1 < len(qms):
                mx = scores(g + 1, lo, hi, mx)
            acc = values(g, lo, hi, m, acc)
        outs.append(acc / jnp.concatenate([acc[:, 128:], acc[:, :128]], axis=1))
    return outs


def _gqa_kernel(q_ref, k_ref, v_ref, o_ref, v1_scr, s_scr, *, n_ctx_tiles, lc):
    i = pl.program_id(1)
    lane_head = lax.broadcasted_iota(jnp.int32, (TQ, MIXER_W), 1) // HEAD_DIM
    lane_half = lax.broadcasted_iota(jnp.int32, (1, MIXER_W), 1) // 128
    pairs = GQA_HEADS // 2

    @pl.when(i == 0)
    def _():
        for pair in range(pairs):
            v1_scr[pair] = _ones_outside(v_ref[0], lane_half == pair)

    def attend(n_keys):
        q = q_ref[0]
        qms = [jnp.concatenate([_keep(lane_head == 2 * p, q), _keep(lane_head == 2 * p + 1, q)], axis=0)
               for p in range(pairs)]
        outs = _attend_groups(qms, k_ref, v1_scr, n_keys, s_scr)
        acc = jnp.zeros((TQ, MIXER_W), F32)
        for p in range(pairs):
            acc = jnp.where(lane_head == 2 * p, outs[p][:TQ], jnp.where(lane_head == 2 * p + 1, outs[p][TQ:], acc))
        o_ref[0] = acc.astype(BF16)

    @pl.when(i < n_ctx_tiles)
    def _():
        attend(lc)

    @pl.when(i >= n_ctx_tiles)
    def _():
        attend(k_ref.shape[1])


def _diff_kernel(q_ref, k_ref, v_ref, lam_ref, sub_ref, g64_ref, o_ref, v1_scr, s_scr, *, n_ctx_tiles, lc, lam_init):
    i = pl.program_id(1)
    lane = lax.broadcasted_iota(jnp.int32, (TQ, MIXER_W), 1)
    lane_head = lane // (2 * DIFF_DIM)
    lane_map = lane // DIFF_DIM
    head_of_lane = lax.broadcasted_iota(jnp.int32, (1, MIXER_W), 1) // (2 * DIFF_DIM)
    lp = lam_ref[...]
    lam = (jnp.exp(jnp.sum(lp[0:1] * lp[1:2], axis=-1, keepdims=True))
           - jnp.exp(jnp.sum(lp[2:3] * lp[3:4], axis=-1, keepdims=True)) + lam_init)

    @pl.when(i == 0)
    def _():
        for hd in range(DIFF_HEADS):
            v1_scr[hd] = _ones_outside(v_ref[0], head_of_lane == hd)

    def attend(n_keys):
        q = q_ref[0]
        qms = [jnp.concatenate([_keep(lane_map == 2 * hd, q), _keep(lane_map == 2 * hd + 1, q)], axis=0)
               for hd in range(DIFF_HEADS)]
        outs = _attend_groups(qms, k_ref, v1_scr, n_keys, s_scr)
        acc = jnp.zeros((TQ, MIXER_W), F32)
        for hd in range(DIFF_HEADS):
            acc = jnp.where(lane_head == hd, outs[hd][:TQ] - lam * outs[hd][TQ:], acc)
        r = lax.rsqrt(_group_sumsq(acc, g64_ref[...]) * (1.0 / (2 * DIFF_DIM)) + NORM_EPS)
        o_ref[0] = (acc * r * sub_ref[...] * (1.0 - lam_init)).astype(BF16)

    @pl.when(i < n_ctx_tiles)
    def _():
        attend(lc)

    @pl.when(i >= n_ctx_tiles)
    def _():
        attend(k_ref.shape[1])


def _attention(kind, q, k, v, n_ctx_tiles, lc, extra=(), lam_init=0.0):
    b, s, w = q.shape
    nt = s // TQ
    tok = lambda bb, i: (bb, i, 0)
    row = lambda bb, i: (bb, 0, 0)
    in_specs = [pl.BlockSpec((1, TQ, w), tok), pl.BlockSpec((1, s, w), row), pl.BlockSpec((1, s, w), row)]
    if kind == "gqa":
        groups = GQA_HEADS // 2
        body = functools.partial(_gqa_kernel, n_ctx_tiles=n_ctx_tiles, lc=lc)
    else:
        groups = DIFF_HEADS
        body = functools.partial(_diff_kernel, n_ctx_tiles=n_ctx_tiles, lc=lc, lam_init=lam_init)
        in_specs += [pl.BlockSpec(e.shape, lambda bb, i: (0, 0)) for e in extra]
    return pl.pallas_call(
        body, grid=(b, nt), in_specs=in_specs, out_specs=pl.BlockSpec((1, TQ, w), tok),
        out_shape=jax.ShapeDtypeStruct((b, s, w), BF16),
        scratch_shapes=[pltpu.VMEM((groups, s, w), BF16), pltpu.VMEM((2, 2 * TQ, s), F32)],
        compiler_params=_params(("arbitrary", "arbitrary")), name=kind + "_attention",
    )(q, k, v, *extra)


def _gdn_prep_kernel(x_ref, xp_ref, xn_ref, ba_ref, cw_ref, av_ref, g64_ref,
                     q_ref, k_ref, v_ref, kt_ref, bg_ref, bgt_ref, *, n_ctx_tiles, n_tiles):
    i = pl.program_id(0)
    x = x_ref[0]
    has_prev = jnp.where((i != 0) & (i != n_ctx_tiles), 1.0, 0.0)
    has_next = jnp.where((i != n_ctx_tiles - 1) & (i != n_tiles - 1), 1.0, 0.0)
    row = lax.broadcasted_iota(jnp.int32, x.shape, 0)
    x_m1 = jnp.where(row == 0, xp_ref[0, 7:8, :] * has_prev, pltpu.roll(x, 1, 0))
    x_p1 = jnp.where(row == TQ - 1, xn_ref[0, 0:1, :] * has_next, pltpu.roll(x, TQ - 1, 0))
    y = _silu(cw_ref[0:1, :] * x_m1 + cw_ref[1:2, :] * x + cw_ref[2:3, :] * x_p1)
    q, k, v = y[:, :MIXER_W], y[:, MIXER_W:2 * MIXER_W], y[:, 2 * MIXER_W:]
    g64 = g64_ref[...]
    qn = q * lax.rsqrt(_group_sumsq(q, g64) + NORM_EPS) * (DN_DK ** -0.5)
    kn = k * lax.rsqrt(_group_sumsq(k, g64) + NORM_EPS)
    q_ref[0] = qn.astype(BF16)
    k_ref[0] = kn.astype(BF16)
    v_ref[0] = v.astype(BF16)
    kt_ref[0] = kn.T.astype(BF16)
    ba = ba_ref[0]
    lane = lax.broadcasted_iota(jnp.int32, ba.shape, 1)
    beta = jax.nn.sigmoid(ba)
    g = -jnp.exp(av_ref[0:1, :]) * _softplus(ba + av_ref[1:2, :])
    bg = jnp.where(lane < 2 * DN_HEADS, beta, jnp.where(lane < 4 * DN_HEADS, g, 0.0))
    bg_ref[0] = bg
    bgt_ref[0] = bg.T


def _gdn_prep(dqkv, dba, conv_w, avec, g64, n_ctx_tiles):
    b, s, w3 = dqkv.shape
    nt = s // TQ
    nb8 = s // 8
    tok = lambda i, bb: (bb, i, 0)
    tokt = lambda i, bb: (bb, 0, i)
    const2 = lambda i, bb: (0, 0)
    outs = pl.pallas_call(
        functools.partial(_gdn_prep_kernel, n_ctx_tiles=n_ctx_tiles, n_tiles=nt),
        grid=(nt, b),
        in_specs=[
            pl.BlockSpec((1, TQ, w3), tok),
            pl.BlockSpec((1, 8, w3), lambda i, bb: (bb, jnp.maximum(i * (TQ // 8) - 1, 0), 0)),
            pl.BlockSpec((1, 8, w3), lambda i, bb: (bb, jnp.minimum((i + 1) * (TQ // 8), nb8 - 1), 0)),
            pl.BlockSpec((1, TQ, 128), tok),
            pl.BlockSpec((3, w3), const2),
            pl.BlockSpec((2, 128), const2),
            pl.BlockSpec((MIXER_W, MIXER_W), const2),
        ],
        out_specs=[pl.BlockSpec((1, TQ, MIXER_W), tok)] * 3 + [
            pl.BlockSpec((1, MIXER_W, TQ), tokt), pl.BlockSpec((1, TQ, 128), tok), pl.BlockSpec((1, 128, TQ), tokt)],
        out_shape=[jax.ShapeDtypeStruct((b, s, MIXER_W), BF16)] * 3 + [
            jax.ShapeDtypeStruct((b, MIXER_W, s), BF16), jax.ShapeDtypeStruct((b, s, 128), F32),
            jax.ShapeDtypeStruct((b, 128, s), F32)],
        compiler_params=_params(("arbitrary", "arbitrary")), name="gdn_prep",
    )(dqkv, dqkv, dqkv, dba, conv_w, avec, g64)
    return outs


def _gdn_scan_kernel(qf, kf, vf, ktf, bgf, bgtf, qb, kb, vb, ktb, bgb, bgtb, of_ref, ob_ref, st_ref):
    @pl.when(pl.program_id(1) == 0)
    def _():
        st_ref[...] = jnp.zeros_like(st_ref)

    pp, cc, w = GDN_PAIR, DN_CHUNK, MIXER_W
    dirs = (0, 1)
    heads = range(DN_HEADS)
    chains = [(d, hd) for d in dirs for hd in heads]
    q_refs, k_refs, v_refs, kt_refs = (qf, qb), (kf, kb), (vf, vb), (ktf, ktb)
    bg_refs, bgt_refs, o_refs = (bgf, bgb), (bgtf, bgtb), (of_ref, ob_ref)

    ii = lax.broadcasted_iota(jnp.int32, (pp, pp), 0)
    jj = lax.broadcasted_iota(jnp.int32, (pp, pp), 1)
    same = (ii // cc) == (jj // cc)
    incl = (same & (jj <= ii), same & (jj >= ii))
    strict = (same & (jj < ii), same & (jj > ii))
    incl_b = [_onehot(m) for m in incl]
    incl_tb = [_onehot(same & (ii <= jj)), _onehot(same & (ii >= jj))]
    eye = jnp.where(ii == jj, 1.0, 0.0)
    merge_masks = [((ii // (2 * sz)) == (jj // (2 * sz))) & ((ii // sz) != (jj // sz))
                   for sz in (2 ** e for e in range(int(math.log2(cc))))]

    src = lax.broadcasted_iota(jnp.int32, (128, w), 0)
    lane_w = lax.broadcasted_iota(jnp.int32, (128, w), 1)
    src2 = lax.broadcasted_iota(jnp.int32, (128, DN_HEADS * pp), 0)
    lane2 = lax.broadcasted_iota(jnp.int32, (128, DN_HEADS * pp), 1)
    bg = [r[0] for r in bg_refs]
    beta_x = [_dot_sel_r(bg[d], _onehot(src == DN_HEADS * d + lane_w // DN_DV)) for d in dirs]
    g_x = [_dot_sel_r(bg[d], _onehot(src == 2 * DN_HEADS + DN_HEADS * d + lane_w // DN_DV)) for d in dirs]
    g_x2 = [_dot_sel_r(bg[d], _onehot(src2 == 2 * DN_HEADS + DN_HEADS * d + lane2 // pp)) for d in dirs]
    cg_rows = [_dot_sel_r(bgt_refs[d][0], incl_tb[d]) for d in dirs]
    cg_x = [_dot_sel_l(incl_b[d], g_x[d]) for d in dirs]
    cg_x2 = [_dot_sel_l(incl_b[d], g_x2[d]) for d in dirs]

    q = [r[0] for r in q_refs]
    k = [r[0] for r in k_refs]
    kf32 = [t.astype(F32) for t in k]
    e_cg = [jnp.exp(t) for t in cg_x]
    rhs_v = [v_refs[d][0].astype(F32) * beta_x[d] for d in dirs]
    rhs_k = [kf32[d] * (beta_x[d] * e_cg[d]) for d in dirs]
    k_beta = [kf32[d] * beta_x[d] for d in dirs]
    lane_head = lax.broadcasted_iota(jnp.int32, (pp, w), 1) // DN_DV

    kk = [_dot_nt(jnp.where(lane_head == hd, k_beta[d], 0.0).astype(BF16), k[d]) for d, hd in chains]
    qk = [_dot_nt(_keep(lane_head == hd, q[d]), k[d]) for d, hd in chains]
    a, qkd = [], []
    for ci, (d, hd) in enumerate(chains):
        gl = 2 * DN_HEADS + DN_HEADS * d + hd
        diff = jnp.where(incl[d], cg_x2[d][:, hd * pp:(hd + 1) * pp] - cg_rows[d][gl:gl + 1, :], 0.0)
        decay = jnp.where(incl[d], jnp.exp(diff), 0.0)
        a.append(jnp.where(strict[d], kk[ci] * decay, 0.0))
        qkd.append((qk[ci] * decay).astype(BF16))

    t_inv = [eye - jnp.where(merge_masks[0], a_c, 0.0) for a_c in a]
    for mask in merge_masks[1:]:
        tb = [t.astype(BF16) for t in t_inv]
        lm = [_dot(jnp.where(mask, a_c, 0.0).astype(BF16), tb_c).astype(BF16) for a_c, tb_c in zip(a, tb)]
        t_inv = [t - _dot(tb_c, lm_c) for t, tb_c, lm_c in zip(t_inv, tb, lm)]
    tb = [t.astype(BF16) for t in t_inv]
    u_part = [_dot(tb[ci], jnp.where(lane_head == hd, rhs_v[d], 0.0).astype(BF16)) for ci, (d, hd) in enumerate(chains)]
    w_part = [_dot(tb[ci], jnp.where(lane_head == hd, rhs_k[d], 0.0).astype(BF16)) for ci, (d, hd) in enumerate(chains)]
    u_all = [sum(u_part[d * DN_HEADS + hd] for hd in heads) for d in dirs]
    w_all = [sum(w_part[d * DN_HEADS + hd] for hd in heads) for d in dirs]

    st = [st_ref[d] for d in dirs]
    blk = (lax.broadcasted_iota(jnp.int32, (w, w), 0) // DN_DK) == (lax.broadcasted_iota(jnp.int32, (w, w), 1) // DN_DV)
    kt = [r[0] for r in kt_refs]
    order = (((0, cc), (cc, pp)), ((cc, pp), (0, cc)))
    zeros_c = jnp.zeros((cc, w), F32)
    lane_head_c = lax.broadcasted_iota(jnp.int32, (cc, w), 1) // DN_DV

    def place(lo, t):
        return jnp.concatenate([t, zeros_c] if lo == 0 else [zeros_c, t], axis=0)

    nv_acc = [None, None]
    for step in range(2):
        lo = [order[d][step][0] for d in dirs]
        rows = [slice(*order[d][step]) for d in dirs]
        stb = [t.astype(BF16) for t in st]
        w_s = [_dot(w_all[d][rows[d]].astype(BF16), stb[d]) for d in dirs]
        q_s = [_dot(q[d][rows[d]], stb[d]) for d in dirs]
        nv = [u_all[d][rows[d]] - w_s[d] for d in dirs]
        for d in dirs:
            full = place(lo[d], nv[d])
            nv_acc[d] = full if nv_acc[d] is None else nv_acc[d] + full
        nvb = [t.astype(BF16) for t in nv_acc]
        intra = [_dot(qkd[ci][rows[d]], nvb[d]) for ci, (d, hd) in enumerate(chains)]
        last = [order[0][step][1] - 1, order[1][step][0]]
        g_end = [cg_x[d][last[d]:last[d] + 1, :] for d in dirs]
        nvs = [place(lo[d], nv[d] * jnp.exp(g_end[d] - cg_x[d][rows[d]])).astype(BF16) for d in dirs]
        upd = [_dot(kt[d], nvs[d]) for d in dirs]
        for d in dirs:
            o = e_cg[d][rows[d]] * q_s[d]
            for hd in heads:
                o = o + jnp.where(lane_head_c == hd, intra[d * DN_HEADS + hd], 0.0)
            o_refs[d][0, rows[d], :] = o
            st[d] = st[d] * jnp.exp(g_end[d]) + jnp.where(blk, upd[d], 0.0)
    for d in dirs:
        st_ref[d] = st[d]


def _gdn_scan(q, k, v, kt, bg, bgt, lc):
    b, s, w = q.shape
    n_pairs = s // GDN_PAIR
    ncp = lc // GDN_PAIR

    def fwd(bb, i):
        return i

    def bwd(bb, i):
        return jnp.where(i < ncp, ncp - 1 - i, n_pairs - 1 + ncp - i)

    def specs(pos):
        tok = lambda bb, i: (bb, pos(bb, i), 0)
        tokt = lambda bb, i: (bb, 0, pos(bb, i))
        return [pl.BlockSpec((1, GDN_PAIR, w), tok)] * 3 + [
            pl.BlockSpec((1, w, GDN_PAIR), tokt), pl.BlockSpec((1, GDN_PAIR, 128), tok),
            pl.BlockSpec((1, 128, GDN_PAIR), tokt)]

    return pl.pallas_call(
        _gdn_scan_kernel, grid=(b, n_pairs),
        in_specs=specs(fwd) + specs(bwd),
        out_specs=[pl.BlockSpec((1, GDN_PAIR, w), lambda bb, i: (bb, fwd(bb, i), 0)),
                   pl.BlockSpec((1, GDN_PAIR, w), lambda bb, i: (bb, bwd(bb, i), 0))],
        out_shape=[jax.ShapeDtypeStruct((b, s, w), F32)] * 2,
        scratch_shapes=[pltpu.VMEM((2, w, w), F32)],
        compiler_params=_params(("arbitrary", "arbitrary")), name="gdn_scan",
    )(q, k, v, kt, bg, bgt, q, k, v, kt, bg, bgt)


def _outproj_kernel(x_ref, oa_ref, ob_ref, oc_ref, of_ref, obw_ref, dz_ref, mod_ref, w_ref, gout_ref,
                    g64_ref, gffn_ref, wr_ref, br_ref, xo_ref, f_ref, lg_ref):
    o = of_ref[0] + obw_ref[0]
    r = lax.rsqrt(_group_sumsq(o, g64_ref[...]) * (1.0 / DN_DV) + NORM_EPS)
    od = (o * r * gout_ref[...] * _silu(dz_ref[0])).astype(BF16)
    y = (_dot(oa_ref[0], w_ref[0]) + _dot(ob_ref[0], w_ref[1]) + _dot(oc_ref[0], w_ref[2]) + _dot(od, w_ref[3]))
    x = x_ref[0] + mod_ref[0, 0, 2:3, :] * y
    xo_ref[0] = x
    ms = jnp.mean(x * x, axis=-1, keepdims=True)
    f = (x * lax.rsqrt(ms + NORM_EPS) * gffn_ref[...]) * (1.0 + mod_ref[0, 0, 4:5, :]) + mod_ref[0, 0, 3:4, :]
    f_ref[...] = f
    lg_ref[...] = lax.dot_general(wr_ref[...], f, (((1,), (1,)), ((), ())), precision=HIGHEST,
                                  preferred_element_type=F32) + br_ref[:, 0:1]


def _outproj(layer, x, oa, ob, oc, o_f, o_b, dz, mod_t, w_out4, gout, g64, gffn, wr_t, br, n_ctx_tiles):
    b, s, d = x.shape
    nt = s // TQ
    ctx_row = b
    tok = lambda i, bb: (bb, i, 0)
    const2 = lambda i, bb: (0, 0)
    slab = pl.BlockSpec((1, TQ, MIXER_W), tok)
    return pl.pallas_call(
        _outproj_kernel, grid=(nt, b),
        in_specs=[pl.BlockSpec((1, TQ, d), tok), slab, slab, slab, slab, slab, slab,
                  pl.BlockSpec((1, 1, 6, d), lambda i, bb: (layer, jnp.where(i < n_ctx_tiles, ctx_row, bb), 0, 0)),
                  pl.BlockSpec((4, MIXER_W, d), lambda i, bb: (0, 0, 0)),
                  pl.BlockSpec((1, MIXER_W), const2), pl.BlockSpec((MIXER_W, MIXER_W), const2),
                  pl.BlockSpec((1, d), const2), pl.BlockSpec((N_EXPERTS, d), const2),
                  pl.BlockSpec((N_EXPERTS, 128), const2)],
        out_specs=[pl.BlockSpec((1, TQ, d), tok), pl.BlockSpec((TQ, d), lambda i, bb: (bb * nt + i, 0)),
                   pl.BlockSpec((N_EXPERTS, TQ), lambda i, bb: (0, bb * nt + i))],
        out_shape=[jax.ShapeDtypeStruct((b, s, d), F32), jax.ShapeDtypeStruct((b * s, d), F32),
                   jax.ShapeDtypeStruct((N_EXPERTS, b * s), F32)],
        compiler_params=_params(("arbitrary", "arbitrary")), name="out_proj",
    )(x, oa, ob, oc, o_f, o_b, dz, mod_t, w_out4, gout, g64, gffn, wr_t, br)


def _route_kernel(lg_ref, tri_ref, pos_ref, gate_ref, cnt_ref):
    x = lg_ref[...]
    e_iota = lax.broadcasted_iota(jnp.int32, x.shape, 0).astype(F32)
    work = x
    chosen = jnp.zeros(x.shape, F32)
    top = None
    den = None
    for kk in range(TOP_K):
        m = jnp.max(work, axis=0, keepdims=True)
        idx = jnp.min(jnp.where(work == m, e_iota, float(N_EXPERTS)), axis=0, keepdims=True)
        pick = e_iota == idx
        chosen = jnp.where(pick, 1.0, chosen)
        if kk == 0:
            top = m
            den = jnp.ones_like(m)
        else:
            den = den + jnp.exp(m - top)
        work = jnp.where(pick, -jnp.inf, work)
    sel = chosen > 0.5
    gate_ref[0] = jnp.where(sel, jnp.exp(x - top) / den, 0.0)
    rank = _dot(chosen.astype(BF16), tri_ref[...])
    pos_ref[0] = jnp.where(sel, rank.astype(jnp.int32), -1)
    cnt = jnp.sum(chosen, axis=1, keepdims=True).astype(jnp.int32)
    cnt_ref[0] = jnp.broadcast_to(cnt, cnt_ref.shape[1:])


def _route(logits_t, tri):
    n_exp, t = logits_t.shape
    n_tiles = t // MOE_TM
    return pl.pallas_call(
        _route_kernel, grid=(n_tiles,),
        in_specs=[pl.BlockSpec((n_exp, MOE_TM), lambda i: (0, i)), pl.BlockSpec((MOE_TM, MOE_TM), lambda i: (0, 0))],
        out_specs=[pl.BlockSpec((1, n_exp, MOE_TM), lambda i: (i, 0, 0)),
                   pl.BlockSpec((1, n_exp, MOE_TM), lambda i: (i, 0, 0)),
                   pl.BlockSpec((1, n_exp, 128), lambda i: (i, 0, 0))],
        out_shape=[jax.ShapeDtypeStruct((n_tiles, n_exp, MOE_TM), jnp.int32),
                   jax.ShapeDtypeStruct((n_tiles, n_exp, MOE_TM), F32),
                   jax.ShapeDtypeStruct((n_tiles, n_exp, 128), jnp.int32)],
        compiler_params=_params(("arbitrary",)), name="route",
    )(logits_t, tri)


def _slots_kernel(pos_ref, gate_ref, base_ref, tril_ref, dest_ref, gk_ref):
    pos = pos_ref[0]
    chosen = pos >= 0
    slot = (base_ref[0][:, 0:1] + pos).astype(F32)
    choice = _dot(tril_ref[...], _onehot(chosen))
    gate = gate_ref[0]
    pad = jnp.zeros((8 - TOP_K, pos.shape[1]), F32)
    d_rows, g_rows = [], []
    for kk in range(TOP_K):
        mine = chosen & (choice == float(kk))
        d_rows.append(jnp.sum(jnp.where(mine, slot, 0.0), axis=0, keepdims=True))
        g_rows.append(jnp.sum(jnp.where(mine, gate, 0.0), axis=0, keepdims=True))
    dest_ref[0] = jnp.concatenate(d_rows + [pad], axis=0).astype(jnp.int32)
    gk_ref[0] = jnp.concatenate(g_rows + [pad], axis=0)


def _slots(pos_t, gate_t, base_b, tril):
    n_tiles, n_exp, tm = pos_t.shape
    tile = lambda i: (i, 0, 0)
    return pl.pallas_call(
        _slots_kernel, grid=(n_tiles,),
        in_specs=[pl.BlockSpec((1, n_exp, tm), tile), pl.BlockSpec((1, n_exp, tm), tile),
                  pl.BlockSpec((1, n_exp, 128), tile), pl.BlockSpec((n_exp, n_exp), lambda i: (0, 0))],
        out_specs=[pl.BlockSpec((1, 8, tm), tile), pl.BlockSpec((1, 8, tm), tile)],
        out_shape=[jax.ShapeDtypeStruct((n_tiles, 8, tm), jnp.int32), jax.ShapeDtypeStruct((n_tiles, 8, tm), F32)],
        compiler_params=_params(("arbitrary",)), name="moe_slots",
    )(pos_t, gate_t, base_b, tril)


def _dispatch_kernel(pad_ref, f_ref, dest_ref, xs_ref, zbuf, zsem, sem):
    tm = f_ref.shape[0]
    n_blocks = xs_ref.shape[0] // MOE_BM

    @pl.when(pl.program_id(0) == 0)
    def _():
        zbuf[...] = jnp.zeros_like(zbuf)
        live = pad_ref[2 * N_EXPERTS]

        def zero_block(row):
            return pltpu.make_async_copy(zbuf, xs_ref.at[pl.ds(pl.multiple_of(row, MOE_BM), MOE_BM)], zsem)

        for wait in (False, True):
            for e in range(N_EXPERTS):
                for cond, row in ((pad_ref[N_EXPERTS + e] > 0, pad_ref[e]), (live + e < n_blocks, (live + e) * MOE_BM)):
                    @pl.when(cond)
                    def _():
                        zero_block(row).wait() if wait else zero_block(row).start()

    def rows(tok, carry):
        for kk in range(TOP_K):
            pltpu.make_async_copy(f_ref.at[pl.ds(tok, 1)], xs_ref.at[pl.ds(dest_ref[0, kk, tok], 1)],
                                  sem).start(priority=kk % 2)
        return carry

    lax.fori_loop(0, tm, rows, 0, unroll=8)
    for kk in range(TOP_K):
        pltpu.make_async_copy(f_ref, xs_ref.at[pl.ds(0, tm)], sem).wait()


def _dispatch(pad_info, f, dest, n_slots):
    t, d = f.shape
    n_tiles, _, tm = dest.shape
    grid_spec = pltpu.PrefetchScalarGridSpec(
        num_scalar_prefetch=1, grid=(n_tiles,),
        in_specs=[pl.BlockSpec((tm, d), lambda i, p: (i, 0)),
                  pl.BlockSpec((1, 8, tm), lambda i, p: (i, 0, 0), memory_space=pltpu.SMEM)],
        out_specs=pl.BlockSpec(memory_space=pl.ANY),
        scratch_shapes=[pltpu.VMEM((MOE_BM, d), F32), pltpu.SemaphoreType.DMA(()), pltpu.SemaphoreType.DMA(())],
    )
    return pl.pallas_call(
        _dispatch_kernel, grid_spec=grid_spec, out_shape=jax.ShapeDtypeStruct((n_slots, d), F32),
        compiler_params=_params(("arbitrary",)), name="moe_dispatch",
    )(pad_info, f, dest)


def _experts_kernel(meta_ref, xs_ref, wu_ref, bu_ref, wd_ref, bd_ref, ys_ref, wu_bf, wd_bf):
    i = pl.program_id(0)
    nb = pl.num_programs(0)
    live = i < meta_ref[2 * nb]
    de = wd_bf.shape[0]

    @pl.when(live & (meta_ref[nb + i] > 0))
    def _():
        wu_bf[...] = wu_ref[0, 0].astype(BF16)
        wd_bf[...] = wd_ref[0, 0].astype(BF16)

    @pl.when(live)
    def _():
        hgu = _dot(xs_ref[...].astype(BF16), wu_bf[...]) + bu_ref[0, 0]
        gate = jnp.minimum(hgu[:, :de], SWIGLU_LIMIT)
        up = jnp.clip(hgu[:, de:], -SWIGLU_LIMIT, SWIGLU_LIMIT)
        hid = gate * jax.nn.sigmoid(SWIGLU_ALPHA * gate) * (up + 1.0)
        ys_ref[...] = _dot(hid.astype(BF16), wd_bf[...]) + bd_ref[0, 0]

    @pl.when(jnp.logical_not(live))
    def _():
        ys_ref[...] = jnp.zeros_like(ys_ref)


def _experts(layer, meta, xs, w_up, b_up, w_down, b_down):
    n_slots, d = xs.shape
    nb = n_slots // MOE_BM
    depth, n_exp, _, de2 = w_up.shape
    de = de2 // 2
    blk = lambda i, m: (jnp.minimum(i, m[2 * nb] - 1), 0)
    exp = lambda i, m: (layer, m[i], 0, 0)
    grid_spec = pltpu.PrefetchScalarGridSpec(
        num_scalar_prefetch=1, grid=(nb,),
        in_specs=[pl.BlockSpec((MOE_BM, d), blk),
                  pl.BlockSpec((1, 1, d, de2), exp), pl.BlockSpec((1, 1, 1, de2), exp),
                  pl.BlockSpec((1, 1, de, d), exp), pl.BlockSpec((1, 1, 1, d), exp)],
        out_specs=pl.BlockSpec((MOE_BM, d), lambda i, m: (i, 0)),
        scratch_shapes=[pltpu.VMEM((d, de2), BF16), pltpu.VMEM((de, d), BF16)],
    )
    return pl.pallas_call(
        _experts_kernel, grid_spec=grid_spec, out_shape=jax.ShapeDtypeStruct((n_slots, d), F32),
        compiler_params=_params(("arbitrary",)), name="moe_experts",
    )(meta, xs, w_up, b_up.reshape(depth, n_exp, 1, de2), w_down, b_down.reshape(depth, n_exp, 1, d))


def _combine_kernel(dest_ref, dest_next_ref, gk_ref, ys_ref, o_ref, ybuf, sems):
    tc, d = o_ref.shape
    i = pl.program_id(0)
    n = pl.num_programs(0)
    cur = i % 2

    def fetch(idx_ref, buf):
        def rows(tok, carry):
            for kk in range(TOP_K):
                pltpu.make_async_copy(ys_ref.at[pl.ds(idx_ref[0, kk, tok], 1)], ybuf.at[buf, kk, pl.ds(tok, 1)],
                                      sems.at[buf]).start(priority=kk % 2)
            return carry

        lax.fori_loop(0, tc, rows, 0, unroll=8)

    @pl.when(i == 0)
    def _():
        fetch(dest_ref, 0)

    @pl.when(i + 1 < n)
    def _():
        fetch(dest_next_ref, 1 - cur)

    for kk in range(TOP_K):
        pltpu.make_async_copy(ys_ref.at[pl.ds(0, tc)], ybuf.at[cur, kk], sems.at[cur]).wait()
    ybuf = ybuf.at[cur]

    hi, mid, lo = _split3(gk_ref[0])
    pick = lax.broadcasted_iota(jnp.int32, (8, 128), 0)
    cols = []
    for kk in range(TOP_K):
        sel = _onehot(pick == kk)
        cols.append(_dot_tn(hi, sel) + _dot_tn(mid, sel) + _dot_tn(lo, sel))
    for j in range(d // 128):
        lanes = slice(j * 128, (j + 1) * 128)
        acc = cols[0] * ybuf[0, :, lanes]
        for kk in range(1, TOP_K):
            acc = acc + cols[kk] * ybuf[kk, :, lanes]
        o_ref[:, lanes] = acc


def _combine(dest, gk, ys):
    n_tiles, _, tm = dest.shape
    d = ys.shape[1]
    per = tm // MOE_TC
    n = n_tiles * per
    blk = lambda i: (i // per, 0, i % per)
    nxt = lambda i: blk(jnp.minimum(i + 1, n - 1))
    return pl.pallas_call(
        _combine_kernel, grid=(n,),
        in_specs=[pl.BlockSpec((1, 8, MOE_TC), blk, memory_space=pltpu.SMEM),
                  pl.BlockSpec((1, 8, MOE_TC), nxt, memory_space=pltpu.SMEM), pl.BlockSpec((1, 8, MOE_TC), blk),
                  pl.BlockSpec(memory_space=pl.ANY)],
        out_specs=pl.BlockSpec((MOE_TC, d), lambda i: (i, 0)),
        out_shape=jax.ShapeDtypeStruct((n_tiles * tm, d), F32),
        scratch_shapes=[pltpu.VMEM((2, TOP_K, MOE_TC, d), F32), pltpu.SemaphoreType.DMA((2,))],
        compiler_params=_params(("arbitrary",)), name="moe_combine",
    )(dest, dest, gk, ys)


def _moe(layer, f, pos_t, gate_t, cnt, tril, w_up, b_up, w_down, b_down):
    t = f.shape[0]
    nb = -(-(t * TOP_K + N_EXPERTS * (MOE_BM - 1)) // MOE_BM)
    per_expert = jnp.sum(cnt, axis=0)
    blocks_e = (per_expert + MOE_BM - 1) // MOE_BM
    blk_end = jnp.cumsum(blocks_e)
    blk_start = blk_end - blocks_e
    base = (blk_start * MOE_BM)[None, :] + jnp.cumsum(cnt, axis=0) - cnt
    live = blk_end[-1]
    blk_id = jnp.minimum(jnp.arange(nb, dtype=jnp.int32), live - 1)
    blk_expert = jnp.sum((blk_end[None, :] <= blk_id[:, None]).astype(jnp.int32), axis=1)
    first = (jnp.arange(nb, dtype=jnp.int32) == blk_start[blk_expert]).astype(jnp.int32)
    meta = jnp.concatenate([blk_expert, first, live[None]]).astype(jnp.int32)
    pad_info = jnp.concatenate([jnp.maximum(blk_end - 1, 0) * MOE_BM, (blocks_e > 0).astype(jnp.int32),
                                live[None]]).astype(jnp.int32)
    base_b = jnp.broadcast_to(base[:, :, None], base.shape + (128,)).astype(jnp.int32)

    dest, gk = _slots(pos_t, gate_t, base_b, tril)
    xs = _dispatch(pad_info, f, dest, nb * MOE_BM)
    ys = _experts(layer, meta, xs, w_up, b_up, w_down, b_down)
    return _combine(dest, gk, ys)


def _final_kernel(x_ref, ffn_ref, mod_ref, g_ref, o_ref):
    x = x_ref[0] + mod_ref[0, 0, 5:6, :] * ffn_ref[0]
    ms = jnp.mean(x * x, axis=-1, keepdims=True)
    o_ref[0] = x * lax.rsqrt(ms + NORM_EPS) * g_ref[...]


def _final(layer, x, ffn, mod_t, gain, n_ctx_tiles, l):
    b, s, d = x.shape
    tok = lambda bb, i: (bb, i + n_ctx_tiles, 0)
    return pl.pallas_call(
        _final_kernel, grid=(b, l // TQ),
        in_specs=[pl.BlockSpec((1, TQ, d), tok), pl.BlockSpec((1, TQ, d), tok),
                  pl.BlockSpec((1, 1, 6, d), lambda bb, i: (layer, bb, 0, 0)),
                  pl.BlockSpec((1, d), lambda bb, i: (0, 0))],
        out_specs=pl.BlockSpec((1, TQ, d), lambda bb, i: (bb, i, 0)),
        out_shape=jax.ShapeDtypeStruct((b, l, d), F32),
        compiler_params=_params(("arbitrary", "arbitrary")), name="final_norm",
    )(x, ffn, mod_t, gain)


def _swap_halves(n_groups, width):
    base = np.arange(n_groups * width).reshape(n_groups, width)
    return np.concatenate([base[:, width // 2:], base[:, :width // 2]], axis=1).reshape(-1)


def _in_columns():
    sizes = (GQA_HEADS * HEAD_DIM, GQA_KV_HEADS * HEAD_DIM, GQA_KV_HEADS * HEAD_DIM, MIXER_W, MIXER_W,
             MIXER_W, MIXER_W, MIXER_W, 3 * MIXER_W, MIXER_W, 2 * DN_HEADS, 2 * DN_HEADS)
    starts = np.concatenate([[0], np.cumsum(sizes)[:-1]])
    aq, ak, av, bu, bv, cq, ck, cv, dqkv, dz, db, da = (np.arange(n) + o for n, o in zip(sizes, starts))
    grp = GQA_HEADS // GQA_KV_HEADS
    expand = np.concatenate([np.arange(HEAD_DIM) + (hd // grp) * HEAD_DIM for hd in range(GQA_HEADS)])
    sw64 = _swap_halves(GQA_HEADS, HEAD_DIM)
    sw32 = _swap_halves(2 * DIFF_HEADS, DIFF_DIM)
    ak_x, av_x = ak[expand], av[expand]
    cols = [aq, aq[sw64], ak_x, ak_x[sw64], av_x, bu, bv, cq, cq[sw32], ck, ck[sw32], cv, dqkv, dz,
            db, da, np.full(128 - 4 * DN_HEADS, -1)]
    return np.concatenate(cols), expand, sw64


def _take_cols(w, cols):
    safe = np.where(cols < 0, 0, cols)
    return jnp.where(jnp.asarray(cols >= 0)[None, :], w[:, safe], 0.0)


def _rope_tables(l, lc):
    rows = l // GRID_W
    r_idx, c_idx = np.meshgrid(np.arange(rows), np.arange(GRID_W), indexing="ij")
    row_pos = jnp.asarray(r_idx.reshape(-1), F32)
    col_pos = jnp.asarray(c_idx.reshape(-1), F32)

    def table(dim, reps):
        n = dim // 4
        inv = jnp.power(ROPE_THETA, -jnp.arange(n, dtype=F32) / n)
        ang = jnp.concatenate([row_pos[:, None] * inv, col_pos[:, None] * inv], axis=-1)
        cos, sin = jnp.cos(ang), jnp.sin(ang)
        cos_t = jnp.tile(jnp.concatenate([cos, cos], axis=-1), (1, reps))
        sin_t = jnp.tile(jnp.concatenate([-sin, sin], axis=-1), (1, reps))
        return (jnp.concatenate([jnp.ones((lc, cos_t.shape[1]), F32), cos_t], axis=0),
                jnp.concatenate([jnp.zeros((lc, sin_t.shape[1]), F32), sin_t], axis=0))

    cos_a, sin_a = table(HEAD_DIM, GQA_HEADS)
    cos_d, sin_d = table(DIFF_DIM, 2 * DIFF_HEADS)
    return jnp.stack([cos_a, sin_a, cos_d, sin_d])


def kernel(x, c, ctx, c_ctx, w_ada, b_ada, norm_mix, norm_ffn, w_in, w_out, gqa_q_norm, gqa_k_norm, gmlp_v_norm, gmlp_w_s, gmlp_b_s, diff_lambda_q1, diff_lambda_k1, diff_lambda_q2, diff_lambda_k2, diff_subln, dn_conv_w, dn_a_log, dn_dt_bias, dn_out_norm, router_w, router_b, exp_w_up, exp_b_up, exp_w_down, exp_b_down, final_norm):
    b, l, d = x.shape
    lc = ctx.shape[1]
    depth = w_ada.shape[0]
    s = lc + l
    assert lc % TQ == 0 and l % TQ == 0 and (b * s) % MOE_TM == 0 and l % GRID_W == 0
    n_ctx_tiles = lc // TQ

    rows = -(-(b + 1) // 8) * 8
    c_all = jnp.zeros((rows, d), F32).at[:b].set(c).at[b].set(c_ctx)
    mod_t = _modulation(c_all, w_ada, b_ada).transpose(0, 2, 1, 3)

    cols, expand, sw64 = _in_columns()
    rope = _rope_tables(l, lc)
    lane = np.arange(MIXER_W)
    g64 = jnp.asarray((lane[:, None] // 64) == (lane[None, :] // 64), BF16)
    tri = jnp.asarray(np.arange(MOE_TM)[:, None] < np.arange(MOE_TM)[None, :], BF16)
    tril = jnp.asarray(np.arange(N_EXPERTS)[:, None] > np.arange(N_EXPERTS)[None, :], BF16)

    xs = jnp.concatenate([ctx, x], axis=1)
    ffn = None
    for layer in range(depth):
        lam_init = 0.8 - 0.6 * math.exp(-0.3 * layer)
        w_ext = _take_cols(w_in[layer], cols).astype(BF16)
        gq = jnp.tile(gqa_q_norm[layer], GQA_HEADS)
        gk = jnp.tile(gqa_k_norm[layer], GQA_HEADS)
        vecs = jnp.zeros((8, MIXER_W), F32).at[0].set(gq).at[1].set(gq[sw64]).at[2].set(gk).at[3].set(gk[sw64])
        vecs = vecs.at[4].set(gmlp_v_norm[layer])
        bst = jnp.repeat(gmlp_b_s[layer].T, GMLP_CH, axis=1)
        xs, (qa, ka, va, out_b, qd, kd, vd, dqkv, dz, dba) = _inproj(
            layer, xs, ffn, mod_t, norm_mix[layer][None, :], w_ext, vecs, g64, rope,
            gmlp_w_s[layer].astype(BF16), bst, n_ctx_tiles)

        out_a = _attention("gqa", qa, ka, va, n_ctx_tiles, lc)
        lam_p = jnp.stack([diff_lambda_q1[layer], diff_lambda_k1[layer], diff_lambda_q2[layer], diff_lambda_k2[layer]])
        sub = jnp.tile(diff_subln[layer], DIFF_HEADS)[None, :]
        out_c = _attention("diff", qd, kd, vd, n_ctx_tiles, lc, extra=(lam_p, sub, g64), lam_init=lam_init)

        avec = jnp.zeros((2, 128), F32)
        avec = avec.at[0, 2 * DN_HEADS:4 * DN_HEADS].set(dn_a_log[layer].reshape(-1))
        avec = avec.at[1, 2 * DN_HEADS:4 * DN_HEADS].set(dn_dt_bias[layer].reshape(-1))
        gq_, gk_, gv_, gkt, gbg, gbgt = _gdn_prep(dqkv, dba, dn_conv_w[layer], avec, g64, n_ctx_tiles)
        o_f, o_b = _gdn_scan(gq_, gk_, gv_, gkt, gbg, gbgt, lc)

        xs, f, logits_t = _outproj(
            layer, xs, out_a, out_b, out_c, o_f, o_b, dz, mod_t,
            w_out[layer].reshape(4, MIXER_W, d).astype(BF16), jnp.tile(dn_out_norm[layer], DN_HEADS)[None, :], g64,
            norm_ffn[layer][None, :], router_w[layer].T, jnp.broadcast_to(router_b[layer][:, None], (N_EXPERTS, 128)),
            n_ctx_tiles)

        pos_t, gate_t, cnt = _route(logits_t, tri)
        ffn = _moe(layer, f, pos_t, gate_t, cnt[:, :, 0], tril, exp_w_up, exp_b_up, exp_w_down, exp_b_down)
        ffn = ffn.reshape(b, s, d)

    return _final(depth - 1, xs, ffn, mod_t, final_norm[None, :], n_ctx_tiles, l)
```

```python
import functools
import math

import numpy as np
import jax
import jax.numpy as jnp
from jax import lax
from jax.experimental import pallas as pl
from jax.experimental.pallas import tpu as pltpu

F32 = jnp.float32
BF16 = jnp.bfloat16
HIGHEST = lax.Precision.HIGHEST

GRID_W = 64
NORM_EPS = 1e-6
ROPE_THETA = 10000.0
HEAD_DIM = 64
GQA_HEADS = 4
GQA_KV_HEADS = 2
GMLP_GROUPS = 4
GMLP_CH = 64
GMLP_CHUNK = 128
DIFF_HEADS = 4
DIFF_DIM = 32
DN_HEADS = 4
DN_DK = 64
DN_DV = 64
DN_CHUNK = 64
N_EXPERTS = 32
TOP_K = 4
SWIGLU_LIMIT = 7.0
SWIGLU_ALPHA = 1.702

MIXER_W = 256
TQ = 256
GDN_PAIR = 2 * DN_CHUNK
MOE_TM = 1024
MOE_BM = 512
MOE_TC = 512
ATT_KC = 1024
ATT_MW = 256
VMEM_LIMIT = 56 * 1024 * 1024

_SEGS = ("aq", "aqs", "ak", "aks", "av", "bu", "bv", "cq", "cqs", "ck", "cks", "cv")
OFF = {name: i * MIXER_W for i, name in enumerate(_SEGS)}
OFF["dqkv"] = len(_SEGS) * MIXER_W
OFF["dz"] = OFF["dqkv"] + 3 * MIXER_W
OFF["dba"] = OFF["dz"] + MIXER_W
W_EXT = OFF["dba"] + 128


def _dot(a, b):
    return jnp.dot(a, b, preferred_element_type=F32)


def _dot_nt(a, b):
    return lax.dot_general(a, b, (((1,), (1,)), ((), ())), preferred_element_type=F32)


def _dot_tn(a, b):
    return lax.dot_general(a, b, (((0,), (0,)), ((), ())), preferred_element_type=F32)


def _split3(x):
    hi = x.astype(BF16)
    r1 = x - hi.astype(F32)
    mid = r1.astype(BF16)
    lo = (r1 - mid.astype(F32)).astype(BF16)
    return hi, mid, lo


def _split2(x):
    hi = x.astype(BF16)
    return hi, (x - hi.astype(F32)).astype(BF16)


def _dot_sel_r(x, sel):
    hi, lo = _split2(x)
    return _dot(hi, sel) + _dot(lo, sel)


def _dot_sel_l(sel, x):
    hi, lo = _split2(x)
    return _dot(sel, hi) + _dot(sel, lo)


def _group_sumsq(x, g_same):
    x2 = x * x
    hi = x2.astype(BF16)
    lo = (x2 - hi.astype(F32)).astype(BF16)
    return _dot(hi, g_same) + _dot(lo, g_same)


def _onehot(cond):
    return jnp.where(cond, 1.0, 0.0).astype(BF16)


def _keep(cond, x):
    return jnp.where(cond, x.astype(F32), 0.0).astype(BF16)


def _silu(x):
    return x * jax.nn.sigmoid(x)


def _softplus(x):
    return jnp.maximum(x, 0.0) + jnp.log1p(jnp.exp(-jnp.abs(x)))


def _params(sem):
    return pltpu.CompilerParams(dimension_semantics=sem, vmem_limit_bytes=VMEM_LIMIT)


def _mod_kernel(c_ref, w_ref, b_ref, o_ref):
    s = _silu(c_ref[...])
    o_ref[0, 0] = jnp.dot(s, w_ref[0], precision=HIGHEST, preferred_element_type=F32) + b_ref[0]


def _modulation(c_all, w_ada, b_ada):
    depth, d, _ = w_ada.shape
    r = c_all.shape[0]
    return pl.pallas_call(
        _mod_kernel,
        grid=(depth, 6),
        in_specs=[
            pl.BlockSpec((r, d), lambda l, j: (0, 0)),
            pl.BlockSpec((1, d, d), lambda l, j: (l, 0, j)),
            pl.BlockSpec((1, 1, d), lambda l, j: (l, 0, j)),
        ],
        out_specs=pl.BlockSpec((1, 1, r, d), lambda l, j: (l, j, 0, 0)),
        out_shape=jax.ShapeDtypeStruct((depth, 6, r, d), F32),
        compiler_params=_params(("arbitrary", "arbitrary")),
        name="adaln_mod",
    )(c_all, w_ada, b_ada.reshape(depth, 1, 6 * d))


def _inproj_kernel(*refs, has_prev, scale_a, scale_d):
    if has_prev:
        x_ref, ffn_ref, modp_ref, refs = refs[0], refs[1], refs[2], refs[3:]
    else:
        x_ref, refs = refs[0], refs[1:]
    (mod_ref, gmix_ref, w_ref, vec_ref, g64_ref, rope_ref, ws_ref, bst_ref) = refs[:8]
    outs = refs[8:]
    if has_prev:
        xo_ref, outs = outs[0], outs[1:]
    (qa_ref, ka_ref, va_ref, ob_ref, qd_ref, kd_ref, vd_ref, dqkv_ref, dz_ref, dba_ref) = outs

    x = x_ref[0]
    if has_prev:
        x = x + modp_ref[0, 0, 5:6, :] * ffn_ref[0]
        xo_ref[0] = x
    ms = jnp.mean(x * x, axis=-1, keepdims=True)
    xn = x * lax.rsqrt(ms + NORM_EPS) * gmix_ref[...]
    h = xn * (1.0 + mod_ref[0, 0, 1:2, :]) + mod_ref[0, 0, 0:1, :]
    p = _dot(h.astype(BF16), w_ref[...])

    def seg(name, width=MIXER_W):
        return p[:, OFF[name]:OFF[name] + width]

    g64 = g64_ref[...]
    cos_a, sin_a, cos_d, sin_d = rope_ref[0], rope_ref[1], rope_ref[2], rope_ref[3]

    def norm_rope(x0, xs, gain, gain_s, scale):
        r = lax.rsqrt(_group_sumsq(x0, g64) * (1.0 / HEAD_DIM) + NORM_EPS)
        return ((x0 * r * gain) * cos_a + (xs * r * gain_s) * sin_a) * scale

    qa_ref[0] = norm_rope(seg("aq"), seg("aqs"), vec_ref[0:1, :], vec_ref[1:2, :], scale_a).astype(BF16)
    ka_ref[0] = norm_rope(seg("ak"), seg("aks"), vec_ref[2:3, :], vec_ref[3:4, :], 1.0).astype(BF16)
    va_ref[0] = seg("av").astype(BF16)

    u = jax.nn.gelu(seg("bu"))
    v = jax.nn.gelu(seg("bv"))
    vn = v * lax.rsqrt(jnp.mean(v * v, axis=-1, keepdims=True) + NORM_EPS) * vec_ref[4:5, :]
    lane_grp = lax.broadcasted_iota(jnp.int32, (GMLP_CHUNK, MIXER_W), 1) // GMLP_CH
    for ci in range(TQ // GMLP_CHUNK):
        rows = slice(ci * GMLP_CHUNK, (ci + 1) * GMLP_CHUNK)
        vc = vn[rows]
        sp = bst_ref[...]
        for g in range(GMLP_GROUPS):
            sp = sp + _dot(ws_ref[g], jnp.where(lane_grp == g, vc, 0.0).astype(BF16))
        ob_ref[0, rows, :] = (u[rows] * sp).astype(BF16)

    qd_ref[0] = ((seg("cq") * cos_d + seg("cqs") * sin_d) * scale_d).astype(BF16)
    kd_ref[0] = (seg("ck") * cos_d + seg("cks") * sin_d).astype(BF16)
    vd_ref[0] = seg("cv").astype(BF16)

    dqkv_ref[0] = seg("dqkv", 3 * MIXER_W)
    dz_ref[0] = seg("dz")
    dba_ref[0] = seg("dba", 128)


def _inproj(layer, x, prev, mod_t, gmix, w_ext, vecs, g64, rope, ws, bst, n_ctx_tiles):
    b, s, d = x.shape
    nt = s // TQ
    ctx_row = b

    def mod_map(l):
        return lambda i, bb: (l, jnp.where(i < n_ctx_tiles, ctx_row, bb), 0, 0)

    tok = lambda i, bb: (bb, i, 0)
    const2 = lambda i, bb: (0, 0)
    const3 = lambda i, bb: (0, 0, 0)
    in_specs = [pl.BlockSpec((1, TQ, d), tok)]
    args = [x]
    if prev is not None:
        in_specs += [pl.BlockSpec((1, TQ, d), tok), pl.BlockSpec((1, 1, 6, d), mod_map(layer - 1))]
        args += [prev, mod_t]
    in_specs += [
        pl.BlockSpec((1, 1, 6, d), mod_map(layer)),
        pl.BlockSpec((1, d), const2),
        pl.BlockSpec((d, W_EXT), const2),
        pl.BlockSpec((8, MIXER_W), const2),
        pl.BlockSpec((MIXER_W, MIXER_W), const2),
        pl.BlockSpec((4, TQ, MIXER_W), lambda i, bb: (0, i, 0)),
        pl.BlockSpec((GMLP_GROUPS, GMLP_CHUNK, GMLP_CHUNK), const3),
        pl.BlockSpec((GMLP_CHUNK, MIXER_W), const2),
    ]
    args += [mod_t, gmix, w_ext, vecs, g64, rope, ws, bst]
    bf = lambda w: jax.ShapeDtypeStruct((b, s, w), BF16)
    ff = lambda w: jax.ShapeDtypeStruct((b, s, w), F32)
    out_shape = [bf(MIXER_W)] * 7 + [ff(3 * MIXER_W), ff(MIXER_W), ff(128)]
    out_specs = [pl.BlockSpec((1, TQ, MIXER_W), tok)] * 7 + [
        pl.BlockSpec((1, TQ, 3 * MIXER_W), tok), pl.BlockSpec((1, TQ, MIXER_W), tok), pl.BlockSpec((1, TQ, 128), tok)]
    if prev is not None:
        out_shape = [ff(d)] + out_shape
        out_specs = [pl.BlockSpec((1, TQ, d), tok)] + out_specs
    outs = pl.pallas_call(
        functools.partial(_inproj_kernel, has_prev=prev is not None,
                          scale_a=HEAD_DIM ** -0.5, scale_d=DIFF_DIM ** -0.5),
        grid=(nt, b), in_specs=in_specs, out_specs=out_specs, out_shape=out_shape,
        compiler_params=_params(("arbitrary", "arbitrary")), name="in_proj",
    )(*args)
    if prev is not None:
        return outs[0], outs[1:]
    return x, outs


def _ones_outside(v, keep_lanes):
    keep = jnp.where(keep_lanes, 1.0, 0.0)
    return v * keep.astype(BF16) + (1.0 - keep).astype(BF16)


def _key_chunks(n_keys):
    cuts = list(range(0, n_keys, ATT_KC)) + [n_keys]
    return list(zip(cuts[:-1], cuts[1:]))


def _attend_groups(qms, k_ref, v1_ref, n_keys, s_scr):
    chunks = _key_chunks(n_keys)

    def scores(g, lo, hi, mx):
        s_c = _dot_nt(qms[g], k_ref[0, lo:hi, :])
        s_scr[g % 2, :, lo:hi] = s_c
        for a in range(lo, hi, ATT_MW):
            piece = s_c[:, a - lo:a - lo + ATT_MW]
            mx = piece if mx is None else jnp.maximum(mx, piece)
        return mx

    def values(g, lo, hi, m, acc):
        e = jnp.exp(s_scr[g % 2, :, lo:hi] - m).astype(BF16)
        part = _dot(e, v1_ref[g, lo:hi, :])
        return part if acc is None else acc + part

    mx = None
    for lo, hi in chunks:
        mx = scores(0, lo, hi, mx)
    outs = []
    for g in range(len(qms)):
        m = jnp.max(mx, axis=-1, keepdims=True)
        mx, acc = None, None
        for lo, hi in chunks:
            if g + 1 < len(qms):
                mx = scores(g + 1, lo, hi, mx)
            acc = values(g, lo, hi, m, acc)
        outs.append(acc / jnp.concatenate([acc[:, 128:], acc[:, :128]], axis=1))
    return outs


def _gqa_kernel(q_ref, k_ref, v_ref, o_ref, v1_scr, s_scr, *, n_ctx_tiles, lc):
    i = pl.program_id(1)
    lane_head = lax.broadcasted_iota(jnp.int32, (TQ, MIXER_W), 1) // HEAD_DIM
    lane_half = lax.broadcasted_iota(jnp.int32, (1, MIXER_W), 1) // 128
    pairs = GQA_HEADS // 2

    @pl.when(i == 0)
    def _():
        for pair in range(pairs):
            v1_scr[pair] = _ones_outside(v_ref[0], lane_half == pair)

    def attend(n_keys):
        q = q_ref[0]
        qms = [jnp.concatenate([_keep(lane_head == 2 * p, q), _keep(lane_head == 2 * p + 1, q)], axis=0)
               for p in range(pairs)]
        outs = _attend_groups(qms, k_ref, v1_scr, n_keys, s_scr)
        acc = jnp.zeros((TQ, MIXER_W), F32)
        for p in range(pairs):
            acc = jnp.where(lane_head == 2 * p, outs[p][:TQ], jnp.where(lane_head == 2 * p + 1, outs[p][TQ:], acc))
        o_ref[0] = acc.astype(BF16)

    @pl.when(i < n_ctx_tiles)
    def _():
        attend(lc)

    @pl.when(i >= n_ctx_tiles)
    def _():
        attend(k_ref.shape[1])


def _diff_kernel(q_ref, k_ref, v_ref, lam_ref, sub_ref, g64_ref, o_ref, v1_scr, s_scr, *, n_ctx_tiles, lc, lam_init):
    i = pl.program_id(1)
    lane = lax.broadcasted_iota(jnp.int32, (TQ, MIXER_W), 1)
    lane_head = lane // (2 * DIFF_DIM)
    lane_map = lane // DIFF_DIM
    head_of_lane = lax.broadcasted_iota(jnp.int32, (1, MIXER_W), 1) // (2 * DIFF_DIM)
    lp = lam_ref[...]
    lam = (jnp.exp(jnp.sum(lp[0:1] * lp[1:2], axis=-1, keepdims=True))
           - jnp.exp(jnp.sum(lp[2:3] * lp[3:4], axis=-1, keepdims=True)) + lam_init)

    @pl.when(i == 0)
    def _():
        for hd in range(DIFF_HEADS):
            v1_scr[hd] = _ones_outside(v_ref[0], head_of_lane == hd)

    def attend(n_keys):
        q = q_ref[0]
        qms = [jnp.concatenate([_keep(lane_map == 2 * hd, q), _keep(lane_map == 2 * hd + 1, q)], axis=0)
               for hd in range(DIFF_HEADS)]
        outs = _attend_groups(qms, k_ref, v1_scr, n_keys, s_scr)
        acc = jnp.zeros((TQ, MIXER_W), F32)
        for hd in range(DIFF_HEADS):
            acc = jnp.where(lane_head == hd, outs[hd][:TQ] - lam * outs[hd][TQ:], acc)
        r = lax.rsqrt(_group_sumsq(acc, g64_ref[...]) * (1.0 / (2 * DIFF_DIM)) + NORM_EPS)
        o_ref[0] = (acc * r * sub_ref[...] * (1.0 - lam_init)).astype(BF16)

    @pl.when(i < n_ctx_tiles)
    def _():
        attend(lc)

    @pl.when(i >= n_ctx_tiles)
    def _():
        attend(k_ref.shape[1])


def _attention(kind, q, k, v, n_ctx_tiles, lc, extra=(), lam_init=0.0):
    b, s, w = q.shape
    nt = s // TQ
    tok = lambda bb, i: (bb, i, 0)
    row = lambda bb, i: (bb, 0, 0)
    in_specs = [pl.BlockSpec((1, TQ, w), tok), pl.BlockSpec((1, s, w), row), pl.BlockSpec((1, s, w), row)]
    if kind == "gqa":
        groups = GQA_HEADS // 2
        body = functools.partial(_gqa_kernel, n_ctx_tiles=n_ctx_tiles, lc=lc)
    else:
        groups = DIFF_HEADS
        body = functools.partial(_diff_kernel, n_ctx_tiles=n_ctx_tiles, lc=lc, lam_init=lam_init)
        in_specs += [pl.BlockSpec(e.shape, lambda bb, i: (0, 0)) for e in extra]
    return pl.pallas_call(
        body, grid=(b, nt), in_specs=in_specs, out_specs=pl.BlockSpec((1, TQ, w), tok),
        out_shape=jax.ShapeDtypeStruct((b, s, w), BF16),
        scratch_shapes=[pltpu.VMEM((groups, s, w), BF16), pltpu.VMEM((2, 2 * TQ, s), F32)],
        compiler_params=_params(("arbitrary", "arbitrary")), name=kind + "_attention",
    )(q, k, v, *extra)


def _gdn_prep_kernel(x_ref, xp_ref, xn_ref, ba_ref, cw_ref, av_ref, g64_ref,
                     q_ref, k_ref, v_ref, kt_ref, bg_ref, bgt_ref, *, n_ctx_tiles, n_tiles):
    i = pl.program_id(0)
    x = x_ref[0]
    has_prev = jnp.where((i != 0) & (i != n_ctx_tiles), 1.0, 0.0)
    has_next = jnp.where((i != n_ctx_tiles - 1) & (i != n_tiles - 1), 1.0, 0.0)
    row = lax.broadcasted_iota(jnp.int32, x.shape, 0)
    x_m1 = jnp.where(row == 0, xp_ref[0, 7:8, :] * has_prev, pltpu.roll(x, 1, 0))
    x_p1 = jnp.where(row == TQ - 1, xn_ref[0, 0:1, :] * has_next, pltpu.roll(x, TQ - 1, 0))
    y = _silu(cw_ref[0:1, :] * x_m1 + cw_ref[1:2, :] * x + cw_ref[2:3, :] * x_p1)
    q, k, v = y[:, :MIXER_W], y[:, MIXER_W:2 * MIXER_W], y[:, 2 * MIXER_W:]
    g64 = g64_ref[...]
    qn = q * lax.rsqrt(_group_sumsq(q, g64) + NORM_EPS) * (DN_DK ** -0.5)
    kn = k * lax.rsqrt(_group_sumsq(k, g64) + NORM_EPS)
    q_ref[0] = qn.astype(BF16)
    k_ref[0] = kn.astype(BF16)
    v_ref[0] = v.astype(BF16)
    kt_ref[0] = kn.T.astype(BF16)
    ba = ba_ref[0]
    lane = lax.broadcasted_iota(jnp.int32, ba.shape, 1)
    beta = jax.nn.sigmoid(ba)
    g = -jnp.exp(av_ref[0:1, :]) * _softplus(ba + av_ref[1:2, :])
    bg = jnp.where(lane < 2 * DN_HEADS, beta, jnp.where(lane < 4 * DN_HEADS, g, 0.0))
    bg_ref[0] = bg
    bgt_ref[0] = bg.T


def _gdn_prep(dqkv, dba, conv_w, avec, g64, n_ctx_tiles):
    b, s, w3 = dqkv.shape
    nt = s // TQ
    nb8 = s // 8
    tok = lambda i, bb: (bb, i, 0)
    tokt = lambda i, bb: (bb, 0, i)
    const2 = lambda i, bb: (0, 0)
    outs = pl.pallas_call(
        functools.partial(_gdn_prep_kernel, n_ctx_tiles=n_ctx_tiles, n_tiles=nt),
        grid=(nt, b),
        in_specs=[
            pl.BlockSpec((1, TQ, w3), tok),
            pl.BlockSpec((1, 8, w3), lambda i, bb: (bb, jnp.maximum(i * (TQ // 8) - 1, 0), 0)),
            pl.BlockSpec((1, 8, w3), lambda i, bb: (bb, jnp.minimum((i + 1) * (TQ // 8), nb8 - 1), 0)),
            pl.BlockSpec((1, TQ, 128), tok),
            pl.BlockSpec((3, w3), const2),
            pl.BlockSpec((2, 128), const2),
            pl.BlockSpec((MIXER_W, MIXER_W), const2),
        ],
        out_specs=[pl.BlockSpec((1, TQ, MIXER_W), tok)] * 3 + [
            pl.BlockSpec((1, MIXER_W, TQ), tokt), pl.BlockSpec((1, TQ, 128), tok), pl.BlockSpec((1, 128, TQ), tokt)],
        out_shape=[jax.ShapeDtypeStruct((b, s, MIXER_W), BF16)] * 3 + [
            jax.ShapeDtypeStruct((b, MIXER_W, s), BF16), jax.ShapeDtypeStruct((b, s, 128), F32),
            jax.ShapeDtypeStruct((b, 128, s), F32)],
        compiler_params=_params(("arbitrary", "arbitrary")), name="gdn_prep",
    )(dqkv, dqkv, dqkv, dba, conv_w, avec, g64)
    return outs


def _gdn_scan_kernel(qf, kf, vf, ktf, bgf, bgtf, qb, kb, vb, ktb, bgb, bgtb, of_ref, ob_ref, st_ref):
    @pl.when(pl.program_id(1) == 0)
    def _():
        st_ref[...] = jnp.zeros_like(st_ref)

    pp, cc, w = GDN_PAIR, DN_CHUNK, MIXER_W
    dirs = (0, 1)
    heads = range(DN_HEADS)
    chains = [(d, hd) for d in dirs for hd in heads]
    q_refs, k_refs, v_refs, kt_refs = (qf, qb), (kf, kb), (vf, vb), (ktf, ktb)
    bg_refs, bgt_refs, o_refs = (bgf, bgb), (bgtf, bgtb), (of_ref, ob_ref)

    ii = lax.broadcasted_iota(jnp.int32, (pp, pp), 0)
    jj = lax.broadcasted_iota(jnp.int32, (pp, pp), 1)
    same = (ii // cc) == (jj // cc)
    incl = (same & (jj <= ii), same & (jj >= ii))
    strict = (same & (jj < ii), same & (jj > ii))
    incl_b = [_onehot(m) for m in incl]
    incl_tb = [_onehot(same & (ii <= jj)), _onehot(same & (ii >= jj))]
    eye = jnp.where(ii == jj, 1.0, 0.0)
    merge_masks = [((ii // (2 * sz)) == (jj // (2 * sz))) & ((ii // sz) != (jj // sz))
                   for sz in (2 ** e for e in range(int(math.log2(cc))))]

    src = lax.broadcasted_iota(jnp.int32, (128, w), 0)
    lane_w = lax.broadcasted_iota(jnp.int32, (128, w), 1)
    src2 = lax.broadcasted_iota(jnp.int32, (128, DN_HEADS * pp), 0)
    lane2 = lax.broadcasted_iota(jnp.int32, (128, DN_HEADS * pp), 1)
    bg = [r[0] for r in bg_refs]
    beta_x = [_dot_sel_r(bg[d], _onehot(src == DN_HEADS * d + lane_w // DN_DV)) for d in dirs]
    g_x = [_dot_sel_r(bg[d], _onehot(src == 2 * DN_HEADS + DN_HEADS * d + lane_w // DN_DV)) for d in dirs]
    g_x2 = [_dot_sel_r(bg[d], _onehot(src2 == 2 * DN_HEADS + DN_HEADS * d + lane2 // pp)) for d in dirs]
    cg_rows = [_dot_sel_r(bgt_refs[d][0], incl_tb[d]) for d in dirs]
    cg_x = [_dot_sel_l(incl_b[d], g_x[d]) for d in dirs]
    cg_x2 = [_dot_sel_l(incl_b[d], g_x2[d]) for d in dirs]

    q = [r[0] for r in q_refs]
    k = [r[0] for r in k_refs]
    kf32 = [t.astype(F32) for t in k]
    e_cg = [jnp.exp(t) for t in cg_x]
    rhs_v = [v_refs[d][0].astype(F32) * beta_x[d] for d in dirs]
    rhs_k = [kf32[d] * (beta_x[d] * e_cg[d]) for d in dirs]
    k_beta = [kf32[d] * beta_x[d] for d in dirs]
    lane_head = lax.broadcasted_iota(jnp.int32, (pp, w), 1) // DN_DV

    kk = [_dot_nt(jnp.where(lane_head == hd, k_beta[d], 0.0).astype(BF16), k[d]) for d, hd in chains]
    qk = [_dot_nt(_keep(lane_head == hd, q[d]), k[d]) for d, hd in chains]
    a, qkd = [], []
    for ci, (d, hd) in enumerate(chains):
        gl = 2 * DN_HEADS + DN_HEADS * d + hd
        diff = jnp.where(incl[d], cg_x2[d][:, hd * pp:(hd + 1) * pp] - cg_rows[d][gl:gl + 1, :], 0.0)
        decay = jnp.where(incl[d], jnp.exp(diff), 0.0)
        a.append(jnp.where(strict[d], kk[ci] * decay, 0.0))
        qkd.append((qk[ci] * decay).astype(BF16))

    t_inv = [eye - jnp.where(merge_masks[0], a_c, 0.0) for a_c in a]
    for mask in merge_masks[1:]:
        tb = [t.astype(BF16) for t in t_inv]
        lm = [_dot(jnp.where(mask, a_c, 0.0).astype(BF16), tb_c).astype(BF16) for a_c, tb_c in zip(a, tb)]
        t_inv = [t - _dot(tb_c, lm_c) for t, tb_c, lm_c in zip(t_inv, tb, lm)]
    tb = [t.astype(BF16) for t in t_inv]
    u_part = [_dot(tb[ci], jnp.where(lane_head == hd, rhs_v[d], 0.0).astype(BF16)) for ci, (d, hd) in enumerate(chains)]
    w_part = [_dot(tb[ci], jnp.where(lane_head == hd, rhs_k[d], 0.0).astype(BF16)) for ci, (d, hd) in enumerate(chains)]
    u_all = [sum(u_part[d * DN_HEADS + hd] for hd in heads) for d in dirs]
    w_all = [sum(w_part[d * DN_HEADS + hd] for hd in heads) for d in dirs]

    st = [st_ref[d] for d in dirs]
    blk = (lax.broadcasted_iota(jnp.int32, (w, w), 0) // DN_DK) == (lax.broadcasted_iota(jnp.int32, (w, w), 1) // DN_DV)
    kt = [r[0] for r in kt_refs]
    order = (((0, cc), (cc, pp)), ((cc, pp), (0, cc)))
    zeros_c = jnp.zeros((cc, w), F32)
    lane_head_c = lax.broadcasted_iota(jnp.int32, (cc, w), 1) // DN_DV

    def place(lo, t):
        return jnp.concatenate([t, zeros_c] if lo == 0 else [zeros_c, t], axis=0)

    nv_acc = [None, None]
    for step in range(2):
        lo = [order[d][step][0] for d in dirs]
        rows = [slice(*order[d][step]) for d in dirs]
        stb = [t.astype(BF16) for t in st]
        w_s = [_dot(w_all[d][rows[d]].astype(BF16), stb[d]) for d in dirs]
        q_s = [_dot(q[d][rows[d]], stb[d]) for d in dirs]
        nv = [u_all[d][rows[d]] - w_s[d] for d in dirs]
        for d in dirs:
            full = place(lo[d], nv[d])
            nv_acc[d] = full if nv_acc[d] is None else nv_acc[d] + full
        nvb = [t.astype(BF16) for t in nv_acc]
        intra = [_dot(qkd[ci][rows[d]], nvb[d]) for ci, (d, hd) in enumerate(chains)]
        last = [order[0][step][1] - 1, order[1][step][0]]
        g_end = [cg_x[d][last[d]:last[d] + 1, :] for d in dirs]
        nvs = [place(lo[d], nv[d] * jnp.exp(g_end[d] - cg_x[d][rows[d]])).astype(BF16) for d in dirs]
        upd = [_dot(kt[d], nvs[d]) for d in dirs]
        for d in dirs:
            o = e_cg[d][rows[d]] * q_s[d]
            for hd in heads:
                o = o + jnp.where(lane_head_c == hd, intra[d * DN_HEADS + hd], 0.0)
            o_refs[d][0, rows[d], :] = o
            st[d] = st[d] * jnp.exp(g_end[d]) + jnp.where(blk, upd[d], 0.0)
    for d in dirs:
        st_ref[d] = st[d]


def _gdn_scan(q, k, v, kt, bg, bgt, lc):
    b, s, w = q.shape
    n_pairs = s // GDN_PAIR
    ncp = lc // GDN_PAIR

    def fwd(bb, i):
        return i

    def bwd(bb, i):
        return jnp.where(i < ncp, ncp - 1 - i, n_pairs - 1 + ncp - i)

    def specs(pos):
        tok = lambda bb, i: (bb, pos(bb, i), 0)
        tokt = lambda bb, i: (bb, 0, pos(bb, i))
        return [pl.BlockSpec((1, GDN_PAIR, w), tok)] * 3 + [
            pl.BlockSpec((1, w, GDN_PAIR), tokt), pl.BlockSpec((1, GDN_PAIR, 128), tok),
            pl.BlockSpec((1, 128, GDN_PAIR), tokt)]

    return pl.pallas_call(
        _gdn_scan_kernel, grid=(b, n_pairs),
        in_specs=specs(fwd) + specs(bwd),
        out_specs=[pl.BlockSpec((1, GDN_PAIR, w), lambda bb, i: (bb, fwd(bb, i), 0)),
                   pl.BlockSpec((1, GDN_PAIR, w), lambda bb, i: (bb, bwd(bb, i), 0))],
        out_shape=[jax.ShapeDtypeStruct((b, s, w), F32)] * 2,
        scratch_shapes=[pltpu.VMEM((2, w, w), F32)],
        compiler_params=_params(("arbitrary", "arbitrary")), name="gdn_scan",
    )(q, k, v, kt, bg, bgt, q, k, v, kt, bg, bgt)


def _outproj_kernel(x_ref, oa_ref, ob_ref, oc_ref, of_ref, obw_ref, dz_ref, mod_ref, w_ref, gout_ref,
                    g64_ref, gffn_ref, wr_ref, br_ref, xo_ref, f_ref, lg_ref):
    parts = [slice(k * (TQ // 2), (k + 1) * (TQ // 2)) for k in range(2)]
    g64 = g64_ref[...]
    o = [of_ref[0, r, :] + obw_ref[0, r, :] for r in parts]
    ssq = [_group_sumsq(t, g64) for t in o]
    od = [(o[k] * lax.rsqrt(ssq[k] * (1.0 / DN_DV) + NORM_EPS) * gout_ref[...] * _silu(dz_ref[0, r, :])).astype(BF16)
          for k, r in enumerate(parts)]
    y = [_dot(oa_ref[0, r, :], w_ref[0]) + _dot(ob_ref[0, r, :], w_ref[1]) + _dot(oc_ref[0, r, :], w_ref[2])
         + _dot(od[k], w_ref[3]) for k, r in enumerate(parts)]
    x = [x_ref[0, r, :] + mod_ref[0, 0, 2:3, :] * y[k] for k, r in enumerate(parts)]
    ms = [jnp.mean(t * t, axis=-1, keepdims=True) for t in x]
    f = [(x[k] * lax.rsqrt(ms[k] + NORM_EPS) * gffn_ref[...]) * (1.0 + mod_ref[0, 0, 4:5, :]) + mod_ref[0, 0, 3:4, :]
         for k in range(2)]
    lg = [lax.dot_general(wr_ref[...], t, (((1,), (1,)), ((), ())), precision=HIGHEST, preferred_element_type=F32)
          + br_ref[:, 0:1] for t in f]
    for k, r in enumerate(parts):
        xo_ref[0, r, :] = x[k]
        f_ref[r, :] = f[k]
        lg_ref[:, r] = lg[k]


def _outproj(layer, x, oa, ob, oc, o_f, o_b, dz, mod_t, w_out4, gout, g64, gffn, wr_t, br, n_ctx_tiles):
    b, s, d = x.shape
    nt = s // TQ
    ctx_row = b
    tok = lambda i, bb: (bb, i, 0)
    const2 = lambda i, bb: (0, 0)
    slab = pl.BlockSpec((1, TQ, MIXER_W), tok)
    return pl.pallas_call(
        _outproj_kernel, grid=(nt, b),
        in_specs=[pl.BlockSpec((1, TQ, d), tok), slab, slab, slab, slab, slab, slab,
                  pl.BlockSpec((1, 1, 6, d), lambda i, bb: (layer, jnp.where(i < n_ctx_tiles, ctx_row, bb), 0, 0)),
                  pl.BlockSpec((4, MIXER_W, d), lambda i, bb: (0, 0, 0)),
                  pl.BlockSpec((1, MIXER_W), const2), pl.BlockSpec((MIXER_W, MIXER_W), const2),
                  pl.BlockSpec((1, d), const2), pl.BlockSpec((N_EXPERTS, d), const2),
                  pl.BlockSpec((N_EXPERTS, 128), const2)],
        out_specs=[pl.BlockSpec((1, TQ, d), tok), pl.BlockSpec((TQ, d), lambda i, bb: (bb * nt + i, 0)),
                   pl.BlockSpec((N_EXPERTS, TQ), lambda i, bb: (0, bb * nt + i))],
        out_shape=[jax.ShapeDtypeStruct((b, s, d), F32), jax.ShapeDtypeStruct((b * s, d), F32),
                   jax.ShapeDtypeStruct((N_EXPERTS, b * s), F32)],
        compiler_params=_params(("arbitrary", "arbitrary")), name="out_proj",
    )(x, oa, ob, oc, o_f, o_b, dz, mod_t, w_out4, gout, g64, gffn, wr_t, br)


def _route_kernel(lg_ref, tri_ref, pos_ref, gate_ref, cnt_ref):
    x = lg_ref[...]
    e_iota = lax.broadcasted_iota(jnp.int32, x.shape, 0).astype(F32)
    work = x
    chosen = jnp.zeros(x.shape, F32)
    top = None
    den = None
    for kk in range(TOP_K):
        m = jnp.max(work, axis=0, keepdims=True)
        idx = jnp.min(jnp.where(work == m, e_iota, float(N_EXPERTS)), axis=0, keepdims=True)
        pick = e_iota == idx
        chosen = jnp.where(pick, 1.0, chosen)
        if kk == 0:
            top = m
            den = jnp.ones_like(m)
        else:
            den = den + jnp.exp(m - top)
        work = jnp.where(pick, -jnp.inf, work)
    sel = chosen > 0.5
    gate_ref[0] = jnp.where(sel, jnp.exp(x - top) / den, 0.0)
    rank = _dot(chosen.astype(BF16), tri_ref[...])
    pos_ref[0] = jnp.where(sel, rank.astype(jnp.int32), -1)
    cnt = jnp.sum(chosen, axis=1, keepdims=True).astype(jnp.int32)
    cnt_ref[0] = jnp.broadcast_to(cnt, cnt_ref.shape[1:])


def _route(logits_t, tri):
    n_exp, t = logits_t.shape
    n_tiles = t // MOE_TM
    return pl.pallas_call(
        _route_kernel, grid=(n_tiles,),
        in_specs=[pl.BlockSpec((n_exp, MOE_TM), lambda i: (0, i)), pl.BlockSpec((MOE_TM, MOE_TM), lambda i: (0, 0))],
        out_specs=[pl.BlockSpec((1, n_exp, MOE_TM), lambda i: (i, 0, 0)),
                   pl.BlockSpec((1, n_exp, MOE_TM), lambda i: (i, 0, 0)),
                   pl.BlockSpec((1, n_exp, 128), lambda i: (i, 0, 0))],
        out_shape=[jax.ShapeDtypeStruct((n_tiles, n_exp, MOE_TM), jnp.int32),
                   jax.ShapeDtypeStruct((n_tiles, n_exp, MOE_TM), F32),
                   jax.ShapeDtypeStruct((n_tiles, n_exp, 128), jnp.int32)],
        compiler_params=_params(("arbitrary",)), name="route",
    )(logits_t, tri)


def _slots_kernel(pos_ref, gate_ref, base_ref, tril_ref, dest_ref, gk_ref):
    pos = pos_ref[0]
    chosen = pos >= 0
    slot = (base_ref[0][:, 0:1] + pos).astype(F32)
    choice = _dot(tril_ref[...], _onehot(chosen))
    gate = gate_ref[0]
    pad = jnp.zeros((8 - TOP_K, pos.shape[1]), F32)
    d_rows, g_rows = [], []
    for kk in range(TOP_K):
        mine = chosen & (choice == float(kk))
        d_rows.append(jnp.sum(jnp.where(mine, slot, 0.0), axis=0, keepdims=True))
        g_rows.append(jnp.sum(jnp.where(mine, gate, 0.0), axis=0, keepdims=True))
    dest_ref[0] = jnp.concatenate(d_rows + [pad], axis=0).astype(jnp.int32)
    gk_ref[0] = jnp.concatenate(g_rows + [pad], axis=0)


def _slots(pos_t, gate_t, base_b, tril):
    n_tiles, n_exp, tm = pos_t.shape
    tile = lambda i: (i, 0, 0)
    return pl.pallas_call(
        _slots_kernel, grid=(n_tiles,),
        in_specs=[pl.BlockSpec((1, n_exp, tm), tile), pl.BlockSpec((1, n_exp, tm), tile),
                  pl.BlockSpec((1, n_exp, 128), tile), pl.BlockSpec((n_exp, n_exp), lambda i: (0, 0))],
        out_specs=[pl.BlockSpec((1, 8, tm), tile), pl.BlockSpec((1, 8, tm), tile)],
        out_shape=[jax.ShapeDtypeStruct((n_tiles, 8, tm), jnp.int32), jax.ShapeDtypeStruct((n_tiles, 8, tm), F32)],
        compiler_params=_params(("arbitrary",)), name="moe_slots",
    )(pos_t, gate_t, base_b, tril)


def _dispatch_kernel(pad_ref, f_ref, dest_ref, xs_ref, zbuf, zsem, sem):
    tm = f_ref.shape[0]
    n_blocks = xs_ref.shape[0] // MOE_BM

    @pl.when(pl.program_id(0) == 0)
    def _():
        zbuf[...] = jnp.zeros_like(zbuf)
        live = pad_ref[2 * N_EXPERTS]

        def zero_block(row):
            return pltpu.make_async_copy(zbuf, xs_ref.at[pl.ds(pl.multiple_of(row, MOE_BM), MOE_BM)], zsem)

        for wait in (False, True):
            for e in range(N_EXPERTS):
                for cond, row in ((pad_ref[N_EXPERTS + e] > 0, pad_ref[e]), (live + e < n_blocks, (live + e) * MOE_BM)):
                    @pl.when(cond)
                    def _():
                        zero_block(row).wait() if wait else zero_block(row).start()

    def rows(tok, carry):
        for kk in range(TOP_K):
            pltpu.make_async_copy(f_ref.at[pl.ds(tok, 1)], xs_ref.at[pl.ds(dest_ref[0, kk, tok], 1)],
                                  sem).start(priority=kk % 2)
        return carry

    lax.fori_loop(0, tm, rows, 0, unroll=8)
    for kk in range(TOP_K):
        pltpu.make_async_copy(f_ref, xs_ref.at[pl.ds(0, tm)], sem).wait()


def _dispatch(pad_info, f, dest, n_slots):
    t, d = f.shape
    n_tiles, _, tm = dest.shape
    grid_spec = pltpu.PrefetchScalarGridSpec(
        num_scalar_prefetch=1, grid=(n_tiles,),
        in_specs=[pl.BlockSpec((tm, d), lambda i, p: (i, 0)),
                  pl.BlockSpec((1, 8, tm), lambda i, p: (i, 0, 0), memory_space=pltpu.SMEM)],
        out_specs=pl.BlockSpec(memory_space=pl.ANY),
        scratch_shapes=[pltpu.VMEM((MOE_BM, d), F32), pltpu.SemaphoreType.DMA(()), pltpu.SemaphoreType.DMA(())],
    )
    return pl.pallas_call(
        _dispatch_kernel, grid_spec=grid_spec, out_shape=jax.ShapeDtypeStruct((n_slots, d), F32),
        compiler_params=_params(("arbitrary",)), name="moe_dispatch",
    )(pad_info, f, dest)


def _experts_kernel(meta_ref, xs_ref, wu_ref, bu_ref, wd_ref, bd_ref, ys_ref, wu_bf, wd_bf):
    i = pl.program_id(0)
    nb = pl.num_programs(0)
    live = i < meta_ref[2 * nb]
    de = wd_bf.shape[0]

    @pl.when(live & (meta_ref[nb + i] > 0))
    def _():
        wu_bf[...] = wu_ref[0, 0].astype(BF16)
        wd_bf[...] = wd_ref[0, 0].astype(BF16)

    @pl.when(live)
    def _():
        hgu = _dot(xs_ref[...].astype(BF16), wu_bf[...]) + bu_ref[0, 0]
        gate = jnp.minimum(hgu[:, :de], SWIGLU_LIMIT)
        up = jnp.clip(hgu[:, de:], -SWIGLU_LIMIT, SWIGLU_LIMIT)
        hid = gate * jax.nn.sigmoid(SWIGLU_ALPHA * gate) * (up + 1.0)
        ys_ref[...] = _dot(hid.astype(BF16), wd_bf[...]) + bd_ref[0, 0]

    @pl.when(jnp.logical_not(live))
    def _():
        ys_ref[...] = jnp.zeros_like(ys_ref)


def _experts(layer, meta, xs, w_up, b_up, w_down, b_down):
    n_slots, d = xs.shape
    nb = n_slots // MOE_BM
    depth, n_exp, _, de2 = w_up.shape
    de = de2 // 2
    blk = lambda i, m: (jnp.minimum(i, m[2 * nb] - 1), 0)
    exp = lambda i, m: (layer, m[i], 0, 0)
    grid_spec = pltpu.PrefetchScalarGridSpec(
        num_scalar_prefetch=1, grid=(nb,),
        in_specs=[pl.BlockSpec((MOE_BM, d), blk),
                  pl.BlockSpec((1, 1, d, de2), exp), pl.BlockSpec((1, 1, 1, de2), exp),
                  pl.BlockSpec((1, 1, de, d), exp), pl.BlockSpec((1, 1, 1, d), exp)],
        out_specs=pl.BlockSpec((MOE_BM, d), lambda i, m: (i, 0)),
        scratch_shapes=[pltpu.VMEM((d, de2), BF16), pltpu.VMEM((de, d), BF16)],
    )
    return pl.pallas_call(
        _experts_kernel, grid_spec=grid_spec, out_shape=jax.ShapeDtypeStruct((n_slots, d), F32),
        compiler_params=_params(("arbitrary",)), name="moe_experts",
    )(meta, xs, w_up, b_up.reshape(depth, n_exp, 1, de2), w_down, b_down.reshape(depth, n_exp, 1, d))


def _combine_kernel(dest_ref, dest_next_ref, gk_ref, ys_ref, o_ref, ybuf, sems):
    tc, d = o_ref.shape
    i = pl.program_id(0)
    n = pl.num_programs(0)
    cur = i % 2

    def fetch(idx_ref, buf):
        def rows(tok, carry):
            for kk in range(TOP_K):
                pltpu.make_async_copy(ys_ref.at[pl.ds(idx_ref[0, kk, tok], 1)], ybuf.at[buf, kk, pl.ds(tok, 1)],
                                      sems.at[buf]).start(priority=kk % 2)
            return carry

        lax.fori_loop(0, tc, rows, 0, unroll=8)

    @pl.when(i == 0)
    def _():
        fetch(dest_ref, 0)

    @pl.when(i + 1 < n)
    def _():
        fetch(dest_next_ref, 1 - cur)

    for kk in range(TOP_K):
        pltpu.make_async_copy(ys_ref.at[pl.ds(0, tc)], ybuf.at[cur, kk], sems.at[cur]).wait()
    ybuf = ybuf.at[cur]

    hi, mid, lo = _split3(gk_ref[0])
    pick = lax.broadcasted_iota(jnp.int32, (8, 128), 0)
    cols = []
    for kk in range(TOP_K):
        sel = _onehot(pick == kk)
        cols.append(_dot_tn(hi, sel) + _dot_tn(mid, sel) + _dot_tn(lo, sel))
    for j in range(d // 128):
        lanes = slice(j * 128, (j + 1) * 128)
        acc = cols[0] * ybuf[0, :, lanes]
        for kk in range(1, TOP_K):
            acc = acc + cols[kk] * ybuf[kk, :, lanes]
        o_ref[:, lanes] = acc


def _combine(dest, gk, ys):
    n_tiles, _, tm = dest.shape
    d = ys.shape[1]
    per = tm // MOE_TC
    n = n_tiles * per
    blk = lambda i: (i // per, 0, i % per)
    nxt = lambda i: blk(jnp.minimum(i + 1, n - 1))
    return pl.pallas_call(
        _combine_kernel, grid=(n,),
        in_specs=[pl.BlockSpec((1, 8, MOE_TC), blk, memory_space=pltpu.SMEM),
                  pl.BlockSpec((1, 8, MOE_TC), nxt, memory_space=pltpu.SMEM), pl.BlockSpec((1, 8, MOE_TC), blk),
                  pl.BlockSpec(memory_space=pl.ANY)],
        out_specs=pl.BlockSpec((MOE_TC, d), lambda i: (i, 0)),
        out_shape=jax.ShapeDtypeStruct((n_tiles * tm, d), F32),
        scratch_shapes=[pltpu.VMEM((2, TOP_K, MOE_TC, d), F32), pltpu.SemaphoreType.DMA((2,))],
        compiler_params=_params(("arbitrary",)), name="moe_combine",
    )(dest, dest, gk, ys)


def _moe(layer, f, pos_t, gate_t, cnt, tril, w_up, b_up, w_down, b_down):
    t = f.shape[0]
    nb = -(-(t * TOP_K + N_EXPERTS * (MOE_BM - 1)) // MOE_BM)
    per_expert = jnp.sum(cnt, axis=0)
    blocks_e = (per_expert + MOE_BM - 1) // MOE_BM
    blk_end = jnp.cumsum(blocks_e)
    blk_start = blk_end - blocks_e
    base = (blk_start * MOE_BM)[None, :] + jnp.cumsum(cnt, axis=0) - cnt
    live = blk_end[-1]
    blk_id = jnp.minimum(jnp.arange(nb, dtype=jnp.int32), live - 1)
    blk_expert = jnp.sum((blk_end[None, :] <= blk_id[:, None]).astype(jnp.int32), axis=1)
    first = (jnp.arange(nb, dtype=jnp.int32) == blk_start[blk_expert]).astype(jnp.int32)
    meta = jnp.concatenate([blk_expert, first, live[None]]).astype(jnp.int32)
    pad_info = jnp.concatenate([jnp.maximum(blk_end - 1, 0) * MOE_BM, (blocks_e > 0).astype(jnp.int32),
                                live[None]]).astype(jnp.int32)
    base_b = jnp.broadcast_to(base[:, :, None], base.shape + (128,)).astype(jnp.int32)

    dest, gk = _slots(pos_t, gate_t, base_b, tril)
    xs = _dispatch(pad_info, f, dest, nb * MOE_BM)
    ys = _experts(layer, meta, xs, w_up, b_up, w_down, b_down)
    return _combine(dest, gk, ys)


def _final_kernel(x_ref, ffn_ref, mod_ref, g_ref, o_ref):
    x = x_ref[0] + mod_ref[0, 0, 5:6, :] * ffn_ref[0]
    ms = jnp.mean(x * x, axis=-1, keepdims=True)
    o_ref[0] = x * lax.rsqrt(ms + NORM_EPS) * g_ref[...]


def _final(layer, x, ffn, mod_t, gain, n_ctx_tiles, l):
    b, s, d = x.shape
    tok = lambda bb, i: (bb, i + n_ctx_tiles, 0)
    return pl.pallas_call(
        _final_kernel, grid=(b, l // TQ),
        in_specs=[pl.BlockSpec((1, TQ, d), tok), pl.BlockSpec((1, TQ, d), tok),
                  pl.BlockSpec((1, 1, 6, d), lambda bb, i: (layer, bb, 0, 0)),
                  pl.BlockSpec((1, d), lambda bb, i: (0, 0))],
        out_specs=pl.BlockSpec((1, TQ, d), lambda bb, i: (bb, i, 0)),
        out_shape=jax.ShapeDtypeStruct((b, l, d), F32),
        compiler_params=_params(("arbitrary", "arbitrary")), name="final_norm",
    )(x, ffn, mod_t, gain)


def _swap_halves(n_groups, width):
    base = np.arange(n_groups * width).reshape(n_groups, width)
    return np.concatenate([base[:, width // 2:], base[:, :width // 2]], axis=1).reshape(-1)


def _in_columns():
    sizes = (GQA_HEADS * HEAD_DIM, GQA_KV_HEADS * HEAD_DIM, GQA_KV_HEADS * HEAD_DIM, MIXER_W, MIXER_W,
             MIXER_W, MIXER_W, MIXER_W, 3 * MIXER_W, MIXER_W, 2 * DN_HEADS, 2 * DN_HEADS)
    starts = np.concatenate([[0], np.cumsum(sizes)[:-1]])
    aq, ak, av, bu, bv, cq, ck, cv, dqkv, dz, db, da = (np.arange(n) + o for n, o in zip(sizes, starts))
    grp = GQA_HEADS // GQA_KV_HEADS
    expand = np.concatenate([np.arange(HEAD_DIM) + (hd // grp) * HEAD_DIM for hd in range(GQA_HEADS)])
    sw64 = _swap_halves(GQA_HEADS, HEAD_DIM)
    sw32 = _swap_halves(2 * DIFF_HEADS, DIFF_DIM)
    ak_x, av_x = ak[expand], av[expand]
    cols = [aq, aq[sw64], ak_x, ak_x[sw64], av_x, bu, bv, cq, cq[sw32], ck, ck[sw32], cv, dqkv, dz,
            db, da, np.full(128 - 4 * DN_HEADS, -1)]
    return np.concatenate(cols), expand, sw64


def _take_cols(w, cols):
    safe = np.where(cols < 0, 0, cols)
    return jnp.where(jnp.asarray(cols >= 0)[None, :], w[:, safe], 0.0)


def _rope_tables(l, lc):
    rows = l // GRID_W
    r_idx, c_idx = np.meshgrid(np.arange(rows), np.arange(GRID_W), indexing="ij")
    row_pos = jnp.asarray(r_idx.reshape(-1), F32)
    col_pos = jnp.asarray(c_idx.reshape(-1), F32)

    def table(dim, reps):
        n = dim // 4
        inv = jnp.power(ROPE_THETA, -jnp.arange(n, dtype=F32) / n)
        ang = jnp.concatenate([row_pos[:, None] * inv, col_pos[:, None] * inv], axis=-1)
        cos, sin = jnp.cos(ang), jnp.sin(ang)
        cos_t = jnp.tile(jnp.concatenate([cos, cos], axis=-1), (1, reps))
        sin_t = jnp.tile(jnp.concatenate([-sin, sin], axis=-1), (1, reps))
        return (jnp.concatenate([jnp.ones((lc, cos_t.shape[1]), F32), cos_t], axis=0),
                jnp.concatenate([jnp.zeros((lc, sin_t.shape[1]), F32), sin_t], axis=0))

    cos_a, sin_a = table(HEAD_DIM, GQA_HEADS)
    cos_d, sin_d = table(DIFF_DIM, 2 * DIFF_HEADS)
    return jnp.stack([cos_a, sin_a, cos_d, sin_d])


def kernel(x, c, ctx, c_ctx, w_ada, b_ada, norm_mix, norm_ffn, w_in, w_out, gqa_q_norm, gqa_k_norm, gmlp_v_norm, gmlp_w_s, gmlp_b_s, diff_lambda_q1, diff_lambda_k1, diff_lambda_q2, diff_lambda_k2, diff_subln, dn_conv_w, dn_a_log, dn_dt_bias, dn_out_norm, router_w, router_b, exp_w_up, exp_b_up, exp_w_down, exp_b_down, final_norm):
    b, l, d = x.shape
    lc = ctx.shape[1]
    depth = w_ada.shape[0]
    s = lc + l
    assert lc % TQ == 0 and l % TQ == 0 and (b * s) % MOE_TM == 0 and l % GRID_W == 0
    n_ctx_tiles = lc // TQ

    rows = -(-(b + 1) // 8) * 8
    c_all = jnp.zeros((rows, d), F32).at[:b].set(c).at[b].set(c_ctx)
    mod_t = _modulation(c_all, w_ada, b_ada).transpose(0, 2, 1, 3)

    cols, expand, sw64 = _in_columns()
    rope = _rope_tables(l, lc)
    lane = np.arange(MIXER_W)
    g64 = jnp.asarray((lane[:, None] // 64) == (lane[None, :] // 64), BF16)
    tri = jnp.asarray(np.arange(MOE_TM)[:, None] < np.arange(MOE_TM)[None, :], BF16)
    tril = jnp.asarray(np.arange(N_EXPERTS)[:, None] > np.arange(N_EXPERTS)[None, :], BF16)

    xs = jnp.concatenate([ctx, x], axis=1)
    ffn = None
    for layer in range(depth):
        lam_init = 0.8 - 0.6 * math.exp(-0.3 * layer)
        w_ext = _take_cols(w_in[layer], cols).astype(BF16)
        gq = jnp.tile(gqa_q_norm[layer], GQA_HEADS)
        gk = jnp.tile(gqa_k_norm[layer], GQA_HEADS)
        vecs = jnp.zeros((8, MIXER_W), F32).at[0].set(gq).at[1].set(gq[sw64]).at[2].set(gk).at[3].set(gk[sw64])
        vecs = vecs.at[4].set(gmlp_v_norm[layer])
        bst = jnp.repeat(gmlp_b_s[layer].T, GMLP_CH, axis=1)
        xs, (qa, ka, va, out_b, qd, kd, vd, dqkv, dz, dba) = _inproj(
            layer, xs, ffn, mod_t, norm_mix[layer][None, :], w_ext, vecs, g64, rope,
            gmlp_w_s[layer].astype(BF16), bst, n_ctx_tiles)

        out_a = _attention("gqa", qa, ka, va, n_ctx_tiles, lc)
        lam_p = jnp.stack([diff_lambda_q1[layer], diff_lambda_k1[layer], diff_lambda_q2[layer], diff_lambda_k2[layer]])
        sub = jnp.tile(diff_subln[layer], DIFF_HEADS)[None, :]
        out_c = _attention("diff", qd, kd, vd, n_ctx_tiles, lc, extra=(lam_p, sub, g64), lam_init=lam_init)

        avec = jnp.zeros((2, 128), F32)
        avec = avec.at[0, 2 * DN_HEADS:4 * DN_HEADS].set(dn_a_log[layer].reshape(-1))
        avec = avec.at[1, 2 * DN_HEADS:4 * DN_HEADS].set(dn_dt_bias[layer].reshape(-1))
        gq_, gk_, gv_, gkt, gbg, gbgt = _gdn_prep(dqkv, dba, dn_conv_w[layer], avec, g64, n_ctx_tiles)
        o_f, o_b = _gdn_scan(gq_, gk_, gv_, gkt, gbg, gbgt, lc)

        xs, f, logits_t = _outproj(
            layer, xs, out_a, out_b, out_c, o_f, o_b, dz, mod_t,
            w_out[layer].reshape(4, MIXER_W, d).astype(BF16), jnp.tile(dn_out_norm[layer], DN_HEADS)[None, :], g64,
            norm_ffn[layer][None, :], router_w[layer].T, jnp.broadcast_to(router_b[layer][:, None], (N_EXPERTS, 128)),
            n_ctx_tiles)

        pos_t, gate_t, cnt = _route(logits_t, tri)
        ffn = _moe(layer, f, pos_t, gate_t, cnt[:, :, 0], tril, exp_w_up, exp_b_up, exp_w_down, exp_b_down)
        ffn = ffn.reshape(b, s, d)

    return _final(depth - 1, xs, ffn, mod_t, final_norm[None, :], n_ctx_tiles, l)
```

```python
import functools
import math

import numpy as np
import jax
import jax.numpy as jnp
from jax import lax
from jax.experimental import pallas as pl
from jax.experimental.pallas import tpu as pltpu

F32 = jnp.float32
BF16 = jnp.bfloat16
HIGHEST = lax.Precision.HIGHEST

GRID_W = 64
NORM_EPS = 1e-6
ROPE_THETA = 10000.0
HEAD_DIM = 64
GQA_HEADS = 4
GQA_KV_HEADS = 2
GMLP_GROUPS = 4
GMLP_CH = 64
GMLP_CHUNK = 128
DIFF_HEADS = 4
DIFF_DIM = 32
DN_HEADS = 4
DN_DK = 64
DN_DV = 64
DN_CHUNK = 64
N_EXPERTS = 32
TOP_K = 4
SWIGLU_LIMIT = 7.0
SWIGLU_ALPHA = 1.702

MIXER_W = 256
TQ = 256
GDN_PAIR = 2 * DN_CHUNK
MOE_TM = 1024
MOE_BM = 512
MOE_TC = 512
ATT_KC = 1024
ATT_MW = 256
VMEM_LIMIT = 56 * 1024 * 1024

_SEGS = ("aq", "ak", "av", "bu", "bv", "cq", "ck", "cv")
OFF = {name: i * MIXER_W for i, name in enumerate(_SEGS)}
OFF["dqkv"] = len(_SEGS) * MIXER_W
OFF["dz"] = OFF["dqkv"] + 3 * MIXER_W
OFF["dba"] = OFF["dz"] + MIXER_W
W_EXT = OFF["dba"] + 128


def _dot(a, b):
    return jnp.dot(a, b, preferred_element_type=F32)


def _dot_nt(a, b):
    return lax.dot_general(a, b, (((1,), (1,)), ((), ())), preferred_element_type=F32)


def _dot_tn(a, b):
    return lax.dot_general(a, b, (((0,), (0,)), ((), ())), preferred_element_type=F32)


def _split3(x):
    hi = x.astype(BF16)
    r1 = x - hi.astype(F32)
    mid = r1.astype(BF16)
    lo = (r1 - mid.astype(F32)).astype(BF16)
    return hi, mid, lo


def _split2(x):
    hi = x.astype(BF16)
    return hi, (x - hi.astype(F32)).astype(BF16)


def _dot_sel_r(x, sel):
    hi, lo = _split2(x)
    return _dot(hi, sel) + _dot(lo, sel)


def _dot_sel_l(sel, x):
    hi, lo = _split2(x)
    return _dot(sel, hi) + _dot(sel, lo)


def _group_sumsq(x, g_same):
    x2 = x * x
    hi = x2.astype(BF16)
    lo = (x2 - hi.astype(F32)).astype(BF16)
    return _dot(hi, g_same) + _dot(lo, g_same)


def _onehot(cond):
    return jnp.where(cond, 1.0, 0.0).astype(BF16)


def _keep(cond, x):
    return jnp.where(cond, x.astype(F32), 0.0).astype(BF16)


def _silu(x):
    return x * jax.nn.sigmoid(x)


def _softplus(x):
    return jnp.maximum(x, 0.0) + jnp.log1p(jnp.exp(-jnp.abs(x)))


def _params(sem):
    return pltpu.CompilerParams(dimension_semantics=sem, vmem_limit_bytes=VMEM_LIMIT)


def _mod_kernel(c_ref, w_ref, b_ref, o_ref):
    s = _silu(c_ref[...])
    o_ref[0, 0] = jnp.dot(s, w_ref[0], precision=HIGHEST, preferred_element_type=F32) + b_ref[0]


def _modulation(c_all, w_ada, b_ada):
    depth, d, _ = w_ada.shape
    r = c_all.shape[0]
    return pl.pallas_call(
        _mod_kernel,
        grid=(depth, 6),
        in_specs=[
            pl.BlockSpec((r, d), lambda l, j: (0, 0)),
            pl.BlockSpec((1, d, d), lambda l, j: (l, 0, j)),
            pl.BlockSpec((1, 1, d), lambda l, j: (l, 0, j)),
        ],
        out_specs=pl.BlockSpec((1, 1, r, d), lambda l, j: (l, j, 0, 0)),
        out_shape=jax.ShapeDtypeStruct((depth, 6, r, d), F32),
        compiler_params=_params(("arbitrary", "arbitrary")),
        name="adaln_mod",
    )(c_all, w_ada, b_ada.reshape(depth, 1, 6 * d))


def _inproj_kernel(*refs, has_prev, scale_a, scale_d):
    if has_prev:
        x_ref, ffn_ref, modp_ref, refs = refs[0], refs[1], refs[2], refs[3:]
    else:
        x_ref, refs = refs[0], refs[1:]
    (mod_ref, gmix_ref, w_ref, vec_ref, g64_ref, rope_ref, ws_ref, bst_ref) = refs[:8]
    outs = refs[8:]
    if has_prev:
        xo_ref, outs = outs[0], outs[1:]
    (qa_ref, ka_ref, va_ref, ob_ref, qd_ref, kd_ref, vd_ref, dqkv_ref, dz_ref, dba_ref) = outs

    x = x_ref[0]
    if has_prev:
        x = x + modp_ref[0, 0, 5:6, :] * ffn_ref[0]
        xo_ref[0] = x
    ms = jnp.mean(x * x, axis=-1, keepdims=True)
    xn = x * lax.rsqrt(ms + NORM_EPS) * gmix_ref[...]
    h = xn * (1.0 + mod_ref[0, 0, 1:2, :]) + mod_ref[0, 0, 0:1, :]
    p = _dot(h.astype(BF16), w_ref[...])

    def seg(name, width=MIXER_W):
        return p[:, OFF[name]:OFF[name] + width]

    g64 = g64_ref[...]
    cos_a, sin_a, cos_d, sin_d = rope_ref[0], rope_ref[1], rope_ref[2], rope_ref[3]

    lane = lax.broadcasted_iota(jnp.int32, (TQ, MIXER_W), 1)

    def swap_halves(t, width):
        half = width // 2
        return jnp.where(lane % width < half, pltpu.roll(t, MIXER_W - half, 1), pltpu.roll(t, half, 1))

    def norm_rope(x0, gain, scale):
        xn = x0 * lax.rsqrt(_group_sumsq(x0, g64) * (1.0 / HEAD_DIM) + NORM_EPS) * gain
        return (xn * cos_a + swap_halves(xn, HEAD_DIM) * sin_a) * scale

    qa_ref[0] = norm_rope(seg("aq"), vec_ref[0:1, :], scale_a).astype(BF16)
    ka_ref[0] = norm_rope(seg("ak"), vec_ref[1:2, :], 1.0).astype(BF16)
    va_ref[0] = seg("av").astype(BF16)

    u = jax.nn.gelu(seg("bu"))
    v = jax.nn.gelu(seg("bv"))
    vn = v * lax.rsqrt(jnp.mean(v * v, axis=-1, keepdims=True) + NORM_EPS) * vec_ref[2:3, :]
    lane_grp = lax.broadcasted_iota(jnp.int32, (GMLP_CHUNK, MIXER_W), 1) // GMLP_CH
    for ci in range(TQ // GMLP_CHUNK):
        rows = slice(ci * GMLP_CHUNK, (ci + 1) * GMLP_CHUNK)
        vc = vn[rows]
        sp = bst_ref[...]
        for g in range(GMLP_GROUPS):
            sp = sp + _dot(ws_ref[g], jnp.where(lane_grp == g, vc, 0.0).astype(BF16))
        ob_ref[0, rows, :] = (u[rows] * sp).astype(BF16)

    cq, ck = seg("cq"), seg("ck")
    qd_ref[0] = ((cq * cos_d + swap_halves(cq, DIFF_DIM) * sin_d) * scale_d).astype(BF16)
    kd_ref[0] = (ck * cos_d + swap_halves(ck, DIFF_DIM) * sin_d).astype(BF16)
    vd_ref[0] = seg("cv").astype(BF16)

    dqkv_ref[0] = seg("dqkv", 3 * MIXER_W)
    dz_ref[0] = seg("dz")
    dba_ref[0] = seg("dba", 128)


def _inproj(layer, x, prev, mod_t, gmix, w_ext, vecs, g64, rope, ws, bst, n_ctx_tiles):
    b, s, d = x.shape
    nt = s // TQ
    ctx_row = b

    def mod_map(l):
        return lambda i, bb: (l, jnp.where(i < n_ctx_tiles, ctx_row, bb), 0, 0)

    tok = lambda i, bb: (bb, i, 0)
    const2 = lambda i, bb: (0, 0)
    const3 = lambda i, bb: (0, 0, 0)
    in_specs = [pl.BlockSpec((1, TQ, d), tok)]
    args = [x]
    if prev is not None:
        in_specs += [pl.BlockSpec((1, TQ, d), tok), pl.BlockSpec((1, 1, 6, d), mod_map(layer - 1))]
        args += [prev, mod_t]
    in_specs += [
        pl.BlockSpec((1, 1, 6, d), mod_map(layer)),
        pl.BlockSpec((1, d), const2),
        pl.BlockSpec((d, W_EXT), const2),
        pl.BlockSpec((8, MIXER_W), const2),
        pl.BlockSpec((MIXER_W, MIXER_W), const2),
        pl.BlockSpec((4, TQ, MIXER_W), lambda i, bb: (0, i, 0)),
        pl.BlockSpec((GMLP_GROUPS, GMLP_CHUNK, GMLP_CHUNK), const3),
        pl.BlockSpec((GMLP_CHUNK, MIXER_W), const2),
    ]
    args += [mod_t, gmix, w_ext, vecs, g64, rope, ws, bst]
    bf = lambda w: jax.ShapeDtypeStruct((b, s, w), BF16)
    ff = lambda w: jax.ShapeDtypeStruct((b, s, w), F32)
    out_shape = [bf(MIXER_W)] * 7 + [ff(3 * MIXER_W), ff(MIXER_W), ff(128)]
    out_specs = [pl.BlockSpec((1, TQ, MIXER_W), tok)] * 7 + [
        pl.BlockSpec((1, TQ, 3 * MIXER_W), tok), pl.BlockSpec((1, TQ, MIXER_W), tok), pl.BlockSpec((1, TQ, 128), tok)]
    if prev is not None:
        out_shape = [ff(d)] + out_shape
        out_specs = [pl.BlockSpec((1, TQ, d), tok)] + out_specs
    outs = pl.pallas_call(
        functools.partial(_inproj_kernel, has_prev=prev is not None,
                          scale_a=HEAD_DIM ** -0.5, scale_d=DIFF_DIM ** -0.5),
        grid=(nt, b), in_specs=in_specs, out_specs=out_specs, out_shape=out_shape,
        compiler_params=_params(("arbitrary", "arbitrary")), name="in_proj",
    )(*args)
    if prev is not None:
        return outs[0], outs[1:]
    return x, outs


def _ones_outside(v, keep_lanes):
    keep = jnp.where(keep_lanes, 1.0, 0.0)
    return v * keep.astype(BF16) + (1.0 - keep).astype(BF16)


def _key_chunks(n_keys):
    cuts = list(range(0, n_keys, ATT_KC)) + [n_keys]
    return list(zip(cuts[:-1], cuts[1:]))


def _attend_groups(qms, k_ref, v1_ref, n_keys, s_scr):
    chunks = _key_chunks(n_keys)

    def scores(g, lo, hi, mx):
        s_c = _dot_nt(qms[g], k_ref[0, lo:hi, :])
        s_scr[g % 2, :, lo:hi] = s_c
        for a in range(lo, hi, ATT_MW):
            piece = s_c[:, a - lo:a - lo + ATT_MW]
            mx = piece if mx is None else jnp.maximum(mx, piece)
        return mx

    def values(g, lo, hi, m, acc):
        e = jnp.exp(s_scr[g % 2, :, lo:hi] - m).astype(BF16)
        part = _dot(e, v1_ref[g, lo:hi, :])
        return part if acc is None else acc + part

    mx = None
    for lo, hi in chunks:
        mx = scores(0, lo, hi, mx)
    outs = []
    for g in range(len(qms)):
        m = jnp.max(mx, axis=-1, keepdims=True)
        mx, acc = None, None
        for lo, hi in chunks:
            if g + 1 < len(qms):
                mx = scores(g + 1, lo, hi, mx)
            acc = values(g, lo, hi, m, acc)
        outs.append(acc / jnp.concatenate([acc[:, 128:], acc[:, :128]], axis=1))
    return outs


def _gqa_kernel(q_ref, k_ref, v_ref, o_ref, v1_scr, s_scr, *, n_ctx_tiles, lc):
    i = pl.program_id(1)
    lane_head = lax.broadcasted_iota(jnp.int32, (TQ, MIXER_W), 1) // HEAD_DIM
    lane_half = lax.broadcasted_iota(jnp.int32, (1, MIXER_W), 1) // 128
    pairs = GQA_HEADS // 2

    @pl.when(i == 0)
    def _():
        for pair in range(pairs):
            v1_scr[pair] = _ones_outside(v_ref[0], lane_half == pair)

    def attend(n_keys):
        q = q_ref[0]
        qms = [jnp.concatenate([_keep(lane_head == 2 * p, q), _keep(lane_head == 2 * p + 1, q)], axis=0)
               for p in range(pairs)]
        outs = _attend_groups(qms, k_ref, v1_scr, n_keys, s_scr)
        acc = jnp.zeros((TQ, MIXER_W), F32)
        for p in range(pairs):
            acc = jnp.where(lane_head == 2 * p, outs[p][:TQ], jnp.where(lane_head == 2 * p + 1, outs[p][TQ:], acc))
        o_ref[0] = acc.astype(BF16)

    @pl.when(i < n_ctx_tiles)
    def _():
        attend(lc)

    @pl.when(i >= n_ctx_tiles)
    def _():
        attend(k_ref.shape[1])


def _diff_kernel(q_ref, k_ref, v_ref, lam_ref, sub_ref, g64_ref, o_ref, v1_scr, s_scr, *, n_ctx_tiles, lc, lam_init):
    i = pl.program_id(1)
    lane = lax.broadcasted_iota(jnp.int32, (TQ, MIXER_W), 1)
    lane_head = lane // (2 * DIFF_DIM)
    lane_map = lane // DIFF_DIM
    head_of_lane = lax.broadcasted_iota(jnp.int32, (1, MIXER_W), 1) // (2 * DIFF_DIM)
    lp = lam_ref[...]
    lam = (jnp.exp(jnp.sum(lp[0:1] * lp[1:2], axis=-1, keepdims=True))
           - jnp.exp(jnp.sum(lp[2:3] * lp[3:4], axis=-1, keepdims=True)) + lam_init)

    @pl.when(i == 0)
    def _():
        for hd in range(DIFF_HEADS):
            v1_scr[hd] = _ones_outside(v_ref[0], head_of_lane == hd)

    def attend(n_keys):
        q = q_ref[0]
        qms = [jnp.concatenate([_keep(lane_map == 2 * hd, q), _keep(lane_map == 2 * hd + 1, q)], axis=0)
               for hd in range(DIFF_HEADS)]
        outs = _attend_groups(qms, k_ref, v1_scr, n_keys, s_scr)
        acc = jnp.zeros((TQ, MIXER_W), F32)
        for hd in range(DIFF_HEADS):
            acc = jnp.where(lane_head == hd, outs[hd][:TQ] - lam * outs[hd][TQ:], acc)
        r = lax.rsqrt(_group_sumsq(acc, g64_ref[...]) * (1.0 / (2 * DIFF_DIM)) + NORM_EPS)
        o_ref[0] = (acc * r * sub_ref[...] * (1.0 - lam_init)).astype(BF16)

    @pl.when(i < n_ctx_tiles)
    def _():
        attend(lc)

    @pl.when(i >= n_ctx_tiles)
    def _():
        attend(k_ref.shape[1])


def _attention(kind, q, k, v, n_ctx_tiles, lc, extra=(), lam_init=0.0):
    b, s, w = q.shape
    nt = s // TQ
    tok = lambda bb, i: (bb, i, 0)
    row = lambda bb, i: (bb, 0, 0)
    in_specs = [pl.BlockSpec((1, TQ, w), tok), pl.BlockSpec((1, s, w), row), pl.BlockSpec((1, s, w), row)]
    if kind == "gqa":
        groups = GQA_HEADS // 2
        body = functools.partial(_gqa_kernel, n_ctx_tiles=n_ctx_tiles, lc=lc)
    else:
        groups = DIFF_HEADS
        body = functools.partial(_diff_kernel, n_ctx_tiles=n_ctx_tiles, lc=lc, lam_init=lam_init)
        in_specs += [pl.BlockSpec(e.shape, lambda bb, i: (0, 0)) for e in extra]
    return pl.pallas_call(
        body, grid=(b, nt), in_specs=in_specs, out_specs=pl.BlockSpec((1, TQ, w), tok),
        out_shape=jax.ShapeDtypeStruct((b, s, w), BF16),
        scratch_shapes=[pltpu.VMEM((groups, s, w), BF16), pltpu.VMEM((2, 2 * TQ, s), F32)],
        compiler_params=_params(("arbitrary", "arbitrary")), name=kind + "_attention",
    )(q, k, v, *extra)


def _gdn_prep_kernel(x_ref, xp_ref, xn_ref, ba_ref, cw_ref, av_ref, g64_ref,
                     q_ref, k_ref, v_ref, kt_ref, bg_ref, bgt_ref, *, n_ctx_tiles, n_tiles):
    i = pl.program_id(0)
    x = x_ref[0]
    has_prev = jnp.where((i != 0) & (i != n_ctx_tiles), 1.0, 0.0)
    has_next = jnp.where((i != n_ctx_tiles - 1) & (i != n_tiles - 1), 1.0, 0.0)
    row = lax.broadcasted_iota(jnp.int32, x.shape, 0)
    x_m1 = jnp.where(row == 0, xp_ref[0, 7:8, :] * has_prev, pltpu.roll(x, 1, 0))
    x_p1 = jnp.where(row == TQ - 1, xn_ref[0, 0:1, :] * has_next, pltpu.roll(x, TQ - 1, 0))
    y = _silu(cw_ref[0:1, :] * x_m1 + cw_ref[1:2, :] * x + cw_ref[2:3, :] * x_p1)
    q, k, v = y[:, :MIXER_W], y[:, MIXER_W:2 * MIXER_W], y[:, 2 * MIXER_W:]
    g64 = g64_ref[...]
    qn = q * lax.rsqrt(_group_sumsq(q, g64) + NORM_EPS) * (DN_DK ** -0.5)
    kn = k * lax.rsqrt(_group_sumsq(k, g64) + NORM_EPS)
    q_ref[0] = qn.astype(BF16)
    k_ref[0] = kn.astype(BF16)
    v_ref[0] = v.astype(BF16)
    kt_ref[0] = kn.T.astype(BF16)
    ba = ba_ref[0]
    lane = lax.broadcasted_iota(jnp.int32, ba.shape, 1)
    beta = jax.nn.sigmoid(ba)
    g = -jnp.exp(av_ref[0:1, :]) * _softplus(ba + av_ref[1:2, :])
    bg = jnp.where(lane < 2 * DN_HEADS, beta, jnp.where(lane < 4 * DN_HEADS, g, 0.0))
    bg_ref[0] = bg
    bgt_ref[0] = bg.T


def _gdn_prep(dqkv, dba, conv_w, avec, g64, n_ctx_tiles):
    b, s, w3 = dqkv.shape
    nt = s // TQ
    nb8 = s // 8
    tok = lambda i, bb: (bb, i, 0)
    tokt = lambda i, bb: (bb, 0, i)
    const2 = lambda i, bb: (0, 0)
    outs = pl.pallas_call(
        functools.partial(_gdn_prep_kernel, n_ctx_tiles=n_ctx_tiles, n_tiles=nt),
        grid=(nt, b),
        in_specs=[
            pl.BlockSpec((1, TQ, w3), tok),
            pl.BlockSpec((1, 8, w3), lambda i, bb: (bb, jnp.maximum(i * (TQ // 8) - 1, 0), 0)),
            pl.BlockSpec((1, 8, w3), lambda i, bb: (bb, jnp.minimum((i + 1) * (TQ // 8), nb8 - 1), 0)),
            pl.BlockSpec((1, TQ, 128), tok),
            pl.BlockSpec((3, w3), const2),
            pl.BlockSpec((2, 128), const2),
            pl.BlockSpec((MIXER_W, MIXER_W), const2),
        ],
        out_specs=[pl.BlockSpec((1, TQ, MIXER_W), tok)] * 3 + [
            pl.BlockSpec((1, MIXER_W, TQ), tokt), pl.BlockSpec((1, TQ, 128), tok), pl.BlockSpec((1, 128, TQ), tokt)],
        out_shape=[jax.ShapeDtypeStruct((b, s, MIXER_W), BF16)] * 3 + [
            jax.ShapeDtypeStruct((b, MIXER_W, s), BF16), jax.ShapeDtypeStruct((b, s, 128), F32),
            jax.ShapeDtypeStruct((b, 128, s), F32)],
        compiler_params=_params(("arbitrary", "arbitrary")), name="gdn_prep",
    )(dqkv, dqkv, dqkv, dba, conv_w, avec, g64)
    return outs


def _gdn_scan_kernel(qf, kf, vf, ktf, bgf, bgtf, qb, kb, vb, ktb, bgb, bgtb, of_ref, ob_ref, st_ref):
    @pl.when(pl.program_id(1) == 0)
    def _():
        st_ref[...] = jnp.zeros_like(st_ref)

    pp, cc, w = GDN_PAIR, DN_CHUNK, MIXER_W
    dirs = (0, 1)
    heads = range(DN_HEADS)
    chains = [(d, hd) for d in dirs for hd in heads]
    q_refs, k_refs, v_refs, kt_refs = (qf, qb), (kf, kb), (vf, vb), (ktf, ktb)
    bg_refs, bgt_refs, o_refs = (bgf, bgb), (bgtf, bgtb), (of_ref, ob_ref)

    ii = lax.broadcasted_iota(jnp.int32, (pp, pp), 0)
    jj = lax.broadcasted_iota(jnp.int32, (pp, pp), 1)
    same = (ii // cc) == (jj // cc)
    incl = (same & (jj <= ii), same & (jj >= ii))
    strict = (same & (jj < ii), same & (jj > ii))
    incl_b = [_onehot(m) for m in incl]
    incl_tb = [_onehot(same & (ii <= jj)), _onehot(same & (ii >= jj))]
    eye = jnp.where(ii == jj, 1.0, 0.0)
    merge_masks = [((ii // (2 * sz)) == (jj // (2 * sz))) & ((ii // sz) != (jj // sz))
                   for sz in (2 ** e for e in range(int(math.log2(cc))))]

    src = lax.broadcasted_iota(jnp.int32, (128, w), 0)
    lane_w = lax.broadcasted_iota(jnp.int32, (128, w), 1)
    src2 = lax.broadcasted_iota(jnp.int32, (128, DN_HEADS * pp), 0)
    lane2 = lax.broadcasted_iota(jnp.int32, (128, DN_HEADS * pp), 1)
    bg = [r[0] for r in bg_refs]
    beta_x = [_dot_sel_r(bg[d], _onehot(src == DN_HEADS * d + lane_w // DN_DV)) for d in dirs]
    g_x = [_dot_sel_r(bg[d], _onehot(src == 2 * DN_HEADS + DN_HEADS * d + lane_w // DN_DV)) for d in dirs]
    g_x2 = [_dot_sel_r(bg[d], _onehot(src2 == 2 * DN_HEADS + DN_HEADS * d + lane2 // pp)) for d in dirs]
    cg_rows = [_dot_sel_r(bgt_refs[d][0], incl_tb[d]) for d in dirs]
    cg_x = [_dot_sel_l(incl_b[d], g_x[d]) for d in dirs]
    cg_x2 = [_dot_sel_l(incl_b[d], g_x2[d]) for d in dirs]

    q = [r[0] for r in q_refs]
    k = [r[0] for r in k_refs]
    kf32 = [t.astype(F32) for t in k]
    e_cg = [jnp.exp(t) for t in cg_x]
    rhs_v = [v_refs[d][0].astype(F32) * beta_x[d] for d in dirs]
    rhs_k = [kf32[d] * (beta_x[d] * e_cg[d]) for d in dirs]
    k_beta = [kf32[d] * beta_x[d] for d in dirs]
    lane_head = lax.broadcasted_iota(jnp.int32, (pp, w), 1) // DN_DV

    kk = [_dot_nt(jnp.where(lane_head == hd, k_beta[d], 0.0).astype(BF16), k[d]) for d, hd in chains]
    qk = [_dot_nt(_keep(lane_head == hd, q[d]), k[d]) for d, hd in chains]
    a, qkd = [], []
    for ci, (d, hd) in enumerate(chains):
        gl = 2 * DN_HEADS + DN_HEADS * d + hd
        diff = jnp.where(incl[d], cg_x2[d][:, hd * pp:(hd + 1) * pp] - cg_rows[d][gl:gl + 1, :], 0.0)
        decay = jnp.where(incl[d], jnp.exp(diff), 0.0)
        a.append(jnp.where(strict[d], kk[ci] * decay, 0.0))
        qkd.append((qk[ci] * decay).astype(BF16))

    t_inv = [eye - jnp.where(merge_masks[0], a_c, 0.0) for a_c in a]
    for mask in merge_masks[1:]:
        tb = [t.astype(BF16) for t in t_inv]
        lm = [_dot(jnp.where(mask, a_c, 0.0).astype(BF16), tb_c).astype(BF16) for a_c, tb_c in zip(a, tb)]
        t_inv = [t - _dot(tb_c, lm_c) for t, tb_c, lm_c in zip(t_inv, tb, lm)]
    tb = [t.astype(BF16) for t in t_inv]
    u_part = [_dot(tb[ci], jnp.where(lane_head == hd, rhs_v[d], 0.0).astype(BF16)) for ci, (d, hd) in enumerate(chains)]
    w_part = [_dot(tb[ci], jnp.where(lane_head == hd, rhs_k[d], 0.0).astype(BF16)) for ci, (d, hd) in enumerate(chains)]
    u_all = [sum(u_part[d * DN_HEADS + hd] for hd in heads) for d in dirs]
    w_all = [sum(w_part[d * DN_HEADS + hd] for hd in heads) for d in dirs]

    st = [st_ref[d] for d in dirs]
    blk = (lax.broadcasted_iota(jnp.int32, (w, w), 0) // DN_DK) == (lax.broadcasted_iota(jnp.int32, (w, w), 1) // DN_DV)
    kt = [r[0] for r in kt_refs]
    order = (((0, cc), (cc, pp)), ((cc, pp), (0, cc)))
    zeros_c = jnp.zeros((cc, w), F32)
    lane_head_c = lax.broadcasted_iota(jnp.int32, (cc, w), 1) // DN_DV

    def place(lo, t):
        return jnp.concatenate([t, zeros_c] if lo == 0 else [zeros_c, t], axis=0)

    nv_acc = [None, None]
    for step in range(2):
        lo = [order[d][step][0] for d in dirs]
        rows = [slice(*order[d][step]) for d in dirs]
        stb = [t.astype(BF16) for t in st]
        w_s = [_dot(w_all[d][rows[d]].astype(BF16), stb[d]) for d in dirs]
        q_s = [_dot(q[d][rows[d]], stb[d]) for d in dirs]
        nv = [u_all[d][rows[d]] - w_s[d] for d in dirs]
        for d in dirs:
            full = place(lo[d], nv[d])
            nv_acc[d] = full if nv_acc[d] is None else nv_acc[d] + full
        nvb = [t.astype(BF16) for t in nv_acc]
        intra = [_dot(qkd[ci][rows[d]], nvb[d]) for ci, (d, hd) in enumerate(chains)]
        last = [order[0][step][1] - 1, order[1][step][0]]
        g_end = [cg_x[d][last[d]:last[d] + 1, :] for d in dirs]
        nvs = [place(lo[d], nv[d] * jnp.exp(g_end[d] - cg_x[d][rows[d]])).astype(BF16) for d in dirs]
        upd = [_dot(kt[d], nvs[d]) for d in dirs]
        for d in dirs:
            o = e_cg[d][rows[d]] * q_s[d]
            for hd in heads:
                o = o + jnp.where(lane_head_c == hd, intra[d * DN_HEADS + hd], 0.0)
            o_refs[d][0, rows[d], :] = o
            st[d] = st[d] * jnp.exp(g_end[d]) + jnp.where(blk, upd[d], 0.0)
    for d in dirs:
        st_ref[d] = st[d]


def _gdn_scan(q, k, v, kt, bg, bgt, lc):
    b, s, w = q.shape
    n_pairs = s // GDN_PAIR
    ncp = lc // GDN_PAIR

    def fwd(bb, i):
        return i

    def bwd(bb, i):
        return jnp.where(i < ncp, ncp - 1 - i, n_pairs - 1 + ncp - i)

    def specs(pos):
        tok = lambda bb, i: (bb, pos(bb, i), 0)
        tokt = lambda bb, i: (bb, 0, pos(bb, i))
        return [pl.BlockSpec((1, GDN_PAIR, w), tok)] * 3 + [
            pl.BlockSpec((1, w, GDN_PAIR), tokt), pl.BlockSpec((1, GDN_PAIR, 128), tok),
            pl.BlockSpec((1, 128, GDN_PAIR), tokt)]

    return pl.pallas_call(
        _gdn_scan_kernel, grid=(b, n_pairs),
        in_specs=specs(fwd) + specs(bwd),
        out_specs=[pl.BlockSpec((1, GDN_PAIR, w), lambda bb, i: (bb, fwd(bb, i), 0)),
                   pl.BlockSpec((1, GDN_PAIR, w), lambda bb, i: (bb, bwd(bb, i), 0))],
        out_shape=[jax.ShapeDtypeStruct((b, s, w), F32)] * 2,
        scratch_shapes=[pltpu.VMEM((2, w, w), F32)],
        compiler_params=_params(("arbitrary", "arbitrary")), name="gdn_scan",
    )(q, k, v, kt, bg, bgt, q, k, v, kt, bg, bgt)


def _outproj_kernel(x_ref, oa_ref, ob_ref, oc_ref, of_ref, obw_ref, dz_ref, mod_ref, w_ref, gout_ref,
                    g64_ref, gffn_ref, wr_ref, br_ref, xo_ref, f_ref, lg_ref):
    parts = [slice(k * (TQ // 2), (k + 1) * (TQ // 2)) for k in range(2)]
    g64 = g64_ref[...]
    o = [of_ref[0, r, :] + obw_ref[0, r, :] for r in parts]
    ssq = [_group_sumsq(t, g64) for t in o]
    od = [(o[k] * lax.rsqrt(ssq[k] * (1.0 / DN_DV) + NORM_EPS) * gout_ref[...] * _silu(dz_ref[0, r, :])).astype(BF16)
          for k, r in enumerate(parts)]
    y = [_dot(oa_ref[0, r, :], w_ref[0]) + _dot(ob_ref[0, r, :], w_ref[1]) + _dot(oc_ref[0, r, :], w_ref[2])
         + _dot(od[k], w_ref[3]) for k, r in enumerate(parts)]
    x = [x_ref[0, r, :] + mod_ref[0, 0, 2:3, :] * y[k] for k, r in enumerate(parts)]
    ms = [jnp.mean(t * t, axis=-1, keepdims=True) for t in x]
    f = [(x[k] * lax.rsqrt(ms[k] + NORM_EPS) * gffn_ref[...]) * (1.0 + mod_ref[0, 0, 4:5, :]) + mod_ref[0, 0, 3:4, :]
         for k in range(2)]
    lg = [lax.dot_general(wr_ref[...], t, (((1,), (1,)), ((), ())), precision=HIGHEST, preferred_element_type=F32)
          + br_ref[:, 0:1] for t in f]
    for k, r in enumerate(parts):
        xo_ref[0, r, :] = x[k]
        f_ref[r, :] = f[k]
        lg_ref[:, r] = lg[k]


def _outproj(layer, x, oa, ob, oc, o_f, o_b, dz, mod_t, w_out4, gout, g64, gffn, wr_t, br, n_ctx_tiles):
    b, s, d = x.shape
    nt = s // TQ
    ctx_row = b
    tok = lambda i, bb: (bb, i, 0)
    const2 = lambda i, bb: (0, 0)
    slab = pl.BlockSpec((1, TQ, MIXER_W), tok)
    return pl.pallas_call(
        _outproj_kernel, grid=(nt, b),
        in_specs=[pl.BlockSpec((1, TQ, d), tok), slab, slab, slab, slab, slab, slab,
                  pl.BlockSpec((1, 1, 6, d), lambda i, bb: (layer, jnp.where(i < n_ctx_tiles, ctx_row, bb), 0, 0)),
                  pl.BlockSpec((4, MIXER_W, d), lambda i, bb: (0, 0, 0)),
                  pl.BlockSpec((1, MIXER_W), const2), pl.BlockSpec((MIXER_W, MIXER_W), const2),
                  pl.BlockSpec((1, d), const2), pl.BlockSpec((N_EXPERTS, d), const2),
                  pl.BlockSpec((N_EXPERTS, 128), const2)],
        out_specs=[pl.BlockSpec((1, TQ, d), tok), pl.BlockSpec((TQ, d), lambda i, bb: (bb * nt + i, 0)),
                   pl.BlockSpec((N_EXPERTS, TQ), lambda i, bb: (0, bb * nt + i))],
        out_shape=[jax.ShapeDtypeStruct((b, s, d), F32), jax.ShapeDtypeStruct((b * s, d), F32),
                   jax.ShapeDtypeStruct((N_EXPERTS, b * s), F32)],
        compiler_params=_params(("arbitrary", "arbitrary")), name="out_proj",
    )(x, oa, ob, oc, o_f, o_b, dz, mod_t, w_out4, gout, g64, gffn, wr_t, br)


def _route_kernel(lg_ref, tri_ref, pos_ref, gate_ref, cnt_ref):
    x = lg_ref[...]
    e_iota = lax.broadcasted_iota(jnp.int32, x.shape, 0).astype(F32)
    work = x
    chosen = jnp.zeros(x.shape, F32)
    top = None
    den = None
    for kk in range(TOP_K):
        m = jnp.max(work, axis=0, keepdims=True)
        idx = jnp.min(jnp.where(work == m, e_iota, float(N_EXPERTS)), axis=0, keepdims=True)
        pick = e_iota == idx
        chosen = jnp.where(pick, 1.0, chosen)
        if kk == 0:
            top = m
            den = jnp.ones_like(m)
        else:
            den = den + jnp.exp(m - top)
        work = jnp.where(pick, -jnp.inf, work)
    sel = chosen > 0.5
    gate_ref[0] = jnp.where(sel, jnp.exp(x - top) / den, 0.0)
    rank = _dot(chosen.astype(BF16), tri_ref[...])
    pos_ref[0] = jnp.where(sel, rank.astype(jnp.int32), -1)
    cnt = jnp.sum(chosen, axis=1, keepdims=True).astype(jnp.int32)
    cnt_ref[0] = jnp.broadcast_to(cnt, cnt_ref.shape[1:])


def _route(logits_t, tri):
    n_exp, t = logits_t.shape
    n_tiles = t // MOE_TM
    return pl.pallas_call(
        _route_kernel, grid=(n_tiles,),
        in_specs=[pl.BlockSpec((n_exp, MOE_TM), lambda i: (0, i)), pl.BlockSpec((MOE_TM, MOE_TM), lambda i: (0, 0))],
        out_specs=[pl.BlockSpec((1, n_exp, MOE_TM), lambda i: (i, 0, 0)),
                   pl.BlockSpec((1, n_exp, MOE_TM), lambda i: (i, 0, 0)),
                   pl.BlockSpec((1, n_exp, 128), lambda i: (i, 0, 0))],
        out_shape=[jax.ShapeDtypeStruct((n_tiles, n_exp, MOE_TM), jnp.int32),
                   jax.ShapeDtypeStruct((n_tiles, n_exp, MOE_TM), F32),
                   jax.ShapeDtypeStruct((n_tiles, n_exp, 128), jnp.int32)],
        compiler_params=_params(("arbitrary",)), name="route",
    )(logits_t, tri)


def _slots_kernel(pos_ref, gate_ref, base_ref, tril_ref, dest_ref, gk_ref):
    pos = pos_ref[0]
    chosen = pos >= 0
    slot = (base_ref[0][:, 0:1] + pos).astype(F32)
    choice = _dot(tril_ref[...], _onehot(chosen))
    gate = gate_ref[0]
    pad = jnp.zeros((8 - TOP_K, pos.shape[1]), F32)
    d_rows, g_rows = [], []
    for kk in range(TOP_K):
        mine = chosen & (choice == float(kk))
        d_rows.append(jnp.sum(jnp.where(mine, slot, 0.0), axis=0, keepdims=True))
        g_rows.append(jnp.sum(jnp.where(mine, gate, 0.0), axis=0, keepdims=True))
    dest_ref[0] = jnp.concatenate(d_rows + [pad], axis=0).astype(jnp.int32)
    gk_ref[0] = jnp.concatenate(g_rows + [pad], axis=0)


def _slots(pos_t, gate_t, base_b, tril):
    n_tiles, n_exp, tm = pos_t.shape
    tile = lambda i: (i, 0, 0)
    return pl.pallas_call(
        _slots_kernel, grid=(n_tiles,),
        in_specs=[pl.BlockSpec((1, n_exp, tm), tile), pl.BlockSpec((1, n_exp, tm), tile),
                  pl.BlockSpec((1, n_exp, 128), tile), pl.BlockSpec((n_exp, n_exp), lambda i: (0, 0))],
        out_specs=[pl.BlockSpec((1, 8, tm), tile), pl.BlockSpec((1, 8, tm), tile)],
        out_shape=[jax.ShapeDtypeStruct((n_tiles, 8, tm), jnp.int32), jax.ShapeDtypeStruct((n_tiles, 8, tm), F32)],
        compiler_params=_params(("arbitrary",)), name="moe_slots",
    )(pos_t, gate_t, base_b, tril)


def _dispatch_kernel(pad_ref, f_ref, dest_ref, xs_ref, zbuf, zsem, sem):
    tm = f_ref.shape[0]
    n_blocks = xs_ref.shape[0] // MOE_BM

    @pl.when(pl.program_id(0) == 0)
    def _():
        zbuf[...] = jnp.zeros_like(zbuf)
        live = pad_ref[2 * N_EXPERTS]

        def zero_block(row):
            return pltpu.make_async_copy(zbuf, xs_ref.at[pl.ds(pl.multiple_of(row, MOE_BM), MOE_BM)], zsem)

        for wait in (False, True):
            for e in range(N_EXPERTS):
                for cond, row in ((pad_ref[N_EXPERTS + e] > 0, pad_ref[e]), (live + e < n_blocks, (live + e) * MOE_BM)):
                    @pl.when(cond)
                    def _():
                        zero_block(row).wait() if wait else zero_block(row).start()

    def rows(tok, carry):
        for kk in range(TOP_K):
            pltpu.make_async_copy(f_ref.at[pl.ds(tok, 1)], xs_ref.at[pl.ds(dest_ref[0, kk, tok], 1)],
                                  sem).start(priority=kk % 2)
        return carry

    lax.fori_loop(0, tm, rows, 0, unroll=8)
    for kk in range(TOP_K):
        pltpu.make_async_copy(f_ref, xs_ref.at[pl.ds(0, tm)], sem).wait()


def _dispatch(pad_info, f, dest, n_slots):
    t, d = f.shape
    n_tiles, _, tm = dest.shape
    grid_spec = pltpu.PrefetchScalarGridSpec(
        num_scalar_prefetch=1, grid=(n_tiles,),
        in_specs=[pl.BlockSpec((tm, d), lambda i, p: (i, 0)),
                  pl.BlockSpec((1, 8, tm), lambda i, p: (i, 0, 0), memory_space=pltpu.SMEM)],
        out_specs=pl.BlockSpec(memory_space=pl.ANY),
        scratch_shapes=[pltpu.VMEM((MOE_BM, d), F32), pltpu.SemaphoreType.DMA(()), pltpu.SemaphoreType.DMA(())],
    )
    return pl.pallas_call(
        _dispatch_kernel, grid_spec=grid_spec, out_shape=jax.ShapeDtypeStruct((n_slots, d), F32),
        compiler_params=_params(("arbitrary",)), name="moe_dispatch",
    )(pad_info, f, dest)


def _experts_kernel(meta_ref, xs_ref, wu_ref, bu_ref, wd_ref, bd_ref, ys_ref, wu_bf, wd_bf):
    i = pl.program_id(0)
    nb = pl.num_programs(0)
    live = i < meta_ref[2 * nb]
    de = wd_bf.shape[0]

    @pl.when(live & (meta_ref[nb + i] > 0))
    def _():
        wu_bf[...] = wu_ref[0, 0].astype(BF16)
        wd_bf[...] = wd_ref[0, 0].astype(BF16)

    @pl.when(live)
    def _():
        hgu = _dot(xs_ref[...].astype(BF16), wu_bf[...]) + bu_ref[0, 0]
        gate = jnp.minimum(hgu[:, :de], SWIGLU_LIMIT)
        up = jnp.clip(hgu[:, de:], -SWIGLU_LIMIT, SWIGLU_LIMIT)
        hid = gate * jax.nn.sigmoid(SWIGLU_ALPHA * gate) * (up + 1.0)
        ys_ref[...] = _dot(hid.astype(BF16), wd_bf[...]) + bd_ref[0, 0]

    @pl.when(jnp.logical_not(live))
    def _():
        ys_ref[...] = jnp.zeros_like(ys_ref)


def _experts(layer, meta, xs, w_up, b_up, w_down, b_down):
    n_slots, d = xs.shape
    nb = n_slots // MOE_BM
    depth, n_exp, _, de2 = w_up.shape
    de = de2 // 2
    blk = lambda i, m: (jnp.minimum(i, m[2 * nb] - 1), 0)
    exp = lambda i, m: (layer, m[i], 0, 0)
    grid_spec = pltpu.PrefetchScalarGridSpec(
        num_scalar_prefetch=1, grid=(nb,),
        in_specs=[pl.BlockSpec((MOE_BM, d), blk),
                  pl.BlockSpec((1, 1, d, de2), exp), pl.BlockSpec((1, 1, 1, de2), exp),
                  pl.BlockSpec((1, 1, de, d), exp), pl.BlockSpec((1, 1, 1, d), exp)],
        out_specs=pl.BlockSpec((MOE_BM, d), lambda i, m: (i, 0)),
        scratch_shapes=[pltpu.VMEM((d, de2), BF16), pltpu.VMEM((de, d), BF16)],
    )
    return pl.pallas_call(
        _experts_kernel, grid_spec=grid_spec, out_shape=jax.ShapeDtypeStruct((n_slots, d), F32),
        compiler_params=_params(("arbitrary",)), name="moe_experts",
    )(meta, xs, w_up, b_up.reshape(depth, n_exp, 1, de2), w_down, b_down.reshape(depth, n_exp, 1, d))


def _combine_kernel(dest_ref, dest_next_ref, gk_ref, ys_ref, o_ref, ybuf, sems):
    tc, d = o_ref.shape
    i = pl.program_id(0)
    n = pl.num_programs(0)
    cur = i % 2

    def fetch(idx_ref, buf):
        def rows(tok, carry):
            for kk in range(TOP_K):
                pltpu.make_async_copy(ys_ref.at[pl.ds(idx_ref[0, kk, tok], 1)], ybuf.at[buf, kk, pl.ds(tok, 1)],
                                      sems.at[buf]).start(priority=kk % 2)
            return carry

        lax.fori_loop(0, tc, rows, 0, unroll=8)

    @pl.when(i == 0)
    def _():
        fetch(dest_ref, 0)

    @pl.when(i + 1 < n)
    def _():
        fetch(dest_next_ref, 1 - cur)

    for kk in range(TOP_K):
        pltpu.make_async_copy(ys_ref.at[pl.ds(0, tc)], ybuf.at[cur, kk], sems.at[cur]).wait()
    ybuf = ybuf.at[cur]

    hi, mid, lo = _split3(gk_ref[0])
    pick = lax.broadcasted_iota(jnp.int32, (8, 128), 0)
    cols = []
    for kk in range(TOP_K):
        sel = _onehot(pick == kk)
        cols.append(_dot_tn(hi, sel) + _dot_tn(mid, sel) + _dot_tn(lo, sel))
    for j in range(d // 128):
        lanes = slice(j * 128, (j + 1) * 128)
        acc = cols[0] * ybuf[0, :, lanes]
        for kk in range(1, TOP_K):
            acc = acc + cols[kk] * ybuf[kk, :, lanes]
        o_ref[:, lanes] = acc


def _combine(dest, gk, ys):
    n_tiles, _, tm = dest.shape
    d = ys.shape[1]
    per = tm // MOE_TC
    n = n_tiles * per
    blk = lambda i: (i // per, 0, i % per)
    nxt = lambda i: blk(jnp.minimum(i + 1, n - 1))
    return pl.pallas_call(
        _combine_kernel, grid=(n,),
        in_specs=[pl.BlockSpec((1, 8, MOE_TC), blk, memory_space=pltpu.SMEM),
                  pl.BlockSpec((1, 8, MOE_TC), nxt, memory_space=pltpu.SMEM), pl.BlockSpec((1, 8, MOE_TC), blk),
                  pl.BlockSpec(memory_space=pl.ANY)],
        out_specs=pl.BlockSpec((MOE_TC, d), lambda i: (i, 0)),
        out_shape=jax.ShapeDtypeStruct((n_tiles * tm, d), F32),
        scratch_shapes=[pltpu.VMEM((2, TOP_K, MOE_TC, d), F32), pltpu.SemaphoreType.DMA((2,))],
        compiler_params=_params(("arbitrary",)), name="moe_combine",
    )(dest, dest, gk, ys)


def _moe(layer, f, pos_t, gate_t, cnt, tril, w_up, b_up, w_down, b_down):
    t = f.shape[0]
    nb = -(-(t * TOP_K + N_EXPERTS * (MOE_BM - 1)) // MOE_BM)
    per_expert = jnp.sum(cnt, axis=0)
    blocks_e = (per_expert + MOE_BM - 1) // MOE_BM
    blk_end = jnp.cumsum(blocks_e)
    blk_start = blk_end - blocks_e
    base = (blk_start * MOE_BM)[None, :] + jnp.cumsum(cnt, axis=0) - cnt
    live = blk_end[-1]
    blk_id = jnp.minimum(jnp.arange(nb, dtype=jnp.int32), live - 1)
    blk_expert = jnp.sum((blk_end[None, :] <= blk_id[:, None]).astype(jnp.int32), axis=1)
    first = (jnp.arange(nb, dtype=jnp.int32) == blk_start[blk_expert]).astype(jnp.int32)
    meta = jnp.concatenate([blk_expert, first, live[None]]).astype(jnp.int32)
    pad_info = jnp.concatenate([jnp.maximum(blk_end - 1, 0) * MOE_BM, (blocks_e > 0).astype(jnp.int32),
                                live[None]]).astype(jnp.int32)
    base_b = jnp.broadcast_to(base[:, :, None], base.shape + (128,)).astype(jnp.int32)

    dest, gk = _slots(pos_t, gate_t, base_b, tril)
    xs = _dispatch(pad_info, f, dest, nb * MOE_BM)
    ys = _experts(layer, meta, xs, w_up, b_up, w_down, b_down)
    return _combine(dest, gk, ys)


def _final_kernel(x_ref, ffn_ref, mod_ref, g_ref, o_ref):
    x = x_ref[0] + mod_ref[0, 0, 5:6, :] * ffn_ref[0]
    ms = jnp.mean(x * x, axis=-1, keepdims=True)
    o_ref[0] = x * lax.rsqrt(ms + NORM_EPS) * g_ref[...]


def _final(layer, x, ffn, mod_t, gain, n_ctx_tiles, l):
    b, s, d = x.shape
    tok = lambda bb, i: (bb, i + n_ctx_tiles, 0)
    return pl.pallas_call(
        _final_kernel, grid=(b, l // TQ),
        in_specs=[pl.BlockSpec((1, TQ, d), tok), pl.BlockSpec((1, TQ, d), tok),
                  pl.BlockSpec((1, 1, 6, d), lambda bb, i: (layer, bb, 0, 0)),
                  pl.BlockSpec((1, d), lambda bb, i: (0, 0))],
        out_specs=pl.BlockSpec((1, TQ, d), lambda bb, i: (bb, i, 0)),
        out_shape=jax.ShapeDtypeStruct((b, l, d), F32),
        compiler_params=_params(("arbitrary", "arbitrary")), name="final_norm",
    )(x, ffn, mod_t, gain)


def _in_columns():
    sizes = (GQA_HEADS * HEAD_DIM, GQA_KV_HEADS * HEAD_DIM, GQA_KV_HEADS * HEAD_DIM, MIXER_W, MIXER_W,
             MIXER_W, MIXER_W, MIXER_W, 3 * MIXER_W, MIXER_W, 2 * DN_HEADS, 2 * DN_HEADS)
    starts = np.concatenate([[0], np.cumsum(sizes)[:-1]])
    aq, ak, av, bu, bv, cq, ck, cv, dqkv, dz, db, da = (np.arange(n) + o for n, o in zip(sizes, starts))
    grp = GQA_HEADS // GQA_KV_HEADS
    expand = np.concatenate([np.arange(HEAD_DIM) + (hd // grp) * HEAD_DIM for hd in range(GQA_HEADS)])
    cols = [aq, ak[expand], av[expand], bu, bv, cq, ck, cv, dqkv, dz, db, da, np.full(128 - 4 * DN_HEADS, -1)]
    return np.concatenate(cols)


def _take_cols(w, cols):
    safe = np.where(cols < 0, 0, cols)
    return jnp.where(jnp.asarray(cols >= 0)[None, :], w[:, safe], 0.0)


def _rope_tables(l, lc):
    rows = l // GRID_W
    r_idx, c_idx = np.meshgrid(np.arange(rows), np.arange(GRID_W), indexing="ij")
    row_pos = jnp.asarray(r_idx.reshape(-1), F32)
    col_pos = jnp.asarray(c_idx.reshape(-1), F32)

    def table(dim, reps):
        n = dim // 4
        inv = jnp.power(ROPE_THETA, -jnp.arange(n, dtype=F32) / n)
        ang = jnp.concatenate([row_pos[:, None] * inv, col_pos[:, None] * inv], axis=-1)
        cos, sin = jnp.cos(ang), jnp.sin(ang)
        cos_t = jnp.tile(jnp.concatenate([cos, cos], axis=-1), (1, reps))
        sin_t = jnp.tile(jnp.concatenate([-sin, sin], axis=-1), (1, reps))
        return (jnp.concatenate([jnp.ones((lc, cos_t.shape[1]), F32), cos_t], axis=0),
                jnp.concatenate([jnp.zeros((lc, sin_t.shape[1]), F32), sin_t], axis=0))

    cos_a, sin_a = table(HEAD_DIM, GQA_HEADS)
    cos_d, sin_d = table(DIFF_DIM, 2 * DIFF_HEADS)
    return jnp.stack([cos_a, sin_a, cos_d, sin_d])


def kernel(x, c, ctx, c_ctx, w_ada, b_ada, norm_mix, norm_ffn, w_in, w_out, gqa_q_norm, gqa_k_norm, gmlp_v_norm, gmlp_w_s, gmlp_b_s, diff_lambda_q1, diff_lambda_k1, diff_lambda_q2, diff_lambda_k2, diff_subln, dn_conv_w, dn_a_log, dn_dt_bias, dn_out_norm, router_w, router_b, exp_w_up, exp_b_up, exp_w_down, exp_b_down, final_norm):
    b, l, d = x.shape
    lc = ctx.shape[1]
    depth = w_ada.shape[0]
    s = lc + l
    assert lc % TQ == 0 and l % TQ == 0 and (b * s) % MOE_TM == 0 and l % GRID_W == 0
    n_ctx_tiles = lc // TQ

    rows = -(-(b + 1) // 8) * 8
    c_all = jnp.zeros((rows, d), F32).at[:b].set(c).at[b].set(c_ctx)
    mod_t = _modulation(c_all, w_ada, b_ada).transpose(0, 2, 1, 3)

    cols = _in_columns()
    rope = _rope_tables(l, lc)
    lane = np.arange(MIXER_W)
    g64 = jnp.asarray((lane[:, None] // 64) == (lane[None, :] // 64), BF16)
    tri = jnp.asarray(np.arange(MOE_TM)[:, None] < np.arange(MOE_TM)[None, :], BF16)
    tril = jnp.asarray(np.arange(N_EXPERTS)[:, None] > np.arange(N_EXPERTS)[None, :], BF16)

    xs = jnp.concatenate([ctx, x], axis=1)
    ffn = None
    for layer in range(depth):
        lam_init = 0.8 - 0.6 * math.exp(-0.3 * layer)
        w_ext = _take_cols(w_in[layer], cols).astype(BF16)
        gq = jnp.tile(gqa_q_norm[layer], GQA_HEADS)
        gk = jnp.tile(gqa_k_norm[layer], GQA_HEADS)
        vecs = jnp.zeros((8, MIXER_W), F32).at[0].set(gq).at[1].set(gk).at[2].set(gmlp_v_norm[layer])
        bst = jnp.repeat(gmlp_b_s[layer].T, GMLP_CH, axis=1)
        xs, (qa, ka, va, out_b, qd, kd, vd, dqkv, dz, dba) = _inproj(
            layer, xs, ffn, mod_t, norm_mix[layer][None, :], w_ext, vecs, g64, rope,
            gmlp_w_s[layer].astype(BF16), bst, n_ctx_tiles)

        out_a = _attention("gqa", qa, ka, va, n_ctx_tiles, lc)
        lam_p = jnp.stack([diff_lambda_q1[layer], diff_lambda_k1[layer], diff_lambda_q2[layer], diff_lambda_k2[layer]])
        sub = jnp.tile(diff_subln[layer], DIFF_HEADS)[None, :]
        out_c = _attention("diff", qd, kd, vd, n_ctx_tiles, lc, extra=(lam_p, sub, g64), lam_init=lam_init)

        avec = jnp.zeros((2, 128), F32)
        avec = avec.at[0, 2 * DN_HEADS:4 * DN_HEADS].set(dn_a_log[layer].reshape(-1))
        avec = avec.at[1, 2 * DN_HEADS:4 * DN_HEADS].set(dn_dt_bias[layer].reshape(-1))
        gq_, gk_, gv_, gkt, gbg, gbgt = _gdn_prep(dqkv, dba, dn_conv_w[layer], avec, g64, n_ctx_tiles)
        o_f, o_b = _gdn_scan(gq_, gk_, gv_, gkt, gbg, gbgt, lc)

        xs, f, logits_t = _outproj(
            layer, xs, out_a, out_b, out_c, o_f, o_b, dz, mod_t,
            w_out[layer].reshape(4, MIXER_W, d).astype(BF16), jnp.tile(dn_out_norm[layer], DN_HEADS)[None, :], g64,
            norm_ffn[layer][None, :], router_w[layer].T, jnp.broadcast_to(router_b[layer][:, None], (N_EXPERTS, 128)),
            n_ctx_tiles)

        pos_t, gate_t, cnt = _route(logits_t, tri)
        ffn = _moe(layer, f, pos_t, gate_t, cnt[:, :, 0], tril, exp_w_up, exp_b_up, exp_w_down, exp_b_down)
        ffn = ffn.reshape(b, s, d)

    return _final(depth - 1, xs, ffn, mod_t, final_norm[None, :], n_ctx_tiles, l)
```

```python
import functools
import math

import numpy as np
import jax
import jax.numpy as jnp
from jax import lax
from jax.experimental import pallas as pl
from jax.experimental.pallas import tpu as pltpu

F32 = jnp.float32
BF16 = jnp.bfloat16
HIGHEST = lax.Precision.HIGHEST

GRID_W = 64
NORM_EPS = 1e-6
ROPE_THETA = 10000.0
HEAD_DIM = 64
GQA_HEADS = 4
GQA_KV_HEADS = 2
GMLP_GROUPS = 4
GMLP_CH = 64
GMLP_CHUNK = 128
DIFF_HEADS = 4
DIFF_DIM = 32
DN_HEADS = 4
DN_DK = 64
DN_DV = 64
DN_CHUNK = 64
N_EXPERTS = 32
TOP_K = 4
SWIGLU_LIMIT = 7.0
SWIGLU_ALPHA = 1.702

MIXER_W = 256
TQ = 256
GDN_PAIR = 2 * DN_CHUNK
MOE_TM = 1024
MOE_BM = 512
MOE_TC = 512
ATT_KC = 1024
ATT_MW = 256
VMEM_LIMIT = 56 * 1024 * 1024

_SEGS = ("aq", "ak", "av", "bu", "bv", "cq", "ck", "cv")
OFF = {name: i * MIXER_W for i, name in enumerate(_SEGS)}
OFF["dqkv"] = len(_SEGS) * MIXER_W
OFF["dz"] = OFF["dqkv"] + 3 * MIXER_W
OFF["dba"] = OFF["dz"] + MIXER_W
W_EXT = OFF["dba"] + 128


def _dot(a, b):
    return jnp.dot(a, b, preferred_element_type=F32)


def _dot_nt(a, b):
    return lax.dot_general(a, b, (((1,), (1,)), ((), ())), preferred_element_type=F32)


def _dot_tn(a, b):
    return lax.dot_general(a, b, (((0,), (0,)), ((), ())), preferred_element_type=F32)


def _split3(x):
    hi = x.astype(BF16)
    r1 = x - hi.astype(F32)
    mid = r1.astype(BF16)
    lo = (r1 - mid.astype(F32)).astype(BF16)
    return hi, mid, lo


def _split2(x):
    hi = x.astype(BF16)
    return hi, (x - hi.astype(F32)).astype(BF16)


def _dot_sel_r(x, sel):
    hi, lo = _split2(x)
    return _dot(hi, sel) + _dot(lo, sel)


def _dot_sel_l(sel, x):
    hi, lo = _split2(x)
    return _dot(sel, hi) + _dot(sel, lo)


def _group_sumsq(x, g_same):
    x2 = x * x
    hi = x2.astype(BF16)
    lo = (x2 - hi.astype(F32)).astype(BF16)
    return _dot(hi, g_same) + _dot(lo, g_same)


def _onehot(cond):
    return jnp.where(cond, 1.0, 0.0).astype(BF16)


def _keep(cond, x):
    return jnp.where(cond, x.astype(F32), 0.0).astype(BF16)


def _silu(x):
    return x * jax.nn.sigmoid(x)


def _softplus(x):
    return jnp.maximum(x, 0.0) + jnp.log1p(jnp.exp(-jnp.abs(x)))


def _params(sem):
    return pltpu.CompilerParams(dimension_semantics=sem, vmem_limit_bytes=VMEM_LIMIT)


def _mod_kernel(c_ref, w_ref, b_ref, o_ref):
    s = _silu(c_ref[...])
    o_ref[0, 0] = jnp.dot(s, w_ref[0], precision=HIGHEST, preferred_element_type=F32) + b_ref[0]


def _modulation(c_all, w_ada, b_ada):
    depth, d, _ = w_ada.shape
    r = c_all.shape[0]
    return pl.pallas_call(
        _mod_kernel,
        grid=(depth, 6),
        in_specs=[
            pl.BlockSpec((r, d), lambda l, j: (0, 0)),
            pl.BlockSpec((1, d, d), lambda l, j: (l, 0, j)),
            pl.BlockSpec((1, 1, d), lambda l, j: (l, 0, j)),
        ],
        out_specs=pl.BlockSpec((1, 1, r, d), lambda l, j: (l, j, 0, 0)),
        out_shape=jax.ShapeDtypeStruct((depth, 6, r, d), F32),
        compiler_params=_params(("arbitrary", "arbitrary")),
        name="adaln_mod",
    )(c_all, w_ada, b_ada.reshape(depth, 1, 6 * d))


def _inproj_kernel(*refs, has_prev, scale_a, scale_d):
    if has_prev:
        x_ref, ffn_ref, modp_ref, refs = refs[0], refs[1], refs[2], refs[3:]
    else:
        x_ref, refs = refs[0], refs[1:]
    (mod_ref, gmix_ref, w_ref, vec_ref, g64_ref, rope_ref, ws_ref, bst_ref) = refs[:8]
    outs = refs[8:]
    if has_prev:
        xo_ref, outs = outs[0], outs[1:]
    (qa_ref, ka_ref, va_ref, ob_ref, qd_ref, kd_ref, vd_ref, dqkv_ref, dz_ref, dba_ref) = outs

    x = x_ref[0]
    if has_prev:
        x = x + modp_ref[0, 0, 5:6, :] * ffn_ref[0]
        xo_ref[0] = x
    ms = jnp.mean(x * x, axis=-1, keepdims=True)
    xn = x * lax.rsqrt(ms + NORM_EPS) * gmix_ref[...]
    h = xn * (1.0 + mod_ref[0, 0, 1:2, :]) + mod_ref[0, 0, 0:1, :]
    p = _dot(h.astype(BF16), w_ref[...])

    def seg(name, width=MIXER_W):
        return p[:, OFF[name]:OFF[name] + width]

    g64 = g64_ref[...]
    cos_a, sin_a, cos_d, sin_d = rope_ref[0], rope_ref[1], rope_ref[2], rope_ref[3]

    lane = lax.broadcasted_iota(jnp.int32, (TQ, MIXER_W), 1)

    def swap_halves(t, width):
        half = width // 2
        return jnp.where(lane % width < half, pltpu.roll(t, MIXER_W - half, 1), pltpu.roll(t, half, 1))

    def norm_rope(x0, gain, scale):
        xn = x0 * lax.rsqrt(_group_sumsq(x0, g64) * (1.0 / HEAD_DIM) + NORM_EPS) * gain
        return (xn * cos_a + swap_halves(xn, HEAD_DIM) * sin_a) * scale

    qa_ref[0] = norm_rope(seg("aq"), vec_ref[0:1, :], scale_a).astype(BF16)
    ka_ref[0] = norm_rope(seg("ak"), vec_ref[1:2, :], 1.0).astype(BF16)
    va_ref[0] = seg("av").astype(BF16)

    u = jax.nn.gelu(seg("bu"))
    v = jax.nn.gelu(seg("bv"))
    vn = v * lax.rsqrt(jnp.mean(v * v, axis=-1, keepdims=True) + NORM_EPS) * vec_ref[2:3, :]
    lane_grp = lax.broadcasted_iota(jnp.int32, (GMLP_CHUNK, MIXER_W), 1) // GMLP_CH
    for ci in range(TQ // GMLP_CHUNK):
        rows = slice(ci * GMLP_CHUNK, (ci + 1) * GMLP_CHUNK)
        vc = vn[rows]
        sp = bst_ref[...]
        for g in range(GMLP_GROUPS):
            sp = sp + _dot(ws_ref[g], jnp.where(lane_grp == g, vc, 0.0).astype(BF16))
        ob_ref[0, rows, :] = (u[rows] * sp).astype(BF16)

    cq, ck = seg("cq"), seg("ck")
    qd_ref[0] = ((cq * cos_d + swap_halves(cq, DIFF_DIM) * sin_d) * scale_d).astype(BF16)
    kd_ref[0] = (ck * cos_d + swap_halves(ck, DIFF_DIM) * sin_d).astype(BF16)
    vd_ref[0] = seg("cv").astype(BF16)

    dqkv_ref[0] = seg("dqkv", 3 * MIXER_W)
    dz_ref[0] = seg("dz")
    dba_ref[0] = seg("dba", 128)


def _inproj(layer, x, prev, mod_t, gmix, w_ext, vecs, g64, rope, ws, bst, n_ctx_tiles):
    b, s, d = x.shape
    nt = s // TQ
    ctx_row = b

    def mod_map(l):
        return lambda i, bb: (l, jnp.where(i < n_ctx_tiles, ctx_row, bb), 0, 0)

    tok = lambda i, bb: (bb, i, 0)
    const2 = lambda i, bb: (0, 0)
    const3 = lambda i, bb: (0, 0, 0)
    in_specs = [pl.BlockSpec((1, TQ, d), tok)]
    args = [x]
    if prev is not None:
        in_specs += [pl.BlockSpec((1, TQ, d), tok), pl.BlockSpec((1, 1, 6, d), mod_map(layer - 1))]
        args += [prev, mod_t]
    in_specs += [
        pl.BlockSpec((1, 1, 6, d), mod_map(layer)),
        pl.BlockSpec((1, d), const2),
        pl.BlockSpec((d, W_EXT), const2),
        pl.BlockSpec((8, MIXER_W), const2),
        pl.BlockSpec((MIXER_W, MIXER_W), const2),
        pl.BlockSpec((4, TQ, MIXER_W), lambda i, bb: (0, i, 0)),
        pl.BlockSpec((GMLP_GROUPS, GMLP_CHUNK, GMLP_CHUNK), const3),
        pl.BlockSpec((GMLP_CHUNK, MIXER_W), const2),
    ]
    args += [mod_t, gmix, w_ext, vecs, g64, rope, ws, bst]
    bf = lambda w: jax.ShapeDtypeStruct((b, s, w), BF16)
    ff = lambda w: jax.ShapeDtypeStruct((b, s, w), F32)
    out_shape = [bf(MIXER_W)] * 7 + [ff(3 * MIXER_W), ff(MIXER_W), ff(128)]
    out_specs = [pl.BlockSpec((1, TQ, MIXER_W), tok)] * 7 + [
        pl.BlockSpec((1, TQ, 3 * MIXER_W), tok), pl.BlockSpec((1, TQ, MIXER_W), tok), pl.BlockSpec((1, TQ, 128), tok)]
    if prev is not None:
        out_shape = [ff(d)] + out_shape
        out_specs = [pl.BlockSpec((1, TQ, d), tok)] + out_specs
    outs = pl.pallas_call(
        functools.partial(_inproj_kernel, has_prev=prev is not None,
                          scale_a=HEAD_DIM ** -0.5, scale_d=DIFF_DIM ** -0.5),
        grid=(nt, b), in_specs=in_specs, out_specs=out_specs, out_shape=out_shape,
        compiler_params=_params(("arbitrary", "arbitrary")), name="in_proj",
    )(*args)
    if prev is not None:
        return outs[0], outs[1:]
    return x, outs


def _ones_outside(v, keep_lanes):
    keep = jnp.where(keep_lanes, 1.0, 0.0)
    return v * keep.astype(BF16) + (1.0 - keep).astype(BF16)


def _key_chunks(n_keys):
    cuts = list(range(0, n_keys, ATT_KC)) + [n_keys]
    return list(zip(cuts[:-1], cuts[1:]))


def _attend_groups(qms, k_ref, v1_ref, n_keys, s_scr):
    chunks = _key_chunks(n_keys)

    def scores(g, lo, hi, mx):
        s_c = _dot_nt(qms[g], k_ref[0, lo:hi, :])
        s_scr[g % 2, :, lo:hi] = s_c
        for a in range(lo, hi, ATT_MW):
            piece = s_c[:, a - lo:a - lo + ATT_MW]
            mx = piece if mx is None else jnp.maximum(mx, piece)
        return mx

    def values(g, lo, hi, m, acc):
        e = jnp.exp(s_scr[g % 2, :, lo:hi] - m).astype(BF16)
        part = _dot(e, v1_ref[g, lo:hi, :])
        return part if acc is None else acc + part

    mx = None
    for lo, hi in chunks:
        mx = scores(0, lo, hi, mx)
    outs = []
    for g in range(len(qms)):
        m = jnp.max(mx, axis=-1, keepdims=True)
        mx, acc = None, None
        for lo, hi in chunks:
            if g + 1 < len(qms):
                mx = scores(g + 1, lo, hi, mx)
            acc = values(g, lo, hi, m, acc)
        outs.append(acc / jnp.concatenate([acc[:, 128:], acc[:, :128]], axis=1))
    return outs


def _gqa_kernel(q_ref, k_ref, v_ref, o_ref, v1_scr, s_scr, *, n_ctx_tiles, lc):
    i = pl.program_id(1)
    lane_head = lax.broadcasted_iota(jnp.int32, (TQ, MIXER_W), 1) // HEAD_DIM
    lane_half = lax.broadcasted_iota(jnp.int32, (1, MIXER_W), 1) // 128
    pairs = GQA_HEADS // 2

    @pl.when(i == 0)
    def _():
        for pair in range(pairs):
            v1_scr[pair] = _ones_outside(v_ref[0], lane_half == pair)

    def attend(n_keys):
        q = q_ref[0]
        qms = [jnp.concatenate([_keep(lane_head == 2 * p, q), _keep(lane_head == 2 * p + 1, q)], axis=0)
               for p in range(pairs)]
        outs = _attend_groups(qms, k_ref, v1_scr, n_keys, s_scr)
        acc = jnp.zeros((TQ, MIXER_W), F32)
        for p in range(pairs):
            acc = jnp.where(lane_head == 2 * p, outs[p][:TQ], jnp.where(lane_head == 2 * p + 1, outs[p][TQ:], acc))
        o_ref[0] = acc.astype(BF16)

    @pl.when(i < n_ctx_tiles)
    def _():
        attend(lc)

    @pl.when(i >= n_ctx_tiles)
    def _():
        attend(k_ref.shape[1])


def _diff_kernel(q_ref, k_ref, v_ref, lam_ref, sub_ref, g64_ref, o_ref, v1_scr, s_scr, *, n_ctx_tiles, lc, lam_init):
    i = pl.program_id(1)
    lane = lax.broadcasted_iota(jnp.int32, (TQ, MIXER_W), 1)
    lane_head = lane // (2 * DIFF_DIM)
    lane_map = lane // DIFF_DIM
    head_of_lane = lax.broadcasted_iota(jnp.int32, (1, MIXER_W), 1) // (2 * DIFF_DIM)
    lp = lam_ref[...]
    lam = (jnp.exp(jnp.sum(lp[0:1] * lp[1:2], axis=-1, keepdims=True))
           - jnp.exp(jnp.sum(lp[2:3] * lp[3:4], axis=-1, keepdims=True)) + lam_init)

    @pl.when(i == 0)
    def _():
        for hd in range(DIFF_HEADS):
            v1_scr[hd] = _ones_outside(v_ref[0], head_of_lane == hd)

    def attend(n_keys):
        q = q_ref[0]
        qms = [jnp.concatenate([_keep(lane_map == 2 * hd, q), _keep(lane_map == 2 * hd + 1, q)], axis=0)
               for hd in range(DIFF_HEADS)]
        outs = _attend_groups(qms, k_ref, v1_scr, n_keys, s_scr)
        acc = jnp.zeros((TQ, MIXER_W), F32)
        for hd in range(DIFF_HEADS):
            acc = jnp.where(lane_head == hd, outs[hd][:TQ] - lam * outs[hd][TQ:], acc)
        r = lax.rsqrt(_group_sumsq(acc, g64_ref[...]) * (1.0 / (2 * DIFF_DIM)) + NORM_EPS)
        o_ref[0] = (acc * r * sub_ref[...] * (1.0 - lam_init)).astype(BF16)

    @pl.when(i < n_ctx_tiles)
    def _():
        attend(lc)

    @pl.when(i >= n_ctx_tiles)
    def _():
        attend(k_ref.shape[1])


def _attention(kind, q, k, v, n_ctx_tiles, lc, extra=(), lam_init=0.0):
    b, s, w = q.shape
    nt = s // TQ
    tok = lambda bb, i: (bb, i, 0)
    row = lambda bb, i: (bb, 0, 0)
    in_specs = [pl.BlockSpec((1, TQ, w), tok), pl.BlockSpec((1, s, w), row), pl.BlockSpec((1, s, w), row)]
    if kind == "gqa":
        groups = GQA_HEADS // 2
        body = functools.partial(_gqa_kernel, n_ctx_tiles=n_ctx_tiles, lc=lc)
    else:
        groups = DIFF_HEADS
        body = functools.partial(_diff_kernel, n_ctx_tiles=n_ctx_tiles, lc=lc, lam_init=lam_init)
        in_specs += [pl.BlockSpec(e.shape, lambda bb, i: (0, 0)) for e in extra]
    return pl.pallas_call(
        body, grid=(b, nt), in_specs=in_specs, out_specs=pl.BlockSpec((1, TQ, w), tok),
        out_shape=jax.ShapeDtypeStruct((b, s, w), BF16),
        scratch_shapes=[pltpu.VMEM((groups, s, w), BF16), pltpu.VMEM((2, 2 * TQ, s), F32)],
        compiler_params=_params(("arbitrary", "arbitrary")), name=kind + "_attention",
    )(q, k, v, *extra)


def _gdn_prep_kernel(x_ref, xp_ref, xn_ref, ba_ref, cw_ref, av_ref, g64_ref,
                     q_ref, k_ref, v_ref, kt_ref, bg_ref, bgt_ref, *, n_ctx_tiles, n_tiles):
    i = pl.program_id(0)
    x = x_ref[0]
    has_prev = jnp.where((i != 0) & (i != n_ctx_tiles), 1.0, 0.0)
    has_next = jnp.where((i != n_ctx_tiles - 1) & (i != n_tiles - 1), 1.0, 0.0)
    row = lax.broadcasted_iota(jnp.int32, x.shape, 0)
    x_m1 = jnp.where(row == 0, xp_ref[0, 7:8, :] * has_prev, pltpu.roll(x, 1, 0))
    x_p1 = jnp.where(row == TQ - 1, xn_ref[0, 0:1, :] * has_next, pltpu.roll(x, TQ - 1, 0))
    y = _silu(cw_ref[0:1, :] * x_m1 + cw_ref[1:2, :] * x + cw_ref[2:3, :] * x_p1)
    q, k, v = y[:, :MIXER_W], y[:, MIXER_W:2 * MIXER_W], y[:, 2 * MIXER_W:]
    g64 = g64_ref[...]
    qn = q * lax.rsqrt(_group_sumsq(q, g64) + NORM_EPS) * (DN_DK ** -0.5)
    kn = k * lax.rsqrt(_group_sumsq(k, g64) + NORM_EPS)
    q_ref[0] = qn.astype(BF16)
    k_ref[0] = kn.astype(BF16)
    v_ref[0] = v.astype(BF16)
    kt_ref[0] = kn.T.astype(BF16)
    ba = ba_ref[0]
    lane = lax.broadcasted_iota(jnp.int32, ba.shape, 1)
    beta = jax.nn.sigmoid(ba)
    g = -jnp.exp(av_ref[0:1, :]) * _softplus(ba + av_ref[1:2, :])
    bg = jnp.where(lane < 2 * DN_HEADS, beta, jnp.where(lane < 4 * DN_HEADS, g, 0.0))
    bg_ref[0] = bg
    bgt_ref[0] = bg.T


def _gdn_prep(dqkv, dba, conv_w, avec, g64, n_ctx_tiles):
    b, s, w3 = dqkv.shape
    nt = s // TQ
    nb8 = s // 8
    tok = lambda i, bb: (bb, i, 0)
    tokt = lambda i, bb: (bb, 0, i)
    const2 = lambda i, bb: (0, 0)
    outs = pl.pallas_call(
        functools.partial(_gdn_prep_kernel, n_ctx_tiles=n_ctx_tiles, n_tiles=nt),
        grid=(nt, b),
        in_specs=[
            pl.BlockSpec((1, TQ, w3), tok),
            pl.BlockSpec((1, 8, w3), lambda i, bb: (bb, jnp.maximum(i * (TQ // 8) - 1, 0), 0)),
            pl.BlockSpec((1, 8, w3), lambda i, bb: (bb, jnp.minimum((i + 1) * (TQ // 8), nb8 - 1), 0)),
            pl.BlockSpec((1, TQ, 128), tok),
            pl.BlockSpec((3, w3), const2),
            pl.BlockSpec((2, 128), const2),
            pl.BlockSpec((MIXER_W, MIXER_W), const2),
        ],
        out_specs=[pl.BlockSpec((1, TQ, MIXER_W), tok)] * 3 + [
            pl.BlockSpec((1, MIXER_W, TQ), tokt), pl.BlockSpec((1, TQ, 128), tok), pl.BlockSpec((1, 128, TQ), tokt)],
        out_shape=[jax.ShapeDtypeStruct((b, s, MIXER_W), BF16)] * 3 + [
            jax.ShapeDtypeStruct((b, MIXER_W, s), BF16), jax.ShapeDtypeStruct((b, s, 128), F32),
            jax.ShapeDtypeStruct((b, 128, s), F32)],
        compiler_params=_params(("arbitrary", "arbitrary")), name="gdn_prep",
    )(dqkv, dqkv, dqkv, dba, conv_w, avec, g64)
    return outs


def _gdn_scan_kernel(qf, kf, vf, ktf, bgf, bgtf, qb, kb, vb, ktb, bgb, bgtb, of_ref, ob_ref, st_ref):
    @pl.when(pl.program_id(1) == 0)
    def _():
        st_ref[...] = jnp.zeros_like(st_ref)

    pp, cc, w = GDN_PAIR, DN_CHUNK, MIXER_W
    dirs = (0, 1)
    heads = range(DN_HEADS)
    chains = [(d, hd) for d in dirs for hd in heads]
    q_refs, k_refs, v_refs, kt_refs = (qf, qb), (kf, kb), (vf, vb), (ktf, ktb)
    bg_refs, bgt_refs, o_refs = (bgf, bgb), (bgtf, bgtb), (of_ref, ob_ref)

    ii = lax.broadcasted_iota(jnp.int32, (pp, pp), 0)
    jj = lax.broadcasted_iota(jnp.int32, (pp, pp), 1)
    same = (ii // cc) == (jj // cc)
    incl = (same & (jj <= ii), same & (jj >= ii))
    strict = (same & (jj < ii), same & (jj > ii))
    incl_b = [_onehot(m) for m in incl]
    incl_tb = [_onehot(same & (ii <= jj)), _onehot(same & (ii >= jj))]
    eye = jnp.where(ii == jj, 1.0, 0.0)
    merge_masks = [((ii // (2 * sz)) == (jj // (2 * sz))) & ((ii // sz) != (jj // sz))
                   for sz in (2 ** e for e in range(int(math.log2(cc))))]

    src = lax.broadcasted_iota(jnp.int32, (128, w), 0)
    lane_w = lax.broadcasted_iota(jnp.int32, (128, w), 1)
    src2 = lax.broadcasted_iota(jnp.int32, (128, DN_HEADS * pp), 0)
    lane2 = lax.broadcasted_iota(jnp.int32, (128, DN_HEADS * pp), 1)
    bg = [r[0] for r in bg_refs]
    beta_x = [_dot_sel_r(bg[d], _onehot(src == DN_HEADS * d + lane_w // DN_DV)) for d in dirs]
    g_x = [_dot_sel_r(bg[d], _onehot(src == 2 * DN_HEADS + DN_HEADS * d + lane_w // DN_DV)) for d in dirs]
    g_x2 = [_dot_sel_r(bg[d], _onehot(src2 == 2 * DN_HEADS + DN_HEADS * d + lane2 // pp)) for d in dirs]
    cg_rows = [_dot_sel_r(bgt_refs[d][0], incl_tb[d]) for d in dirs]
    cg_x = [_dot_sel_l(incl_b[d], g_x[d]) for d in dirs]
    cg_x2 = [_dot_sel_l(incl_b[d], g_x2[d]) for d in dirs]

    q = [r[0] for r in q_refs]
    k = [r[0] for r in k_refs]
    kf32 = [t.astype(F32) for t in k]
    e_cg = [jnp.exp(t) for t in cg_x]
    rhs_v = [v_refs[d][0].astype(F32) * beta_x[d] for d in dirs]
    rhs_k = [kf32[d] * (beta_x[d] * e_cg[d]) for d in dirs]
    k_beta = [kf32[d] * beta_x[d] for d in dirs]
    lane_head = lax.broadcasted_iota(jnp.int32, (pp, w), 1) // DN_DV

    kk = [_dot_nt(jnp.where(lane_head == hd, k_beta[d], 0.0).astype(BF16), k[d]) for d, hd in chains]
    qk = [_dot_nt(_keep(lane_head == hd, q[d]), k[d]) for d, hd in chains]
    a, qkd = [], []
    for ci, (d, hd) in enumerate(chains):
        gl = 2 * DN_HEADS + DN_HEADS * d + hd
        diff = jnp.where(incl[d], cg_x2[d][:, hd * pp:(hd + 1) * pp] - cg_rows[d][gl:gl + 1, :], 0.0)
        decay = jnp.where(incl[d], jnp.exp(diff), 0.0)
        a.append(jnp.where(strict[d], kk[ci] * decay, 0.0))
        qkd.append((qk[ci] * decay).astype(BF16))

    t_inv = [eye - jnp.where(merge_masks[0], a_c, 0.0) for a_c in a]
    for mask in merge_masks[1:]:
        tb = [t.astype(BF16) for t in t_inv]
        lm = [_dot(jnp.where(mask, a_c, 0.0).astype(BF16), tb_c).astype(BF16) for a_c, tb_c in zip(a, tb)]
        t_inv = [t - _dot(tb_c, lm_c) for t, tb_c, lm_c in zip(t_inv, tb, lm)]
    tb = [t.astype(BF16) for t in t_inv]
    u_part = [_dot(tb[ci], jnp.where(lane_head == hd, rhs_v[d], 0.0).astype(BF16)) for ci, (d, hd) in enumerate(chains)]
    w_part = [_dot(tb[ci], jnp.where(lane_head == hd, rhs_k[d], 0.0).astype(BF16)) for ci, (d, hd) in enumerate(chains)]
    u_all = [sum(u_part[d * DN_HEADS + hd] for hd in heads) for d in dirs]
    w_all = [sum(w_part[d * DN_HEADS + hd] for hd in heads) for d in dirs]

    st = [st_ref[d] for d in dirs]
    blk = (lax.broadcasted_iota(jnp.int32, (w, w), 0) // DN_DK) == (lax.broadcasted_iota(jnp.int32, (w, w), 1) // DN_DV)
    kt = [r[0] for r in kt_refs]
    order = (((0, cc), (cc, pp)), ((cc, pp), (0, cc)))
    zeros_c = jnp.zeros((cc, w), F32)
    lane_head_c = lax.broadcasted_iota(jnp.int32, (cc, w), 1) // DN_DV

    def place(lo, t):
        return jnp.concatenate([t, zeros_c] if lo == 0 else [zeros_c, t], axis=0)

    nv_acc = [None, None]
    for step in range(2):
        lo = [order[d][step][0] for d in dirs]
        rows = [slice(*order[d][step]) for d in dirs]
        stb = [t.astype(BF16) for t in st]
        w_s = [_dot(w_all[d][rows[d]].astype(BF16), stb[d]) for d in dirs]
        q_s = [_dot(q[d][rows[d]], stb[d]) for d in dirs]
        nv = [u_all[d][rows[d]] - w_s[d] for d in dirs]
        for d in dirs:
            full = place(lo[d], nv[d])
            nv_acc[d] = full if nv_acc[d] is None else nv_acc[d] + full
        nvb = [t.astype(BF16) for t in nv_acc]
        intra = [_dot(qkd[ci][rows[d]], nvb[d]) for ci, (d, hd) in enumerate(chains)]
        last = [order[0][step][1] - 1, order[1][step][0]]
        g_end = [cg_x[d][last[d]:last[d] + 1, :] for d in dirs]
        nvs = [place(lo[d], nv[d] * jnp.exp(g_end[d] - cg_x[d][rows[d]])).astype(BF16) for d in dirs]
        upd = [_dot(kt[d], nvs[d]) for d in dirs]
        for d in dirs:
            o = e_cg[d][rows[d]] * q_s[d]
            for hd in heads:
                o = o + jnp.where(lane_head_c == hd, intra[d * DN_HEADS + hd], 0.0)
            o_refs[d][0, rows[d], :] = o
            st[d] = st[d] * jnp.exp(g_end[d]) + jnp.where(blk, upd[d], 0.0)
    for d in dirs:
        st_ref[d] = st[d]


def _gdn_scan(q, k, v, kt, bg, bgt, lc):
    b, s, w = q.shape
    n_pairs = s // GDN_PAIR
    ncp = lc // GDN_PAIR

    def fwd(bb, i):
        return i

    def bwd(bb, i):
        return jnp.where(i < ncp, ncp - 1 - i, n_pairs - 1 + ncp - i)

    def specs(pos):
        tok = lambda bb, i: (bb, pos(bb, i), 0)
        tokt = lambda bb, i: (bb, 0, pos(bb, i))
        return [pl.BlockSpec((1, GDN_PAIR, w), tok)] * 3 + [
            pl.BlockSpec((1, w, GDN_PAIR), tokt), pl.BlockSpec((1, GDN_PAIR, 128), tok),
            pl.BlockSpec((1, 128, GDN_PAIR), tokt)]

    return pl.pallas_call(
        _gdn_scan_kernel, grid=(b, n_pairs),
        in_specs=specs(fwd) + specs(bwd),
        out_specs=[pl.BlockSpec((1, GDN_PAIR, w), lambda bb, i: (bb, fwd(bb, i), 0)),
                   pl.BlockSpec((1, GDN_PAIR, w), lambda bb, i: (bb, bwd(bb, i), 0))],
        out_shape=[jax.ShapeDtypeStruct((b, s, w), F32)] * 2,
        scratch_shapes=[pltpu.VMEM((2, w, w), F32)],
        compiler_params=_params(("arbitrary", "arbitrary")), name="gdn_scan",
    )(q, k, v, kt, bg, bgt, q, k, v, kt, bg, bgt)


def _outproj_kernel(x_ref, oa_ref, ob_ref, oc_ref, of_ref, obw_ref, dz_ref, mod_ref, w_ref, gout_ref,
                    g64_ref, gffn_ref, wr_ref, br_ref, xo_ref, f_ref, lg_ref):
    parts = [slice(k * (TQ // 2), (k + 1) * (TQ // 2)) for k in range(2)]
    g64 = g64_ref[...]
    o = [of_ref[0, r, :] + obw_ref[0, r, :] for r in parts]
    ssq = [_group_sumsq(t, g64) for t in o]
    od = [(o[k] * lax.rsqrt(ssq[k] * (1.0 / DN_DV) + NORM_EPS) * gout_ref[...] * _silu(dz_ref[0, r, :])).astype(BF16)
          for k, r in enumerate(parts)]
    y = [_dot(oa_ref[0, r, :], w_ref[0]) + _dot(ob_ref[0, r, :], w_ref[1]) + _dot(oc_ref[0, r, :], w_ref[2])
         + _dot(od[k], w_ref[3]) for k, r in enumerate(parts)]
    x = [x_ref[0, r, :] + mod_ref[0, 0, 2:3, :] * y[k] for k, r in enumerate(parts)]
    ms = [jnp.mean(t * t, axis=-1, keepdims=True) for t in x]
    f = [(x[k] * lax.rsqrt(ms[k] + NORM_EPS) * gffn_ref[...]) * (1.0 + mod_ref[0, 0, 4:5, :]) + mod_ref[0, 0, 3:4, :]
         for k in range(2)]
    lg = [lax.dot_general(wr_ref[...], t, (((1,), (1,)), ((), ())), precision=HIGHEST, preferred_element_type=F32)
          + br_ref[:, 0:1] for t in f]
    for k, r in enumerate(parts):
        xo_ref[0, r, :] = x[k]
        f_ref[r, :] = f[k]
        lg_ref[:, r] = lg[k]


def _outproj(layer, x, oa, ob, oc, o_f, o_b, dz, mod_t, w_out4, gout, g64, gffn, wr_t, br, n_ctx_tiles):
    b, s, d = x.shape
    nt = s // TQ
    ctx_row = b
    tok = lambda i, bb: (bb, i, 0)
    const2 = lambda i, bb: (0, 0)
    slab = pl.BlockSpec((1, TQ, MIXER_W), tok)
    return pl.pallas_call(
        _outproj_kernel, grid=(nt, b),
        in_specs=[pl.BlockSpec((1, TQ, d), tok), slab, slab, slab, slab, slab, slab,
                  pl.BlockSpec((1, 1, 6, d), lambda i, bb: (layer, jnp.where(i < n_ctx_tiles, ctx_row, bb), 0, 0)),
                  pl.BlockSpec((4, MIXER_W, d), lambda i, bb: (0, 0, 0)),
                  pl.BlockSpec((1, MIXER_W), const2), pl.BlockSpec((MIXER_W, MIXER_W), const2),
                  pl.BlockSpec((1, d), const2), pl.BlockSpec((N_EXPERTS, d), const2),
                  pl.BlockSpec((N_EXPERTS, 128), const2)],
        out_specs=[pl.BlockSpec((1, TQ, d), tok), pl.BlockSpec((TQ, d), lambda i, bb: (bb * nt + i, 0)),
                   pl.BlockSpec((N_EXPERTS, TQ), lambda i, bb: (0, bb * nt + i))],
        out_shape=[jax.ShapeDtypeStruct((b, s, d), F32), jax.ShapeDtypeStruct((b * s, d), F32),
                   jax.ShapeDtypeStruct((N_EXPERTS, b * s), F32)],
        compiler_params=_params(("arbitrary", "arbitrary")), name="out_proj",
    )(x, oa, ob, oc, o_f, o_b, dz, mod_t, w_out4, gout, g64, gffn, wr_t, br)


def _route_kernel(lg_ref, tri_ref, pos_ref, gate_ref, cnt_ref):
    x = lg_ref[...]
    e_iota = lax.broadcasted_iota(jnp.int32, x.shape, 0).astype(F32)
    work = x
    chosen = jnp.zeros(x.shape, F32)
    top = None
    den = None
    for kk in range(TOP_K):
        m = jnp.max(work, axis=0, keepdims=True)
        idx = jnp.min(jnp.where(work == m, e_iota, float(N_EXPERTS)), axis=0, keepdims=True)
        pick = e_iota == idx
        chosen = jnp.where(pick, 1.0, chosen)
        if kk == 0:
            top = m
            den = jnp.ones_like(m)
        else:
            den = den + jnp.exp(m - top)
        work = jnp.where(pick, -jnp.inf, work)
    sel = chosen > 0.5
    gate_ref[0] = jnp.where(sel, jnp.exp(x - top) / den, 0.0)
    rank = _dot(chosen.astype(BF16), tri_ref[...])
    pos_ref[0] = jnp.where(sel, rank.astype(jnp.int32), -1)
    cnt = jnp.sum(chosen, axis=1, keepdims=True).astype(jnp.int32)
    cnt_ref[0] = jnp.broadcast_to(cnt, cnt_ref.shape[1:])


def _route(logits_t, tri):
    n_exp, t = logits_t.shape
    n_tiles = t // MOE_TM
    return pl.pallas_call(
        _route_kernel, grid=(n_tiles,),
        in_specs=[pl.BlockSpec((n_exp, MOE_TM), lambda i: (0, i)), pl.BlockSpec((MOE_TM, MOE_TM), lambda i: (0, 0))],
        out_specs=[pl.BlockSpec((1, n_exp, MOE_TM), lambda i: (i, 0, 0)),
                   pl.BlockSpec((1, n_exp, MOE_TM), lambda i: (i, 0, 0)),
                   pl.BlockSpec((1, n_exp, 128), lambda i: (i, 0, 0))],
        out_shape=[jax.ShapeDtypeStruct((n_tiles, n_exp, MOE_TM), jnp.int32),
                   jax.ShapeDtypeStruct((n_tiles, n_exp, MOE_TM), F32),
                   jax.ShapeDtypeStruct((n_tiles, n_exp, 128), jnp.int32)],
        compiler_params=_params(("arbitrary",)), name="route",
    )(logits_t, tri)


def _slots_kernel(pos_ref, gate_ref, base_ref, tril_ref, dest_ref, gk_ref):
    pos = pos_ref[0]
    chosen = pos >= 0
    slot = (base_ref[0][:, 0:1] + pos).astype(F32)
    choice = _dot(tril_ref[...], _onehot(chosen))
    gate = gate_ref[0]
    pad = jnp.zeros((8 - TOP_K, pos.shape[1]), F32)
    d_rows, g_rows = [], []
    for kk in range(TOP_K):
        mine = chosen & (choice == float(kk))
        d_rows.append(jnp.sum(jnp.where(mine, slot, 0.0), axis=0, keepdims=True))
        g_rows.append(jnp.sum(jnp.where(mine, gate, 0.0), axis=0, keepdims=True))
    dest_ref[0] = jnp.concatenate(d_rows + [pad], axis=0).astype(jnp.int32)
    gk_ref[0] = jnp.concatenate(g_rows + [pad], axis=0)


def _slots(pos_t, gate_t, base_b, tril):
    n_tiles, n_exp, tm = pos_t.shape
    tile = lambda i: (i, 0, 0)
    return pl.pallas_call(
        _slots_kernel, grid=(n_tiles,),
        in_specs=[pl.BlockSpec((1, n_exp, tm), tile), pl.BlockSpec((1, n_exp, tm), tile),
                  pl.BlockSpec((1, n_exp, 128), tile), pl.BlockSpec((n_exp, n_exp), lambda i: (0, 0))],
        out_specs=[pl.BlockSpec((1, 8, tm), tile), pl.BlockSpec((1, 8, tm), tile)],
        out_shape=[jax.ShapeDtypeStruct((n_tiles, 8, tm), jnp.int32), jax.ShapeDtypeStruct((n_tiles, 8, tm), F32)],
        compiler_params=_params(("arbitrary",)), name="moe_slots",
    )(pos_t, gate_t, base_b, tril)


def _dispatch_kernel(pad_ref, f_ref, dest_ref, xs_ref, zbuf, zsem, sem):
    tm = f_ref.shape[0]
    n_blocks = xs_ref.shape[0] // MOE_BM

    @pl.when(pl.program_id(0) == 0)
    def _():
        zbuf[...] = jnp.zeros_like(zbuf)
        live = pad_ref[2 * N_EXPERTS]

        def zero_block(row):
            return pltpu.make_async_copy(zbuf, xs_ref.at[pl.ds(pl.multiple_of(row, MOE_BM), MOE_BM)], zsem)

        for wait in (False, True):
            for e in range(N_EXPERTS):
                for cond, row in ((pad_ref[N_EXPERTS + e] > 0, pad_ref[e]), (live + e < n_blocks, (live + e) * MOE_BM)):
                    @pl.when(cond)
                    def _():
                        zero_block(row).wait() if wait else zero_block(row).start()

    def rows(tok, carry):
        for kk in range(TOP_K):
            pltpu.make_async_copy(f_ref.at[pl.ds(tok, 1)], xs_ref.at[pl.ds(dest_ref[0, kk, tok], 1)],
                                  sem).start(priority=kk % 2)
        return carry

    lax.fori_loop(0, tm, rows, 0, unroll=8)
    for kk in range(TOP_K):
        pltpu.make_async_copy(f_ref, xs_ref.at[pl.ds(0, tm)], sem).wait()


def _dispatch(pad_info, f, dest, n_slots):
    t, d = f.shape
    n_tiles, _, tm = dest.shape
    grid_spec = pltpu.PrefetchScalarGridSpec(
        num_scalar_prefetch=1, grid=(n_tiles,),
        in_specs=[pl.BlockSpec((tm, d), lambda i, p: (i, 0)),
                  pl.BlockSpec((1, 8, tm), lambda i, p: (i, 0, 0), memory_space=pltpu.SMEM)],
        out_specs=pl.BlockSpec(memory_space=pl.ANY),
        scratch_shapes=[pltpu.VMEM((MOE_BM, d), F32), pltpu.SemaphoreType.DMA(()), pltpu.SemaphoreType.DMA(())],
    )
    return pl.pallas_call(
        _dispatch_kernel, grid_spec=grid_spec, out_shape=jax.ShapeDtypeStruct((n_slots, d), F32),
        compiler_params=_params(("arbitrary",)), name="moe_dispatch",
    )(pad_info, f, dest)


def _experts_kernel(meta_ref, xs_ref, wu_ref, bu_ref, wd_ref, bd_ref, ys_ref, wu_bf, wd_bf):
    i = pl.program_id(0)
    nb = pl.num_programs(0)
    live = i < meta_ref[2 * nb]
    de = wd_bf.shape[0]

    @pl.when(live & (meta_ref[nb + i] > 0))
    def _():
        wu_bf[...] = wu_ref[0, 0].astype(BF16)
        wd_bf[...] = wd_ref[0, 0].astype(BF16)

    @pl.when(live)
    def _():
        hgu = _dot(xs_ref[...].astype(BF16), wu_bf[...]) + bu_ref[0, 0]
        gate = jnp.minimum(hgu[:, :de], SWIGLU_LIMIT)
        up = jnp.clip(hgu[:, de:], -SWIGLU_LIMIT, SWIGLU_LIMIT)
        hid = gate * jax.nn.sigmoid(SWIGLU_ALPHA * gate) * (up + 1.0)
        ys_ref[...] = _dot(hid.astype(BF16), wd_bf[...]) + bd_ref[0, 0]

    @pl.when(jnp.logical_not(live))
    def _():
        ys_ref[...] = jnp.zeros_like(ys_ref)


def _experts(layer, meta, xs, w_up, b_up, w_down, b_down):
    n_slots, d = xs.shape
    nb = n_slots // MOE_BM
    depth, n_exp, _, de2 = w_up.shape
    de = de2 // 2
    blk = lambda i, m: (jnp.minimum(i, m[2 * nb] - 1), 0)
    exp = lambda i, m: (layer, m[i], 0, 0)
    grid_spec = pltpu.PrefetchScalarGridSpec(
        num_scalar_prefetch=1, grid=(nb,),
        in_specs=[pl.BlockSpec((MOE_BM, d), blk),
                  pl.BlockSpec((1, 1, d, de2), exp), pl.BlockSpec((1, 1, 1, de2), exp),
                  pl.BlockSpec((1, 1, de, d), exp), pl.BlockSpec((1, 1, 1, d), exp)],
        out_specs=pl.BlockSpec((MOE_BM, d), lambda i, m: (i, 0)),
        scratch_shapes=[pltpu.VMEM((d, de2), BF16), pltpu.VMEM((de, d), BF16)],
    )
    return pl.pallas_call(
        _experts_kernel, grid_spec=grid_spec, out_shape=jax.ShapeDtypeStruct((n_slots, d), F32),
        compiler_params=_params(("arbitrary",)), name="moe_experts",
    )(meta, xs, w_up, b_up.reshape(depth, n_exp, 1, de2), w_down, b_down.reshape(depth, n_exp, 1, d))


def _combine_kernel(dest_ref, dest_next_ref, gk_ref, ys_ref, o_ref, ybuf, sems):
    tc, d = o_ref.shape
    i = pl.program_id(0)
    n = pl.num_programs(0)
    cur = i % 2

    def fetch(idx_ref, buf):
        def rows(tok, carry):
            for kk in range(TOP_K):
                pltpu.make_async_copy(ys_ref.at[pl.ds(idx_ref[0, kk, tok], 1)], ybuf.at[buf, kk, pl.ds(tok, 1)],
                                      sems.at[buf]).start(priority=kk % 2)
            return carry

        lax.fori_loop(0, tc, rows, 0, unroll=8)

    @pl.when(i == 0)
    def _():
        fetch(dest_ref, 0)

    for kk in range(TOP_K):
        pltpu.make_async_copy(ys_ref.at[pl.ds(0, tc)], ybuf.at[cur, kk], sems.at[cur]).wait()
    ycur = ybuf.at[cur]

    hi, mid, lo = _split3(gk_ref[0])
    pick = lax.broadcasted_iota(jnp.int32, (8, 128), 0)
    cols = []
    for kk in range(TOP_K):
        sel = _onehot(pick == kk)
        cols.append(_dot_tn(hi, sel) + _dot_tn(mid, sel) + _dot_tn(lo, sel))
    for j in range(d // 128):
        lanes = slice(j * 128, (j + 1) * 128)
        acc = cols[0] * ycur[0, :, lanes]
        for kk in range(1, TOP_K):
            acc = acc + cols[kk] * ycur[kk, :, lanes]
        o_ref[:, lanes] = acc

    @pl.when(i + 1 < n)
    def _():
        fetch(dest_next_ref, 1 - cur)


def _combine(dest, gk, ys):
    n_tiles, _, tm = dest.shape
    d = ys.shape[1]
    per = tm // MOE_TC
    n = n_tiles * per
    blk = lambda i: (i // per, 0, i % per)
    nxt = lambda i: blk(jnp.minimum(i + 1, n - 1))
    return pl.pallas_call(
        _combine_kernel, grid=(n,),
        in_specs=[pl.BlockSpec((1, 8, MOE_TC), blk, memory_space=pltpu.SMEM),
                  pl.BlockSpec((1, 8, MOE_TC), nxt, memory_space=pltpu.SMEM), pl.BlockSpec((1, 8, MOE_TC), blk),
                  pl.BlockSpec(memory_space=pl.ANY)],
        out_specs=pl.BlockSpec((MOE_TC, d), lambda i: (i, 0)),
        out_shape=jax.ShapeDtypeStruct((n_tiles * tm, d), F32),
        scratch_shapes=[pltpu.VMEM((2, TOP_K, MOE_TC, d), F32), pltpu.SemaphoreType.DMA((2,))],
        compiler_params=_params(("arbitrary",)), name="moe_combine",
    )(dest, dest, gk, ys)


def _moe(layer, f, pos_t, gate_t, cnt, tril, w_up, b_up, w_down, b_down):
    t = f.shape[0]
    nb = -(-(t * TOP_K + N_EXPERTS * (MOE_BM - 1)) // MOE_BM)
    per_expert = jnp.sum(cnt, axis=0)
    blocks_e = (per_expert + MOE_BM - 1) // MOE_BM
    blk_end = jnp.cumsum(blocks_e)
    blk_start = blk_end - blocks_e
    base = (blk_start * MOE_BM)[None, :] + jnp.cumsum(cnt, axis=0) - cnt
    live = blk_end[-1]
    blk_id = jnp.minimum(jnp.arange(nb, dtype=jnp.int32), live - 1)
    blk_expert = jnp.sum((blk_end[None, :] <= blk_id[:, None]).astype(jnp.int32), axis=1)
    first = (jnp.arange(nb, dtype=jnp.int32) == blk_start[blk_expert]).astype(jnp.int32)
    meta = jnp.concatenate([blk_expert, first, live[None]]).astype(jnp.int32)
    pad_info = jnp.concatenate([jnp.maximum(blk_end - 1, 0) * MOE_BM, (blocks_e > 0).astype(jnp.int32),
                                live[None]]).astype(jnp.int32)
    base_b = jnp.broadcast_to(base[:, :, None], base.shape + (128,)).astype(jnp.int32)

    dest, gk = _slots(pos_t, gate_t, base_b, tril)
    xs = _dispatch(pad_info, f, dest, nb * MOE_BM)
    ys = _experts(layer, meta, xs, w_up, b_up, w_down, b_down)
    return _combine(dest, gk, ys)


def _final_kernel(x_ref, ffn_ref, mod_ref, g_ref, o_ref):
    x = x_ref[0] + mod_ref[0, 0, 5:6, :] * ffn_ref[0]
    ms = jnp.mean(x * x, axis=-1, keepdims=True)
    o_ref[0] = x * lax.rsqrt(ms + NORM_EPS) * g_ref[...]


def _final(layer, x, ffn, mod_t, gain, n_ctx_tiles, l):
    b, s, d = x.shape
    tok = lambda bb, i: (bb, i + n_ctx_tiles, 0)
    return pl.pallas_call(
        _final_kernel, grid=(b, l // TQ),
        in_specs=[pl.BlockSpec((1, TQ, d), tok), pl.BlockSpec((1, TQ, d), tok),
                  pl.BlockSpec((1, 1, 6, d), lambda bb, i: (layer, bb, 0, 0)),
                  pl.BlockSpec((1, d), lambda bb, i: (0, 0))],
        out_specs=pl.BlockSpec((1, TQ, d), lambda bb, i: (bb, i, 0)),
        out_shape=jax.ShapeDtypeStruct((b, l, d), F32),
        compiler_params=_params(("arbitrary", "arbitrary")), name="final_norm",
    )(x, ffn, mod_t, gain)


def _in_columns():
    sizes = (GQA_HEADS * HEAD_DIM, GQA_KV_HEADS * HEAD_DIM, GQA_KV_HEADS * HEAD_DIM, MIXER_W, MIXER_W,
             MIXER_W, MIXER_W, MIXER_W, 3 * MIXER_W, MIXER_W, 2 * DN_HEADS, 2 * DN_HEADS)
    starts = np.concatenate([[0], np.cumsum(sizes)[:-1]])
    aq, ak, av, bu, bv, cq, ck, cv, dqkv, dz, db, da = (np.arange(n) + o for n, o in zip(sizes, starts))
    grp = GQA_HEADS // GQA_KV_HEADS
    expand = np.concatenate([np.arange(HEAD_DIM) + (hd // grp) * HEAD_DIM for hd in range(GQA_HEADS)])
    cols = [aq, ak[expand], av[expand], bu, bv, cq, ck, cv, dqkv, dz, db, da, np.full(128 - 4 * DN_HEADS, -1)]
    return np.concatenate(cols)


def _take_cols(w, cols):
    safe = np.where(cols < 0, 0, cols)
    return jnp.where(jnp.asarray(cols >= 0)[None, :], w[:, safe], 0.0)


def _rope_tables(l, lc):
    rows = l // GRID_W
    r_idx, c_idx = np.meshgrid(np.arange(rows), np.arange(GRID_W), indexing="ij")
    row_pos = jnp.asarray(r_idx.reshape(-1), F32)
    col_pos = jnp.asarray(c_idx.reshape(-1), F32)

    def table(dim, reps):
        n = dim // 4
        inv = jnp.power(ROPE_THETA, -jnp.arange(n, dtype=F32) / n)
        ang = jnp.concatenate([row_pos[:, None] * inv, col_pos[:, None] * inv], axis=-1)
        cos, sin = jnp.cos(ang), jnp.sin(ang)
        cos_t = jnp.tile(jnp.concatenate([cos, cos], axis=-1), (1, reps))
        sin_t = jnp.tile(jnp.concatenate([-sin, sin], axis=-1), (1, reps))
        return (jnp.concatenate([jnp.ones((lc, cos_t.shape[1]), F32), cos_t], axis=0),
                jnp.concatenate([jnp.zeros((lc, sin_t.shape[1]), F32), sin_t], axis=0))

    cos_a, sin_a = table(HEAD_DIM, GQA_HEADS)
    cos_d, sin_d = table(DIFF_DIM, 2 * DIFF_HEADS)
    return jnp.stack([cos_a, sin_a, cos_d, sin_d])


def kernel(x, c, ctx, c_ctx, w_ada, b_ada, norm_mix, norm_ffn, w_in, w_out, gqa_q_norm, gqa_k_norm, gmlp_v_norm, gmlp_w_s, gmlp_b_s, diff_lambda_q1, diff_lambda_k1, diff_lambda_q2, diff_lambda_k2, diff_subln, dn_conv_w, dn_a_log, dn_dt_bias, dn_out_norm, router_w, router_b, exp_w_up, exp_b_up, exp_w_down, exp_b_down, final_norm):
    b, l, d = x.shape
    lc = ctx.shape[1]
    depth = w_ada.shape[0]
    s = lc + l
    assert lc % TQ == 0 and l % TQ == 0 and (b * s) % MOE_TM == 0 and l % GRID_W == 0
    n_ctx_tiles = lc // TQ

    rows = -(-(b + 1) // 8) * 8
    c_all = jnp.zeros((rows, d), F32).at[:b].set(c).at[b].set(c_ctx)
    mod_t = _modulation(c_all, w_ada, b_ada).transpose(0, 2, 1, 3)

    cols = _in_columns()
    rope = _rope_tables(l, lc)
    lane = np.arange(MIXER_W)
    g64 = jnp.asarray((lane[:, None] // 64) == (lane[None, :] // 64), BF16)
    tri = jnp.asarray(np.arange(MOE_TM)[:, None] < np.arange(MOE_TM)[None, :], BF16)
    tril = jnp.asarray(np.arange(N_EXPERTS)[:, None] > np.arange(N_EXPERTS)[None, :], BF16)

    xs = jnp.concatenate([ctx, x], axis=1)
    ffn = None
    for layer in range(depth):
        lam_init = 0.8 - 0.6 * math.exp(-0.3 * layer)
        w_ext = _take_cols(w_in[layer], cols).astype(BF16)
        gq = jnp.tile(gqa_q_norm[layer], GQA_HEADS)
        gk = jnp.tile(gqa_k_norm[layer], GQA_HEADS)
        vecs = jnp.zeros((8, MIXER_W), F32).at[0].set(gq).at[1].set(gk).at[2].set(gmlp_v_norm[layer])
        bst = jnp.repeat(gmlp_b_s[layer].T, GMLP_CH, axis=1)
        xs, (qa, ka, va, out_b, qd, kd, vd, dqkv, dz, dba) = _inproj(
            layer, xs, ffn, mod_t, norm_mix[layer][None, :], w_ext, vecs, g64, rope,
            gmlp_w_s[layer].astype(BF16), bst, n_ctx_tiles)

        out_a = _attention("gqa", qa, ka, va, n_ctx_tiles, lc)
        lam_p = jnp.stack([diff_lambda_q1[layer], diff_lambda_k1[layer], diff_lambda_q2[layer], diff_lambda_k2[layer]])
        sub = jnp.tile(diff_subln[layer], DIFF_HEADS)[None, :]
        out_c = _attention("diff", qd, kd, vd, n_ctx_tiles, lc, extra=(lam_p, sub, g64), lam_init=lam_init)

        avec = jnp.zeros((2, 128), F32)
        avec = avec.at[0, 2 * DN_HEADS:4 * DN_HEADS].set(dn_a_log[layer].reshape(-1))
        avec = avec.at[1, 2 * DN_HEADS:4 * DN_HEADS].set(dn_dt_bias[layer].reshape(-1))
        gq_, gk_, gv_, gkt, gbg, gbgt = _gdn_prep(dqkv, dba, dn_conv_w[layer], avec, g64, n_ctx_tiles)
        o_f, o_b = _gdn_scan(gq_, gk_, gv_, gkt, gbg, gbgt, lc)

        xs, f, logits_t = _outproj(
            layer, xs, out_a, out_b, out_c, o_f, o_b, dz, mod_t,
            w_out[layer].reshape(4, MIXER_W, d).astype(BF16), jnp.tile(dn_out_norm[layer], DN_HEADS)[None, :], g64,
            norm_ffn[layer][None, :], router_w[layer].T, jnp.broadcast_to(router_b[layer][:, None], (N_EXPERTS, 128)),
            n_ctx_tiles)

        pos_t, gate_t, cnt = _route(logits_t, tri)
        ffn = _moe(layer, f, pos_t, gate_t, cnt[:, :, 0], tril, exp_w_up, exp_b_up, exp_w_down, exp_b_down)
        ffn = ffn.reshape(b, s, d)

    return _final(depth - 1, xs, ffn, mod_t, final_norm[None, :], n_ctx_tiles, l)
```
